```python
import math
import jax, jax.numpy as jnp
from jax import lax
import numpy as np

D_MODEL = 1024
BATCH = 16
SEQ = 256
DEPTH = 4
DEC_BATCH = 8
DEC_SEQ = 4096
PAST_LEN = 512

GRID_W = 64
N_MIXERS = 2
EXPAND = 2
D_INNER = EXPAND * D_MODEL
N_HEADS = 8
HEAD_DIM = D_INNER // (2 * N_HEADS)
V_DIM = 2 * HEAD_DIM
AXIS_DIM = HEAD_DIM // 2
ROPE_BASE = 10000.0
N_FOURIER_GROUPS = 8
FOURIER_GROUP_DIM = D_INNER // N_FOURIER_GROUPS
N_FOURIER_LAYERS = (DEPTH + 1) // 2
N_ATTN_LAYERS = DEPTH // 2
Q_BLOCK = 128
EPS = 1e-6
LAMBDA_STD = 0.1

kernel_name = 'hybrid_fnet_diffattn_dit_step'


def rms_norm(x, g):
    xf = x.astype(jnp.float32)
    y = xf * lax.rsqrt(jnp.mean(xf * xf, axis=-1, keepdims=True) + EPS)
    return (y * g.astype(jnp.float32)).astype(x.dtype)


def adaln_params(cond, w_ada, b_ada):
    m = jax.nn.silu(cond) @ w_ada + b_ada
    return jnp.split(m, 3, axis=-1)


def axial_rope_tables(n):
    rows = n // GRID_W
    row = jnp.broadcast_to(jnp.arange(rows, dtype=jnp.float32)[:, None], (rows, GRID_W)).reshape(-1)
    col = jnp.broadcast_to(jnp.arange(GRID_W, dtype=jnp.float32)[None, :], (rows, GRID_W)).reshape(-1)
    inv = ROPE_BASE ** (-jnp.arange(0, AXIS_DIM, 2, dtype=jnp.float32) / AXIS_DIM)
    ar = row[:, None] * inv[None, :]
    ac = col[:, None] * inv[None, :]
    ang = jnp.concatenate([ar, ar, ac, ac], axis=-1)
    return jnp.cos(ang), jnp.sin(ang)


def apply_rope(x, cos, sin):
    xr = x.reshape(x.shape[:-1] + (2, 2, AXIS_DIM // 2))
    rot = jnp.stack([-xr[..., 1, :], xr[..., 0, :]], axis=-2).reshape(x.shape)
    c = cos[:, None, None, :].astype(x.dtype)
    s = sin[:, None, None, :].astype(x.dtype)
    return x * c + rot * s


def gated_out(o, g, w_out):
    return (o * jax.nn.silu(g)) @ w_out


def fourier_mix(u):
    b, n, e = u.shape
    ug = u.reshape(b, n, N_FOURIER_GROUPS, FOURIER_GROUP_DIM).astype(jnp.float32)
    f = jnp.fft.fft2(ug, axes=(1, 3), norm='ortho').real
    return f.reshape(b, n, e).astype(u.dtype)


def fourier_layer(h, w_in, w_out):
    u, g = jnp.split(h @ w_in, 2, axis=-1)
    return gated_out(fourier_mix(u), g, w_out)


def attn_project(h, w_in):
    b, n, _ = h.shape
    q, k, v, g = jnp.split(h @ w_in, 4, axis=-1)
    return (q.reshape(b, n, N_HEADS, 2, HEAD_DIM),
            k.reshape(b, n, N_HEADS, 2, HEAD_DIM),
            v.reshape(b, n, N_HEADS, V_DIM),
            g)


def diff_attend(q, k, v, lam):
    b, nq = q.shape[:2]
    qb = min(Q_BLOCK, nq)
    nblk = nq // qb
    qblocks = jnp.moveaxis(q.reshape(b, nblk, qb, N_HEADS, 2, HEAD_DIM), 1, 0)
    scale = HEAD_DIM ** -0.5

    def block(qi):
        s = jnp.einsum('bqhcd,bkhcd->cbhqk', qi, k).astype(jnp.float32) * scale
        p = jax.nn.softmax(s, axis=-1)
        a = (p[0] - lam * p[1]).astype(v.dtype)
        return jnp.einsum('bhqk,bkhv->bqhv', a, v)

    out = lax.map(block, qblocks)
    return jnp.moveaxis(out, 0, 1).reshape(b, nq, N_HEADS, V_DIM)


def diff_head_out(o, g, subln, lam_init, w_out):
    b, n = o.shape[:2]
    o = rms_norm(o, subln) * (1.0 - lam_init)
    return gated_out(o.reshape(b, n, D_INNER), g, w_out)


def setup_inputs(seed: int = 0) -> dict:
    key = jax.random.key(seed)
    ks = jax.random.split(key, 19)
    nrm = jax.random.normal
    f32 = jnp.float32
    return {
        'x_prompt': nrm(ks[0], (BATCH, SEQ, D_MODEL), f32),
        'x_sample': nrm(ks[1], (DEC_BATCH, DEC_SEQ, D_MODEL), f32),
        'cache_k': nrm(ks[2], (DEC_BATCH, N_ATTN_LAYERS, PAST_LEN, N_HEADS, 2 * HEAD_DIM), f32),
        'cache_v': nrm(ks[3], (DEC_BATCH, N_ATTN_LAYERS, PAST_LEN, N_HEADS, V_DIM), f32),
        'c': nrm(ks[4], (DEC_BATCH, D_MODEL), f32),
        'c_ctx': nrm(ks[5], (D_MODEL,), f32),
        'norm_g': 1.0 + 0.01 * nrm(ks[6], (DEPTH, D_MODEL), f32),
        'w_ada': 0.5 * D_MODEL ** -0.5 * nrm(ks[7], (DEPTH, D_MODEL, 3 * D_MODEL), f32),
        'b_ada': 0.01 * nrm(ks[8], (DEPTH, 3 * D_MODEL), f32),
        'w_in_fourier': D_MODEL ** -0.5 * nrm(ks[9], (N_FOURIER_LAYERS, D_MODEL, 2 * D_INNER), f32),
        'w_out_fourier': D_INNER ** -0.5 * nrm(ks[10], (N_FOURIER_LAYERS, D_INNER, D_MODEL), f32),
        'w_in_attn': D_MODEL ** -0.5 * nrm(ks[11], (N_ATTN_LAYERS, D_MODEL, 4 * D_INNER), f32),
        'w_out_attn': D_INNER ** -0.5 * nrm(ks[12], (N_ATTN_LAYERS, D_INNER, D_MODEL), f32),
        'lam_q1': LAMBDA_STD * nrm(ks[13], (N_ATTN_LAYERS, HEAD_DIM), f32),
        'lam_k1': LAMBDA_STD * nrm(ks[14], (N_ATTN_LAYERS, HEAD_DIM), f32),
        'lam_q2': LAMBDA_STD * nrm(ks[15], (N_ATTN_LAYERS, HEAD_DIM), f32),
        'lam_k2': LAMBDA_STD * nrm(ks[16], (N_ATTN_LAYERS, HEAD_DIM), f32),
        'subln_g': 1.0 + 0.01 * nrm(ks[17], (N_ATTN_LAYERS, V_DIM), f32),
        'final_norm_g': 1.0 + 0.01 * nrm(ks[18], (D_MODEL,), f32),
    }


def reference(x_prompt, x_sample, cache_k, cache_v, c, c_ctx, norm_g, w_ada, b_ada,
              w_in_fourier, w_out_fourier, w_in_attn, w_out_attn,
              lam_q1, lam_k1, lam_q2, lam_k2, subln_g, final_norm_g):
    dec_b, n_lat = x_sample.shape[:2]
    past_len = cache_k.shape[2]
    cos, sin = axial_rope_tables(n_lat)
    xp, xs = x_prompt, x_sample
    new_k, new_v = [], []
    for i in range(DEPTH):
        sh_p, sc_p, gt_p = adaln_params(c_ctx, w_ada[i], b_ada[i])
        sh_s, sc_s, gt_s = adaln_params(c, w_ada[i], b_ada[i])
        sh_s, sc_s, gt_s = sh_s[:, None, :], sc_s[:, None, :], gt_s[:, None, :]
        hp = rms_norm(xp, norm_g[i]) * (1.0 + sc_p) + sh_p
        hs = rms_norm(xs, norm_g[i]) * (1.0 + sc_s) + sh_s
        j = i // N_MIXERS
        if i % N_MIXERS == 0:
            op = fourier_layer(hp, w_in_fourier[j], w_out_fourier[j])
            os_ = fourier_layer(hs, w_in_fourier[j], w_out_fourier[j])
        else:
            lam_init = 0.8 - 0.6 * math.exp(-0.3 * i)
            lam = (jnp.exp(jnp.sum(lam_q1[j].astype(jnp.float32) * lam_k1[j].astype(jnp.float32)))
                   - jnp.exp(jnp.sum(lam_q2[j].astype(jnp.float32) * lam_k2[j].astype(jnp.float32)))
                   + lam_init)
            qp, kp, vp, gp = attn_project(hp, w_in_attn[j])
            op = diff_head_out(diff_attend(qp, kp, vp, lam), gp, subln_g[j], lam_init, w_out_attn[j])
            new_k.append(kp.reshape(kp.shape[:2] + (N_HEADS, 2 * HEAD_DIM)))
            new_v.append(vp)
            qs, ks_, vs, gs = attn_project(hs, w_in_attn[j])
            qs = apply_rope(qs, cos, sin)
            ks_ = apply_rope(ks_, cos, sin)
            k_ctx = cache_k[:, j].reshape(dec_b, past_len, N_HEADS, 2, HEAD_DIM)
            k_all = jnp.concatenate([ks_, k_ctx], axis=1)
            v_all = jnp.concatenate([vs, cache_v[:, j]], axis=1)
            os_ = diff_head_out(diff_attend(qs, k_all, v_all, lam), gs, subln_g[j], lam_init, w_out_attn[j])
        xp = xp + gt_p * op
        xs = xs + gt_s * os_
    y_prompt = rms_norm(xp, final_norm_g)
    y_sample = rms_norm(xs, final_norm_g)
    new_cache_k = jnp.stack(new_k, axis=1)
    new_cache_v = jnp.stack(new_v, axis=1)
    return (y_prompt, y_sample, new_cache_k, new_cache_v)
```

```python
import functools
import math

import jax
import jax.numpy as jnp
from jax import lax
from jax.experimental import pallas as pl
from jax.experimental.pallas import tpu as pltpu

N_HEADS = 8
N_FOURIER_GROUPS = 8
GRID_W = 64
ROPE_BASE = 10000.0
EPS = 1e-6
N_MIXERS = 2

V7X_VMEM_BYTES = 64 * 1024 * 1024
VMEM_LIMIT_BYTES = V7X_VMEM_BYTES - 8 * 1024 * 1024
LANES = 128

BF16 = jnp.bfloat16
F32 = jnp.float32


def _tile(n, pref):
    t = min(n, pref)
    assert n % t == 0, (n, t)
    return t


def _params(n_axes):
    return pltpu.CompilerParams(
        dimension_semantics=("arbitrary",) * n_axes,
        vmem_limit_bytes=VMEM_LIMIT_BYTES,
    )


def _resident(block_shape, index_map):
    return pl.BlockSpec(block_shape, index_map, pipeline_mode=pl.Buffered(1))


def _adaln_kernel(cond_ref, w_ref, b_ref, out_ref):
    cond = cond_ref[...]
    act = cond * jax.nn.sigmoid(cond)
    out_ref[...] = jnp.dot(act, w_ref[...], preferred_element_type=F32,
                           precision=lax.Precision.HIGHEST) + b_ref[...]


def _adaln(cond, w_ada, b_ada):
    depth, d, d3 = w_ada.shape
    r = cond.shape[0]
    tn = _tile(d3, 1024)
    return pl.pallas_call(
        _adaln_kernel,
        grid=(depth, d3 // tn),
        in_specs=[
            pl.BlockSpec((r, d), lambda i, j: (0, 0)),
            pl.BlockSpec((None, d, tn), lambda i, j: (i, 0, j)),
            pl.BlockSpec((None, 1, tn), lambda i, j: (i, 0, j)),
        ],
        out_specs=pl.BlockSpec((None, r, tn), lambda i, j: (i, 0, j)),
        out_shape=jax.ShapeDtypeStruct((depth, r, d3), F32),
        compiler_params=_params(2),
        name="adaln",
    )(cond, w_ada, b_ada.reshape(depth, 1, d3))


def _modulated_norm(x, ng, sc, sh):
    ms = jnp.mean(x * x, axis=-1, keepdims=True)
    h = (x * lax.rsqrt(ms + EPS)) * (ng * (1.0 + sc)) + sh
    return h.astype(BF16)


def _silu(x):
    return x * jax.nn.sigmoid(x)


def _fourier_in_kernel(x_ref, ng_ref, sc_ref, sh_ref, w_ref, cs_ref, ab_ref, sg_ref, *, e, gd):
    hb = _modulated_norm(x_ref[...], ng_ref[...], sc_ref[...], sh_ref[...])
    for grp in range(e // gd):
        cols = slice(grp * gd, (grp + 1) * gd)
        u = jnp.dot(hb, w_ref[:, cols], preferred_element_type=F32)
        ab = jnp.dot(u.astype(BF16), cs_ref[...], preferred_element_type=F32)
        ab_ref[0, :, cols] = ab[:, :gd].astype(BF16)
        ab_ref[1, :, cols] = ab[:, gd:].astype(BF16)
    tn = _tile(e, 512)
    for j in range(e // tn):
        cols = slice(j * tn, (j + 1) * tn)
        g = jnp.dot(hb, w_ref[:, e + j * tn:e + (j + 1) * tn], preferred_element_type=F32)
        sg_ref[:, cols] = _silu(g).astype(BF16)


def _fourier_in(x, ng, sc, sh, w_in, cs):
    b, n, d = x.shape
    e = w_in.shape[1] // 2
    gd = e // N_FOURIER_GROUPS
    tm = _tile(n, 512)
    return pl.pallas_call(
        functools.partial(_fourier_in_kernel, e=e, gd=gd),
        grid=(b, n // tm),
        in_specs=[
            pl.BlockSpec((None, tm, d), lambda bi, i: (bi, i, 0)),
            _resident((1, d), lambda bi, i: (0, 0)),
            pl.BlockSpec((None, 1, d), lambda bi, i: (bi, 0, 0)),
            pl.BlockSpec((None, 1, d), lambda bi, i: (bi, 0, 0)),
            _resident((d, 2 * e), lambda bi, i: (0, 0)),
            _resident((gd, 2 * gd), lambda bi, i: (0, 0)),
        ],
        out_specs=[
            pl.BlockSpec((None, 2, tm, e), lambda bi, i: (bi, 0, i, 0)),
            pl.BlockSpec((None, tm, e), lambda bi, i: (bi, i, 0)),
        ],
        out_shape=[
            jax.ShapeDtypeStruct((b, 2, n, e), BF16),
            jax.ShapeDtypeStruct((b, n, e), BF16),
        ],
        compiler_params=_params(2),
        name="fourier_in",
    )(x, ng, sc, sh, w_in, cs)


def _seq_dft_kernel(d_ref, ab_ref, sg_ref, out_ref, acc_ref):
    k = pl.program_id(2)
    part = jnp.dot(d_ref[...], ab_ref[...], preferred_element_type=F32)

    @pl.when(k == 0)
    def _():
        acc_ref[...] = part

    @pl.when(k > 0)
    def _():
        acc_ref[...] += part

    @pl.when(k == pl.num_programs(2) - 1)
    def _():
        out_ref[...] = (acc_ref[...] * sg_ref[...].astype(F32)).astype(BF16)


def _seq_dft(dmat, ab, sg):
    b, n2, e = ab.shape
    n = n2 // 2
    tm = _tile(n, 1024)
    tk = _tile(n2, 1024)
    return pl.pallas_call(
        _seq_dft_kernel,
        grid=(b, n // tm, n2 // tk),
        in_specs=[
            pl.BlockSpec((tm, tk), lambda bi, i, k: (i, k)),
            pl.BlockSpec((None, tk, e), lambda bi, i, k: (bi, k, 0)),
            pl.BlockSpec((None, tm, e), lambda bi, i, k: (bi, i, 0)),
        ],
        out_specs=pl.BlockSpec((None, tm, e), lambda bi, i, k: (bi, i, 0)),
        out_shape=jax.ShapeDtypeStruct((b, n, e), BF16),
        scratch_shapes=[pltpu.VMEM((tm, e), F32)],
        compiler_params=_params(3),
        name="seq_dft",
    )(dmat, ab, sg)


def _dft_cos_sin(n):
    k = jnp.arange(n, dtype=jnp.int32)
    if n >= 1024 and n % 64 == 0:
        m1 = jnp.arange(n // 64, dtype=jnp.int32) * 64
        m2 = jnp.arange(64, dtype=jnp.int32)
        ang_a = ((k[:, None] * m1[None, :]) % n).astype(F32) * (2.0 * math.pi / n)
        ang_b = ((k[:, None] * m2[None, :]) % n).astype(F32) * (2.0 * math.pi / n)
        ca, sa = jnp.cos(ang_a)[:, :, None], jnp.sin(ang_a)[:, :, None]
        cb, sb = jnp.cos(ang_b)[:, None, :], jnp.sin(ang_b)[:, None, :]
        return (ca * cb - sa * sb).reshape(n, n), (sa * cb + ca * sb).reshape(n, n)
    ang = ((k[:, None] * k[None, :]) % n).astype(F32) * (2.0 * math.pi / n)
    return jnp.cos(ang), jnp.sin(ang)


def _seq_dft_matrix(n):
    c, s = _dft_cos_sin(n)
    return (jnp.concatenate([c, -s], axis=1) * (n ** -0.5)).astype(BF16)


def _chan_dft_matrix(gd):
    c, s = _dft_cos_sin(gd)
    return (jnp.concatenate([c, s], axis=1) * (gd ** -0.5)).astype(BF16)


def _rope(x, cos, sin_signed, first_half):
    fwd = pltpu.roll(x, LANES - 32, axis=1)
    bwd = pltpu.roll(x, 32, axis=1)
    return x * cos + jnp.where(first_half, fwd, bwd) * sin_signed


def _attn_in_kernel(*refs, e, hd, rope, q_scale):
    if rope:
        (x_ref, ng_ref, sc_ref, sh_ref, w_ref, cq_ref, sq_ref, ck_ref, sk_ref,
         q_ref, k_ref, v_ref, sg_ref) = refs
        lane = lax.broadcasted_iota(jnp.int32, (1, LANES), 1)
        first_half = (lane % 64) < 32
    else:
        x_ref, ng_ref, sc_ref, sh_ref, w_ref, q_ref, k_ref, v_ref, sg_ref = refs
    hb = _modulated_norm(x_ref[...], ng_ref[...], sc_ref[...], sh_ref[...])
    for j in range(e // hd):
        cols = slice(j * hd, (j + 1) * hd)
        q = jnp.dot(hb, w_ref[:, cols], preferred_element_type=F32)
        k = jnp.dot(hb, w_ref[:, e + j * hd:e + (j + 1) * hd], preferred_element_type=F32)
        if rope:
            q = _rope(q, cq_ref[...], sq_ref[...], first_half)
            k = _rope(k, ck_ref[...], sk_ref[...], first_half)
        else:
            q = q * q_scale
        q_ref[:, cols] = q.astype(q_ref.dtype)
        k_ref[:, cols] = k.astype(k_ref.dtype)
    tn = _tile(e, 512)
    for j in range(e // tn):
        cols = slice(j * tn, (j + 1) * tn)
        v = jnp.dot(hb, w_ref[:, 2 * e + j * tn:2 * e + (j + 1) * tn], preferred_element_type=F32)
        v_ref[:, cols] = v.astype(v_ref.dtype)
        g = jnp.dot(hb, w_ref[:, 3 * e + j * tn:3 * e + (j + 1) * tn], preferred_element_type=F32)
        sg_ref[:, cols] = _silu(g).astype(BF16)


def _attn_in(x, ng, sc, sh, w_in, q_scale, rope_tables, kv_dtype):
    b, n, d = x.shape
    e = w_in.shape[1] // 4
    hd = e // (2 * N_HEADS)
    assert hd == LANES
    tm = _tile(n, 512)
    rope = rope_tables is not None
    in_specs = [
        pl.BlockSpec((None, tm, d), lambda bi, i: (bi, i, 0)),
        _resident((1, d), lambda bi, i: (0, 0)),
        pl.BlockSpec((None, 1, d), lambda bi, i: (bi, 0, 0)),
        pl.BlockSpec((None, 1, d), lambda bi, i: (bi, 0, 0)),
        _resident((d, 4 * e), lambda bi, i: (0, 0)),
    ]
    args = [x, ng, sc, sh, w_in]
    if rope:
        in_specs += [pl.BlockSpec((tm, hd), lambda bi, i: (i, 0))] * 4
        args += list(rope_tables)
    tok = pl.BlockSpec((None, tm, e), lambda bi, i: (bi, i, 0))
    return pl.pallas_call(
        functools.partial(_attn_in_kernel, e=e, hd=hd, rope=rope, q_scale=q_scale),
        grid=(b, n // tm),
        in_specs=in_specs,
        out_specs=[tok, tok, tok, tok],
        out_shape=[
            jax.ShapeDtypeStruct((b, n, e), BF16),
            jax.ShapeDtypeStruct((b, n, e), kv_dtype),
            jax.ShapeDtypeStruct((b, n, e), kv_dtype),
            jax.ShapeDtypeStruct((b, n, e), BF16),
        ],
        compiler_params=_params(2),
        name="attn_in_rope" if rope else "attn_in",
    )(*args)


def _diff_attn_kernel(*refs, n_new, n_past, ck, hd, lam_init):
    if n_past:
        (q_ref, k_ref, v_ref, pk_ref, pv_ref, sg_ref, lam_ref, sub_ref,
         out_ref, s1_ref, s2_ref, pkb_ref, pvb_ref) = refs

        @pl.when(pl.program_id(2) == 0)
        def _():
            pkb_ref[...] = pk_ref[...].astype(BF16)
            pvb_ref[...] = pv_ref[...].astype(BF16)
    else:
        q_ref, k_ref, v_ref, sg_ref, lam_ref, sub_ref, out_ref, s1_ref, s2_ref = refs

    chunks = [(k_ref, v_ref, o, ck) for o in range(0, n_new, ck)]
    if n_past:
        chunks += [(pkb_ref, pvb_ref, o, min(ck, n_past - o)) for o in range(0, n_past, ck)]

    tq = q_ref.shape[0]
    nt = (((1,), (1,)), ((), ()))
    q = q_ref[...]
    q1, q2 = q[:, :hd], q[:, hd:]

    m1 = jnp.full((tq, LANES), -jnp.inf, F32)
    m2 = jnp.full((tq, LANES), -jnp.inf, F32)
    col = 0
    for kr, _, off, rows in chunks:
        kc = kr[off:off + rows, :].astype(BF16)
        s1 = lax.dot_general(q1, kc[:, :hd], nt, preferred_element_type=F32)
        s2 = lax.dot_general(q2, kc[:, hd:], nt, preferred_element_type=F32)
        s1_ref[:, col:col + rows] = s1
        s2_ref[:, col:col + rows] = s2
        for j in range(rows // LANES):
            m1 = jnp.maximum(m1, s1[:, j * LANES:(j + 1) * LANES])
            m2 = jnp.maximum(m2, s2[:, j * LANES:(j + 1) * LANES])
        col += rows
    m1 = jnp.max(m1, axis=-1, keepdims=True)
    m2 = jnp.max(m2, axis=-1, keepdims=True)

    vd = v_ref.shape[1]
    l1 = jnp.zeros((tq, LANES), F32)
    l2 = jnp.zeros((tq, LANES), F32)
    o1 = jnp.zeros((tq, vd), F32)
    o2 = jnp.zeros((tq, vd), F32)
    col = 0
    for _, vr, off, rows in chunks:
        vc = vr[off:off + rows, :].astype(BF16)
        p1 = jnp.exp2(s1_ref[:, col:col + rows] - m1)
        p2 = jnp.exp2(s2_ref[:, col:col + rows] - m2)
        for j in range(rows // LANES):
            l1 = l1 + p1[:, j * LANES:(j + 1) * LANES]
            l2 = l2 + p2[:, j * LANES:(j + 1) * LANES]
        o1 = o1 + jnp.dot(p1.astype(BF16), vc, preferred_element_type=F32)
        o2 = o2 + jnp.dot(p2.astype(BF16), vc, preferred_element_type=F32)
        col += rows
    l1 = jnp.sum(l1, axis=-1, keepdims=True)
    l2 = jnp.sum(l2, axis=-1, keepdims=True)

    lam_p = lam_ref[...]
    lam = (jnp.exp(jnp.sum(lam_p[0:1] * lam_p[1:2], axis=-1, keepdims=True))
           - jnp.exp(jnp.sum(lam_p[2:3] * lam_p[3:4], axis=-1, keepdims=True)) + lam_init)
    o = o1 * (1.0 / l1) - (lam / l2) * o2
    o = o * lax.rsqrt(jnp.mean(o * o, axis=-1, keepdims=True) + EPS) * sub_ref[...]
    out_ref[...] = ((o * (1.0 - lam_init)) * sg_ref[...].astype(F32)).astype(BF16)


def _diff_attn(q, k, v, sg, lam_p, subln, lam_init, past=None):
    b, n, e = q.shape
    hw = e // N_HEADS
    hd = hw // 2
    tq = _tile(n, 512)
    ck = _tile(n, 512)
    n_past = 0 if past is None else past[0].shape[2]
    tok = pl.BlockSpec((None, tq, hw), lambda bi, h, i: (bi, i, h))
    keys = pl.BlockSpec((None, n, hw), lambda bi, h, i: (bi, 0, h))
    in_specs = [tok, keys, keys]
    args = [q, k, v]
    scratch = [pltpu.VMEM((tq, n + n_past), F32), pltpu.VMEM((tq, n + n_past), F32)]
    if n_past:
        cache_k, cache_v, layer = past
        assert n_past % LANES == 0
        cached = pl.BlockSpec((None, None, n_past, hw), lambda bi, h, i: (bi, layer, 0, h))
        in_specs += [cached, cached]
        args += [cache_k, cache_v]
        scratch += [pltpu.VMEM((n_past, hw), BF16), pltpu.VMEM((n_past, hw), BF16)]
    in_specs += [tok,
                 _resident((4, hd), lambda bi, h, i: (0, 0)),
                 _resident((1, hw), lambda bi, h, i: (0, 0))]
    args += [sg, lam_p, subln]
    return pl.pallas_call(
        functools.partial(_diff_attn_kernel, n_new=n, n_past=n_past, ck=ck, hd=hd, lam_init=lam_init),
        grid=(b, N_HEADS, n // tq),
        in_specs=in_specs,
        out_specs=tok,
        out_shape=jax.ShapeDtypeStruct((b, n, e), BF16),
        scratch_shapes=scratch,
        compiler_params=_params(3),
        name="diff_attn_past" if n_past else "diff_attn",
    )(*args)


def _rope_tables(n, hd, q_scale):
    axis_dim = hd // 2
    rows = n // GRID_W
    row = jnp.broadcast_to(jnp.arange(rows, dtype=F32)[:, None], (rows, GRID_W)).reshape(-1)
    colp = jnp.broadcast_to(jnp.arange(GRID_W, dtype=F32)[None, :], (rows, GRID_W)).reshape(-1)
    inv = ROPE_BASE ** (-jnp.arange(0, axis_dim, 2, dtype=F32) / axis_dim)
    ar = row[:, None] * inv[None, :]
    ac = colp[:, None] * inv[None, :]
    ang = jnp.concatenate([ar, ar, ac, ac], axis=-1)
    cos, sin = jnp.cos(ang), jnp.sin(ang)
    half = axis_dim // 2
    sign = jnp.where((jnp.arange(hd) % axis_dim) < half, -1.0, 1.0).astype(F32)
    sin_signed = sin * sign[None, :]
    return cos * q_scale, sin_signed * q_scale, cos, sin_signed


def _out_proj_kernel(*refs, final):
    if final:
        a_ref, w_ref, x_ref, gt_ref, fg_ref, y_ref = refs
    else:
        a_ref, w_ref, x_ref, gt_ref, y_ref = refs
    y = x_ref[...] + gt_ref[...] * jnp.dot(a_ref[...], w_ref[...], preferred_element_type=F32)
    if final:
        y = (y * lax.rsqrt(jnp.mean(y * y, axis=-1, keepdims=True) + EPS)) * fg_ref[...]
    y_ref[...] = y


def _out_proj(a, w_out, x, gate, final_g=None):
    b, n, d = x.shape
    e = a.shape[2]
    tm = _tile(n, 512)
    final = final_g is not None
    in_specs = [
        pl.BlockSpec((None, tm, e), lambda bi, i: (bi, i, 0)),
        _resident((e, d), lambda bi, i: (0, 0)),
        pl.BlockSpec((None, tm, d), lambda bi, i: (bi, i, 0)),
        pl.BlockSpec((None, 1, d), lambda bi, i: (bi, 0, 0)),
    ]
    args = [a, w_out, x, gate]
    if final:
        in_specs.append(_resident((1, d), lambda bi, i: (0, 0)))
        args.append(final_g)
    return pl.pallas_call(
        functools.partial(_out_proj_kernel, final=final),
        grid=(b, n // tm),
        in_specs=in_specs,
        out_specs=pl.BlockSpec((None, tm, d), lambda bi, i: (bi, i, 0)),
        out_shape=jax.ShapeDtypeStruct((b, n, d), F32),
        compiler_params=_params(2),
        name="out_proj_final" if final else "out_proj",
    )(*args)


def kernel(x_prompt, x_sample, cache_k, cache_v, c, c_ctx, norm_g, w_ada, b_ada, w_in_fourier, w_out_fourier, w_in_attn, w_out_attn, lam_q1, lam_k1, lam_q2, lam_k2, subln_g, final_norm_g):
    depth, d = norm_g.shape
    bp, n_ctx, _ = x_prompt.shape
    bs, n_lat, _ = x_sample.shape
    past_len = cache_k.shape[2]
    e = w_out_attn.shape[1]
    hd = e // (2 * N_HEADS)
    gd = e // N_FOURIER_GROUPS

    n_cond = 1 + bs
    rows = -(-n_cond // 8) * 8
    cond = jnp.concatenate([c_ctx[None, :], c, jnp.zeros((rows - n_cond, d), F32)], axis=0)
    mods = _adaln(cond, w_ada, b_ada).reshape(depth, rows, 3, d)

    q_scale = hd ** -0.5 * math.log2(math.e)
    tables = _rope_tables(n_lat, hd, q_scale)
    cs = _chan_dft_matrix(gd)
    dmat_p = _seq_dft_matrix(n_ctx)
    dmat_s = _seq_dft_matrix(n_lat)
    cache_k4 = cache_k.reshape(bs, cache_k.shape[1], past_len, e)
    cache_v4 = cache_v.reshape(bs, cache_v.shape[1], past_len, e)

    xp, xs = x_prompt, x_sample
    new_k, new_v = [], []
    for i in range(depth):
        ng = norm_g[i][None, :]
        sh_p, sc_p, gt_p = (jnp.broadcast_to(mods[i, 0:1, t][:, None, :], (bp, 1, d)) for t in range(3))
        sh_s, sc_s, gt_s = (mods[i, 1:1 + bs, t][:, None, :] for t in range(3))
        j = i // N_MIXERS
        last = i == depth - 1
        fg = final_norm_g[None, :] if last else None
        if i % N_MIXERS == 0:
            w_in = w_in_fourier[j].astype(BF16)
            w_out = w_out_fourier[j].astype(BF16)
            ab_p, sg_p = _fourier_in(xp, ng, sc_p, sh_p, w_in, cs)
            ab_s, sg_s = _fourier_in(xs, ng, sc_s, sh_s, w_in, cs)
            a_p = _seq_dft(dmat_p, ab_p.reshape(bp, 2 * n_ctx, e), sg_p)
            a_s = _seq_dft(dmat_s, ab_s.reshape(bs, 2 * n_lat, e), sg_s)
        else:
            lam_init = 0.8 - 0.6 * math.exp(-0.3 * i)
            w_in = w_in_attn[j].astype(BF16)
            w_out = w_out_attn[j].astype(BF16)
            lam_p = jnp.stack([lam_q1[j], lam_k1[j], lam_q2[j], lam_k2[j]], axis=0)
            sub = subln_g[j][None, :]
            q_p, k_p, v_p, sg_p = _attn_in(xp, ng, sc_p, sh_p, w_in, q_scale, None, F32)
            new_k.append(k_p.reshape(bp, n_ctx, N_HEADS, 2 * hd))
            new_v.append(v_p.reshape(bp, n_ctx, N_HEADS, 2 * hd))
            a_p = _diff_attn(q_p, k_p, v_p, sg_p, lam_p, sub, lam_init)
            q_s, k_s, v_s, sg_s = _attn_in(xs, ng, sc_s, sh_s, w_in, q_scale, tables, BF16)
            a_s = _diff_attn(q_s, k_s, v_s, sg_s, lam_p, sub, lam_init,
                             past=(cache_k4, cache_v4, j))
        xp = _out_proj(a_p, w_out, xp, gt_p, fg)
        xs = _out_proj(a_s, w_out, xs, gt_s, fg)
    return (xp, xs, jnp.stack(new_k, axis=1), jnp.stack(new_v, axis=1))
```

```python
import functools
import math

import jax
import jax.numpy as jnp
from jax import lax
from jax.experimental import pallas as pl
from jax.experimental.pallas import tpu as pltpu

N_HEADS = 8
N_FOURIER_GROUPS = 8
GRID_W = 64
ROPE_BASE = 10000.0
EPS = 1e-6
N_MIXERS = 2

V7X_VMEM_BYTES = 64 * 1024 * 1024
VMEM_LIMIT_BYTES = V7X_VMEM_BYTES - 8 * 1024 * 1024
LANES = 128

BF16 = jnp.bfloat16
F32 = jnp.float32


def _tile(n, pref):
    t = min(n, pref)
    assert n % t == 0, (n, t)
    return t


def _params(n_axes):
    return pltpu.CompilerParams(
        dimension_semantics=("arbitrary",) * n_axes,
        vmem_limit_bytes=VMEM_LIMIT_BYTES,
    )


def _resident(block_shape, index_map):
    return pl.BlockSpec(block_shape, index_map, pipeline_mode=pl.Buffered(1))


def _adaln_kernel(cond_ref, w_ref, b_ref, out_ref):
    cond = cond_ref[...]
    act = cond * jax.nn.sigmoid(cond)
    out_ref[...] = jnp.dot(act, w_ref[...], preferred_element_type=F32,
                           precision=lax.Precision.HIGHEST) + b_ref[...]


def _adaln(cond, w_ada, b_ada):
    depth, d, d3 = w_ada.shape
    r = cond.shape[0]
    tn = _tile(d3, 1024)
    return pl.pallas_call(
        _adaln_kernel,
        grid=(depth, d3 // tn),
        in_specs=[
            pl.BlockSpec((r, d), lambda i, j: (0, 0)),
            pl.BlockSpec((None, d, tn), lambda i, j: (i, 0, j)),
            pl.BlockSpec((None, 1, tn), lambda i, j: (i, 0, j)),
        ],
        out_specs=pl.BlockSpec((None, r, tn), lambda i, j: (i, 0, j)),
        out_shape=jax.ShapeDtypeStruct((depth, r, d3), F32),
        compiler_params=_params(2),
        name="adaln",
    )(cond, w_ada, b_ada.reshape(depth, 1, d3))


def _modulated_norm(x, ng, sc, sh):
    ms = jnp.mean(x * x, axis=-1, keepdims=True)
    h = (x * lax.rsqrt(ms + EPS)) * (ng * (1.0 + sc)) + sh
    return h.astype(BF16)


def _silu(x):
    return x * jax.nn.sigmoid(x)


def _fourier_in_kernel(x_ref, ng_ref, sc_ref, sh_ref, w_ref, cs_ref, ab_ref, sg_ref, *, e, gd):
    hb = _modulated_norm(x_ref[...], ng_ref[...], sc_ref[...], sh_ref[...])
    for grp in range(e // gd):
        cols = slice(grp * gd, (grp + 1) * gd)
        u = jnp.dot(hb, w_ref[:, cols], preferred_element_type=F32)
        ab = jnp.dot(u.astype(BF16), cs_ref[...], preferred_element_type=F32)
        ab_ref[0, :, cols] = ab[:, :gd].astype(BF16)
        ab_ref[1, :, cols] = ab[:, gd:].astype(BF16)
    tn = _tile(e, 512)
    for j in range(e // tn):
        cols = slice(j * tn, (j + 1) * tn)
        g = jnp.dot(hb, w_ref[:, e + j * tn:e + (j + 1) * tn], preferred_element_type=F32)
        sg_ref[:, cols] = _silu(g).astype(BF16)


def _fourier_in(x, ng, sc, sh, w_in, cs):
    b, n, d = x.shape
    e = w_in.shape[1] // 2
    gd = e // N_FOURIER_GROUPS
    tm = _tile(n, 512)
    return pl.pallas_call(
        functools.partial(_fourier_in_kernel, e=e, gd=gd),
        grid=(b, n // tm),
        in_specs=[
            pl.BlockSpec((None, tm, d), lambda bi, i: (bi, i, 0)),
            _resident((1, d), lambda bi, i: (0, 0)),
            pl.BlockSpec((None, 1, d), lambda bi, i: (bi, 0, 0)),
            pl.BlockSpec((None, 1, d), lambda bi, i: (bi, 0, 0)),
            _resident((d, 2 * e), lambda bi, i: (0, 0)),
            _resident((gd, 2 * gd), lambda bi, i: (0, 0)),
        ],
        out_specs=[
            pl.BlockSpec((None, 2, tm, e), lambda bi, i: (bi, 0, i, 0)),
            pl.BlockSpec((None, tm, e), lambda bi, i: (bi, i, 0)),
        ],
        out_shape=[
            jax.ShapeDtypeStruct((b, 2, n, e), BF16),
            jax.ShapeDtypeStruct((b, n, e), BF16),
        ],
        compiler_params=_params(2),
        name="fourier_in",
    )(x, ng, sc, sh, w_in, cs)


def _seq_dft_kernel(d_ref, ab_ref, sg_ref, out_ref, acc_ref):
    k = pl.program_id(2)
    part = jnp.dot(d_ref[...], ab_ref[...], preferred_element_type=F32)

    @pl.when(k == 0)
    def _():
        acc_ref[...] = part

    @pl.when(k > 0)
    def _():
        acc_ref[...] += part

    @pl.when(k == pl.num_programs(2) - 1)
    def _():
        out_ref[...] = (acc_ref[...] * sg_ref[...].astype(F32)).astype(BF16)


def _seq_dft(dmat, ab, sg):
    b, n2, e = ab.shape
    n = n2 // 2
    tm = _tile(n, 1024)
    tk = _tile(n2, 1024)
    return pl.pallas_call(
        _seq_dft_kernel,
        grid=(b, n // tm, n2 // tk),
        in_specs=[
            pl.BlockSpec((tm, tk), lambda bi, i, k: (i, k)),
            pl.BlockSpec((None, tk, e), lambda bi, i, k: (bi, k, 0)),
            pl.BlockSpec((None, tm, e), lambda bi, i, k: (bi, i, 0)),
        ],
        out_specs=pl.BlockSpec((None, tm, e), lambda bi, i, k: (bi, i, 0)),
        out_shape=jax.ShapeDtypeStruct((b, n, e), BF16),
        scratch_shapes=[pltpu.VMEM((tm, e), F32)],
        compiler_params=_params(3),
        name="seq_dft",
    )(dmat, ab, sg)


def _dft_cos_sin(n):
    k = jnp.arange(n, dtype=jnp.int32)
    if n >= 1024 and n % 64 == 0:
        m1 = jnp.arange(n // 64, dtype=jnp.int32) * 64
        m2 = jnp.arange(64, dtype=jnp.int32)
        ang_a = ((k[:, None] * m1[None, :]) % n).astype(F32) * (2.0 * math.pi / n)
        ang_b = ((k[:, None] * m2[None, :]) % n).astype(F32) * (2.0 * math.pi / n)
        ca, sa = jnp.cos(ang_a)[:, :, None], jnp.sin(ang_a)[:, :, None]
        cb, sb = jnp.cos(ang_b)[:, None, :], jnp.sin(ang_b)[:, None, :]
        return (ca * cb - sa * sb).reshape(n, n), (sa * cb + ca * sb).reshape(n, n)
    ang = ((k[:, None] * k[None, :]) % n).astype(F32) * (2.0 * math.pi / n)
    return jnp.cos(ang), jnp.sin(ang)


def _seq_dft_matrix(n):
    c, s = _dft_cos_sin(n)
    return (jnp.concatenate([c, -s], axis=1) * (n ** -0.5)).astype(BF16)


def _chan_dft_matrix(gd):
    c, s = _dft_cos_sin(gd)
    return (jnp.concatenate([c, s], axis=1) * (gd ** -0.5)).astype(BF16)


def _rope(x, cos, sin_signed, first_half):
    fwd = pltpu.roll(x, LANES - 32, axis=1)
    bwd = pltpu.roll(x, 32, axis=1)
    return x * cos + jnp.where(first_half, fwd, bwd) * sin_signed


def _attn_in_kernel(*refs, e, hd, rope, q_scale):
    if rope:
        (x_ref, ng_ref, sc_ref, sh_ref, w_ref, cq_ref, sq_ref, ck_ref, sk_ref,
         q_ref, k_ref, v_ref, sg_ref) = refs
        lane = lax.broadcasted_iota(jnp.int32, (1, LANES), 1)
        first_half = (lane % 64) < 32
    else:
        x_ref, ng_ref, sc_ref, sh_ref, w_ref, q_ref, k_ref, v_ref, sg_ref = refs
    hb = _modulated_norm(x_ref[...], ng_ref[...], sc_ref[...], sh_ref[...])
    for j in range(e // hd):
        cols = slice(j * hd, (j + 1) * hd)
        q = jnp.dot(hb, w_ref[:, cols], preferred_element_type=F32)
        k = jnp.dot(hb, w_ref[:, e + j * hd:e + (j + 1) * hd], preferred_element_type=F32)
        if rope:
            q = _rope(q, cq_ref[...], sq_ref[...], first_half)
            k = _rope(k, ck_ref[...], sk_ref[...], first_half)
        else:
            q = q * q_scale
        q_ref[:, cols] = q.astype(q_ref.dtype)
        k_ref[:, cols] = k.astype(k_ref.dtype)
    tn = _tile(e, 512)
    for j in range(e // tn):
        cols = slice(j * tn, (j + 1) * tn)
        v = jnp.dot(hb, w_ref[:, 2 * e + j * tn:2 * e + (j + 1) * tn], preferred_element_type=F32)
        v_ref[:, cols] = v.astype(v_ref.dtype)
        g = jnp.dot(hb, w_ref[:, 3 * e + j * tn:3 * e + (j + 1) * tn], preferred_element_type=F32)
        sg_ref[:, cols] = _silu(g).astype(BF16)


def _attn_in(x, ng, sc, sh, w_in, q_scale, rope_tables, kv_dtype):
    b, n, d = x.shape
    e = w_in.shape[1] // 4
    hd = e // (2 * N_HEADS)
    assert hd == LANES
    tm = _tile(n, 512)
    rope = rope_tables is not None
    in_specs = [
        pl.BlockSpec((None, tm, d), lambda bi, i: (bi, i, 0)),
        _resident((1, d), lambda bi, i: (0, 0)),
        pl.BlockSpec((None, 1, d), lambda bi, i: (bi, 0, 0)),
        pl.BlockSpec((None, 1, d), lambda bi, i: (bi, 0, 0)),
        _resident((d, 4 * e), lambda bi, i: (0, 0)),
    ]
    args = [x, ng, sc, sh, w_in]
    if rope:
        in_specs += [pl.BlockSpec((tm, hd), lambda bi, i: (i, 0))] * 4
        args += list(rope_tables)
    tok = pl.BlockSpec((None, tm, e), lambda bi, i: (bi, i, 0))
    return pl.pallas_call(
        functools.partial(_attn_in_kernel, e=e, hd=hd, rope=rope, q_scale=q_scale),
        grid=(b, n // tm),
        in_specs=in_specs,
        out_specs=[tok, tok, tok, tok],
        out_shape=[
            jax.ShapeDtypeStruct((b, n, e), BF16),
            jax.ShapeDtypeStruct((b, n, e), kv_dtype),
            jax.ShapeDtypeStruct((b, n, e), kv_dtype),
            jax.ShapeDtypeStruct((b, n, e), BF16),
        ],
        compiler_params=_params(2),
        name="attn_in_rope" if rope else "attn_in",
    )(*args)


def _lane_blocks(x):
    return [x[:, j * LANES:(j + 1) * LANES] for j in range(x.shape[1] // LANES)]


def _tree(op, xs):
    xs = list(xs)
    while len(xs) > 1:
        xs = [op(xs[i], xs[i + 1]) if i + 1 < len(xs) else xs[i] for i in range(0, len(xs), 2)]
    return xs[0]


def _diff_attn_kernel(*refs, n_new, n_past, ck, hd, lam_init, pipelined):
    if n_past:
        (q_ref, qn_ref, k_ref, v_ref, pk_ref, pv_ref, sg_ref, lam_ref, sub_ref, out_ref,
         s_ref, p_ref, mcur_ref, l_ref, r_ref, pkb_ref, pvb_ref) = refs
    else:
        (q_ref, qn_ref, k_ref, v_ref, sg_ref, lam_ref, sub_ref, out_ref,
         s_ref, p_ref, mcur_ref, l_ref, r_ref) = refs
    n_new_chunks = n_new // ck
    n_chunks = n_new_chunks + (1 if n_past else 0)
    tq = q_ref.shape[0]
    nt = (((1,), (1,)), ((), ()))
    first = pl.program_id(2) == 0

    def rows_of(c):
        if isinstance(c, int):
            return slice(c * ck, (c + 1) * ck)
        return pl.ds(pl.multiple_of(c * ck, ck), ck)

    def new_keys(c):
        return k_ref[rows_of(c), :]

    def new_values(c):
        return v_ref[rows_of(c), :].astype(BF16)

    def past_keys():
        return pkb_ref[...]

    def past_values():
        return pvb_ref[...]

    def past_at(mp):
        return (mp, n_new_chunks, slice(None), slice(0, n_past))

    def over_chunks(body, unroll=1):
        def step(c, carry):
            body(lambda mp: (mp, c), lambda: new_keys(c), lambda: new_values(c))
            return carry
        lax.fori_loop(0, n_new_chunks, step, 0, unroll=min(unroll, n_new_chunks))
        if n_past:
            body(past_at, past_keys, past_values)

    def row_max(mp, rows):
        blocks = []
        for c in range(n_new_chunks):
            blocks += _lane_blocks(s_ref[mp, c, rows, :])
        if n_past:
            blocks += _lane_blocks(s_ref[mp, n_new_chunks, rows, 0:n_past])
        m = jnp.max(_tree(jnp.maximum, blocks), axis=-1, keepdims=True)
        mcur_ref[mp, rows, :] = jnp.broadcast_to(m, (rows.stop - rows.start, LANES))

    band = min(tq, 64)
    max_units = [(mp, slice(r, r + band)) for r in range(0, tq, band) for mp in range(2)]

    def scores_chunk(src_ref, at, load_keys):
        kc = load_keys().astype(BF16)
        for mp in range(2):
            cols = slice(mp * hd, (mp + 1) * hd)
            s_ref[at(mp)] = lax.dot_general(src_ref[:, cols], kc[:, cols], nt, preferred_element_type=F32)

    def probs_chunk(at):
        for mp in range(2):
            m = mcur_ref[mp]
            blocks = [jnp.exp2(blk - m) for blk in _lane_blocks(s_ref[at(mp)])]
            l_ref[mp] += _tree(jnp.add, blocks)
            p_ref[at(mp)] = jnp.concatenate(blocks, axis=1).astype(BF16)

    @pl.when(first)
    def _():
        if n_past:
            pkb_ref[...] = pk_ref[...].astype(BF16)
            pvb_ref[...] = pv_ref[...].astype(BF16)
        over_chunks(lambda at, load_keys, load_values: scores_chunk(q_ref, at, load_keys))
        for unit in max_units:
            row_max(*unit)

    l_ref[...] = jnp.zeros(l_ref.shape, F32)

    def fused_chunk(at, load_keys, load_values):
        probs_chunk(at)
        if pipelined:
            scores_chunk(qn_ref, at, load_keys)

    over_chunks(fused_chunk, unroll=2)

    lam_p = lam_ref[...]
    lam = (jnp.exp(jnp.sum(lam_p[0:1] * lam_p[1:2], axis=-1, keepdims=True))
           - jnp.exp(jnp.sum(lam_p[2:3] * lam_p[3:4], axis=-1, keepdims=True)) + lam_init)
    l1 = jnp.sum(l_ref[0], axis=-1, keepdims=True)
    l2 = jnp.sum(l_ref[1], axis=-1, keepdims=True)
    r_ref[...] = jnp.broadcast_to(lam * l1 / l2, (tq, LANES)).astype(BF16)

    def values_chunk(at, load_values):
        r = r_ref[...]
        a = jnp.concatenate([x - r * y for x, y in zip(_lane_blocks(p_ref[at(0)]),
                                                       _lane_blocks(p_ref[at(1)]))], axis=1)
        return jnp.dot(a, load_values(), preferred_element_type=F32)

    pending = list(max_units) if pipelined else []
    per_chunk = -(-len(pending) // n_chunks)
    o = None
    for c in range(n_chunks):
        if c < n_new_chunks:
            d = values_chunk(lambda mp, c=c: (mp, c), lambda c=c: new_values(c))
        else:
            d = values_chunk(past_at, past_values)
        o = d if o is None else o + d
        for unit in pending[:per_chunk]:
            row_max(*unit)
        pending = pending[per_chunk:]

    o = o * (1.0 / l1)
    o = o * lax.rsqrt(jnp.mean(o * o, axis=-1, keepdims=True) + EPS) * sub_ref[...]
    out_ref[...] = ((o * (1.0 - lam_init)) * sg_ref[...].astype(F32)).astype(BF16)


def _diff_attn(q, k, v, sg, lam_p, subln, lam_init, past=None):
    b, n, e = q.shape
    hw = e // N_HEADS
    hd = hw // 2
    tq = _tile(n, 512)
    ck = _tile(n, 512)
    n_tiles = n // tq
    n_past = 0 if past is None else past[0].shape[2]
    assert n_past <= ck and n_past % LANES == 0
    n_chunks = n // ck + (1 if n_past else 0)
    tok = pl.BlockSpec((None, tq, hw), lambda bi, h, i: (bi, i, h))
    tok_next = pl.BlockSpec((None, tq, hw), lambda bi, h, i: (bi, jnp.minimum(i + 1, n_tiles - 1), h))
    keys = pl.BlockSpec((None, n, hw), lambda bi, h, i: (bi, 0, h))
    in_specs = [tok, tok_next, keys, keys]
    args = [q, q, k, v]
    scratch = [
        pltpu.VMEM((2, n_chunks, tq, ck), F32),
        pltpu.VMEM((2, n_chunks, tq, ck), BF16),
        pltpu.VMEM((2, tq, LANES), F32),
        pltpu.VMEM((2, tq, LANES), F32),
        pltpu.VMEM((tq, LANES), BF16),
    ]
    if n_past:
        cache_k, cache_v, layer = past
        cached = pl.BlockSpec((None, None, n_past, hw), lambda bi, h, i: (bi, layer, 0, h))
        in_specs += [cached, cached]
        args += [cache_k, cache_v]
        scratch += [pltpu.VMEM((n_past, hw), BF16), pltpu.VMEM((n_past, hw), BF16)]
    in_specs += [tok,
                 _resident((4, hd), lambda bi, h, i: (0, 0)),
                 _resident((1, hw), lambda bi, h, i: (0, 0))]
    args += [sg, lam_p, subln]
    return pl.pallas_call(
        functools.partial(_diff_attn_kernel, n_new=n, n_past=n_past, ck=ck, hd=hd, lam_init=lam_init,
                          pipelined=n_tiles > 1),
        grid=(b, N_HEADS, n_tiles),
        in_specs=in_specs,
        out_specs=tok,
        out_shape=jax.ShapeDtypeStruct((b, n, e), BF16),
        scratch_shapes=scratch,
        compiler_params=_params(3),
        name="diff_attn_past" if n_past else "diff_attn",
    )(*args)


def _rope_tables(n, hd, q_scale):
    axis_dim = hd // 2
    rows = n // GRID_W
    row = jnp.broadcast_to(jnp.arange(rows, dtype=F32)[:, None], (rows, GRID_W)).reshape(-1)
    colp = jnp.broadcast_to(jnp.arange(GRID_W, dtype=F32)[None, :], (rows, GRID_W)).reshape(-1)
    inv = ROPE_BASE ** (-jnp.arange(0, axis_dim, 2, dtype=F32) / axis_dim)
    ar = row[:, None] * inv[None, :]
    ac = colp[:, None] * inv[None, :]
    ang = jnp.concatenate([ar, ar, ac, ac], axis=-1)
    cos, sin = jnp.cos(ang), jnp.sin(ang)
    half = axis_dim // 2
    sign = jnp.where((jnp.arange(hd) % axis_dim) < half, -1.0, 1.0).astype(F32)
    sin_signed = sin * sign[None, :]
    return cos * q_scale, sin_signed * q_scale, cos, sin_signed


def _out_proj_kernel(*refs, final):
    if final:
        a_ref, w_ref, x_ref, gt_ref, fg_ref, y_ref = refs
    else:
        a_ref, w_ref, x_ref, gt_ref, y_ref = refs
    y = x_ref[...] + gt_ref[...] * jnp.dot(a_ref[...], w_ref[...], preferred_element_type=F32)
    if final:
        y = (y * lax.rsqrt(jnp.mean(y * y, axis=-1, keepdims=True) + EPS)) * fg_ref[...]
    y_ref[...] = y


def _out_proj(a, w_out, x, gate, final_g=None):
    b, n, d = x.shape
    e = a.shape[2]
    tm = _tile(n, 512)
    final = final_g is not None
    in_specs = [
        pl.BlockSpec((None, tm, e), lambda bi, i: (bi, i, 0)),
        _resident((e, d), lambda bi, i: (0, 0)),
        pl.BlockSpec((None, tm, d), lambda bi, i: (bi, i, 0)),
        pl.BlockSpec((None, 1, d), lambda bi, i: (bi, 0, 0)),
    ]
    args = [a, w_out, x, gate]
    if final:
        in_specs.append(_resident((1, d), lambda bi, i: (0, 0)))
        args.append(final_g)
    return pl.pallas_call(
        functools.partial(_out_proj_kernel, final=final),
        grid=(b, n // tm),
        in_specs=in_specs,
        out_specs=pl.BlockSpec((None, tm, d), lambda bi, i: (bi, i, 0)),
        out_shape=jax.ShapeDtypeStruct((b, n, d), F32),
        compiler_params=_params(2),
        name="out_proj_final" if final else "out_proj",
    )(*args)


def kernel(x_prompt, x_sample, cache_k, cache_v, c, c_ctx, norm_g, w_ada, b_ada, w_in_fourier, w_out_fourier, w_in_attn, w_out_attn, lam_q1, lam_k1, lam_q2, lam_k2, subln_g, final_norm_g):
    depth, d = norm_g.shape
    bp, n_ctx, _ = x_prompt.shape
    bs, n_lat, _ = x_sample.shape
    past_len = cache_k.shape[2]
    e = w_out_attn.shape[1]
    hd = e // (2 * N_HEADS)
    gd = e // N_FOURIER_GROUPS

    n_cond = 1 + bs
    rows = -(-n_cond // 8) * 8
    cond = jnp.concatenate([c_ctx[None, :], c, jnp.zeros((rows - n_cond, d), F32)], axis=0)
    mods = _adaln(cond, w_ada, b_ada).reshape(depth, rows, 3, d)

    q_scale = hd ** -0.5 * math.log2(math.e)
    tables = _rope_tables(n_lat, hd, q_scale)
    cs = _chan_dft_matrix(gd)
    dmat_p = _seq_dft_matrix(n_ctx)
    dmat_s = _seq_dft_matrix(n_lat)
    cache_k4 = cache_k.reshape(bs, cache_k.shape[1], past_len, e)
    cache_v4 = cache_v.reshape(bs, cache_v.shape[1], past_len, e)

    xp, xs = x_prompt, x_sample
    new_k, new_v = [], []
    for i in range(depth):
        ng = norm_g[i][None, :]
        sh_p, sc_p, gt_p = (jnp.broadcast_to(mods[i, 0:1, t][:, None, :], (bp, 1, d)) for t in range(3))
        sh_s, sc_s, gt_s = (mods[i, 1:1 + bs, t][:, None, :] for t in range(3))
        j = i // N_MIXERS
        last = i == depth - 1
        fg = final_norm_g[None, :] if last else None
        if i % N_MIXERS == 0:
            w_in = w_in_fourier[j].astype(BF16)
            w_out = w_out_fourier[j].astype(BF16)
            ab_p, sg_p = _fourier_in(xp, ng, sc_p, sh_p, w_in, cs)
            ab_s, sg_s = _fourier_in(xs, ng, sc_s, sh_s, w_in, cs)
            a_p = _seq_dft(dmat_p, ab_p.reshape(bp, 2 * n_ctx, e), sg_p)
            a_s = _seq_dft(dmat_s, ab_s.reshape(bs, 2 * n_lat, e), sg_s)
        else:
            lam_init = 0.8 - 0.6 * math.exp(-0.3 * i)
            w_in = w_in_attn[j].astype(BF16)
            w_out = w_out_attn[j].astype(BF16)
            lam_p = jnp.stack([lam_q1[j], lam_k1[j], lam_q2[j], lam_k2[j]], axis=0)
            sub = subln_g[j][None, :]
            q_p, k_p, v_p, sg_p = _attn_in(xp, ng, sc_p, sh_p, w_in, q_scale, None, F32)
            new_k.append(k_p.reshape(bp, n_ctx, N_HEADS, 2 * hd))
            new_v.append(v_p.reshape(bp, n_ctx, N_HEADS, 2 * hd))
            a_p = _diff_attn(q_p, k_p, v_p, sg_p, lam_p, sub, lam_init)
            q_s, k_s, v_s, sg_s = _attn_in(xs, ng, sc_s, sh_s, w_in, q_scale, tables, BF16)
            a_s = _diff_attn(q_s, k_s, v_s, sg_s, lam_p, sub, lam_init,
                             past=(cache_k4, cache_v4, j))
        xp = _out_proj(a_p, w_out, xp, gt_p, fg)
        xs = _out_proj(a_s, w_out, xs, gt_s, fg)
    return (xp, xs, jnp.stack(new_k, axis=1), jnp.stack(new_v, axis=1))
```

```python
import functools
import math

import jax
import jax.numpy as jnp
from jax import lax
from jax.experimental import pallas as pl
from jax.experimental.pallas import tpu as pltpu

N_HEADS = 8
N_FOURIER_GROUPS = 8
GRID_W = 64
ROPE_BASE = 10000.0
EPS = 1e-6
N_MIXERS = 2

V7X_VMEM_BYTES = 64 * 1024 * 1024
VMEM_LIMIT_BYTES = V7X_VMEM_BYTES - 8 * 1024 * 1024
LANES = 128

BF16 = jnp.bfloat16
F32 = jnp.float32


def _tile(n, pref):
    t = min(n, pref)
    assert n % t == 0, (n, t)
    return t


def _params(n_axes):
    return pltpu.CompilerParams(
        dimension_semantics=("arbitrary",) * n_axes,
        vmem_limit_bytes=VMEM_LIMIT_BYTES,
    )


def _resident(block_shape, index_map):
    return pl.BlockSpec(block_shape, index_map, pipeline_mode=pl.Buffered(1))


def _adaln_kernel(cond_ref, w_ref, b_ref, out_ref):
    cond = cond_ref[...]
    act = cond * jax.nn.sigmoid(cond)
    out_ref[...] = jnp.dot(act, w_ref[...], preferred_element_type=F32,
                           precision=lax.Precision.HIGHEST) + b_ref[...]


def _adaln(cond, w_ada, b_ada):
    depth, d, d3 = w_ada.shape
    r = cond.shape[0]
    tn = _tile(d3, 1024)
    return pl.pallas_call(
        _adaln_kernel,
        grid=(depth, d3 // tn),
        in_specs=[
            pl.BlockSpec((r, d), lambda i, j: (0, 0)),
            pl.BlockSpec((None, d, tn), lambda i, j: (i, 0, j)),
            pl.BlockSpec((None, 1, tn), lambda i, j: (i, 0, j)),
        ],
        out_specs=pl.BlockSpec((None, r, tn), lambda i, j: (i, 0, j)),
        out_shape=jax.ShapeDtypeStruct((depth, r, d3), F32),
        compiler_params=_params(2),
        name="adaln",
    )(cond, w_ada, b_ada.reshape(depth, 1, d3))


def _modulated_norm(x, ng, sc, sh):
    ms = jnp.mean(x * x, axis=-1, keepdims=True)
    h = (x * lax.rsqrt(ms + EPS)) * (ng * (1.0 + sc)) + sh
    return h.astype(BF16)


def _silu(x):
    return x * jax.nn.sigmoid(x)


def _fourier_in_kernel(x_ref, ng_ref, sc_ref, sh_ref, w_ref, cs_ref, ab_ref, sg_ref, *, e, gd):
    hb = _modulated_norm(x_ref[...], ng_ref[...], sc_ref[...], sh_ref[...])
    for grp in range(e // gd):
        cols = slice(grp * gd, (grp + 1) * gd)
        u = jnp.dot(hb, w_ref[:, cols], preferred_element_type=F32)
        ab = jnp.dot(u.astype(BF16), cs_ref[...], preferred_element_type=F32)
        ab_ref[0, :, cols] = ab[:, :gd].astype(BF16)
        ab_ref[1, :, cols] = ab[:, gd:].astype(BF16)
    tn = _tile(e, 512)
    for j in range(e // tn):
        cols = slice(j * tn, (j + 1) * tn)
        g = jnp.dot(hb, w_ref[:, e + j * tn:e + (j + 1) * tn], preferred_element_type=F32)
        sg_ref[:, cols] = _silu(g).astype(BF16)


def _fourier_in(x, ng, sc, sh, w_in, cs):
    b, n, d = x.shape
    e = w_in.shape[1] // 2
    gd = e // N_FOURIER_GROUPS
    tm = _tile(n, 512)
    return pl.pallas_call(
        functools.partial(_fourier_in_kernel, e=e, gd=gd),
        grid=(b, n // tm),
        in_specs=[
            pl.BlockSpec((None, tm, d), lambda bi, i: (bi, i, 0)),
            _resident((1, d), lambda bi, i: (0, 0)),
            pl.BlockSpec((None, 1, d), lambda bi, i: (bi, 0, 0)),
            pl.BlockSpec((None, 1, d), lambda bi, i: (bi, 0, 0)),
            _resident((d, 2 * e), lambda bi, i: (0, 0)),
            _resident((gd, 2 * gd), lambda bi, i: (0, 0)),
        ],
        out_specs=[
            pl.BlockSpec((None, 2, tm, e), lambda bi, i: (bi, 0, i, 0)),
            pl.BlockSpec((None, tm, e), lambda bi, i: (bi, i, 0)),
        ],
        out_shape=[
            jax.ShapeDtypeStruct((b, 2, n, e), BF16),
            jax.ShapeDtypeStruct((b, n, e), BF16),
        ],
        compiler_params=_params(2),
        name="fourier_in",
    )(x, ng, sc, sh, w_in, cs)


def _seq_dft_kernel(d_ref, ab_ref, sg_ref, out_ref, acc_ref):
    k = pl.program_id(2)
    part = jnp.dot(d_ref[...], ab_ref[...], preferred_element_type=F32)

    @pl.when(k == 0)
    def _():
        acc_ref[...] = part

    @pl.when(k > 0)
    def _():
        acc_ref[...] += part

    @pl.when(k == pl.num_programs(2) - 1)
    def _():
        out_ref[...] = (acc_ref[...] * sg_ref[...].astype(F32)).astype(BF16)


def _seq_dft(dmat, ab, sg):
    b, n2, e = ab.shape
    n = n2 // 2
    tm = _tile(n, 1024)
    tk = _tile(n2, 1024)
    return pl.pallas_call(
        _seq_dft_kernel,
        grid=(b, n // tm, n2 // tk),
        in_specs=[
            pl.BlockSpec((tm, tk), lambda bi, i, k: (i, k)),
            pl.BlockSpec((None, tk, e), lambda bi, i, k: (bi, k, 0)),
            pl.BlockSpec((None, tm, e), lambda bi, i, k: (bi, i, 0)),
        ],
        out_specs=pl.BlockSpec((None, tm, e), lambda bi, i, k: (bi, i, 0)),
        out_shape=jax.ShapeDtypeStruct((b, n, e), BF16),
        scratch_shapes=[pltpu.VMEM((tm, e), F32)],
        compiler_params=_params(3),
        name="seq_dft",
    )(dmat, ab, sg)


def _fourier_in_r4_kernel(x_ref, ng_ref, sc_ref, sh_ref, w_ref, cs_ref, y_ref, sg_ref, u_ref, *, e, gd):
    hb = _modulated_norm(x_ref[...], ng_ref[...], sc_ref[...], sh_ref[...])
    q = x_ref.shape[0] // 4
    tn = _tile(e, 512)
    for j in range(e // tn):
        cols = slice(j * tn, (j + 1) * tn)
        u_ref[:, cols] = jnp.dot(hb, w_ref[:, cols], preferred_element_type=F32).astype(BF16)
    for grp in range(e // gd):
        cols = slice(grp * gd, (grp + 1) * gd)
        ab = jnp.dot(u_ref[:, cols], cs_ref[...], preferred_element_type=F32)
        a = [ab[j * q:(j + 1) * q, :gd] for j in range(4)]
        b = [ab[j * q:(j + 1) * q, gd:] for j in range(4)]
        s02a, d02a, s13a, d13a = a[0] + a[2], a[0] - a[2], a[1] + a[3], a[1] - a[3]
        s02b, d02b, s13b, d13b = b[0] + b[2], b[0] - b[2], b[1] + b[3], b[1] - b[3]
        ya = [s02a + s13a, d02a - d13b, s02a - s13a, d02a + d13b]
        yb = [s02b + s13b, d02b + d13a, s02b - s13b, d02b - d13a]
        for kappa in range(4):
            y_ref[kappa, 0, :, cols] = ya[kappa].astype(BF16)
            y_ref[kappa, 1, :, cols] = yb[kappa].astype(BF16)
    for j in range(e // tn):
        cols = slice(j * tn, (j + 1) * tn)
        g = jnp.dot(hb, w_ref[:, e + j * tn:e + (j + 1) * tn], preferred_element_type=F32)
        sg_ref[:, cols] = _silu(g).astype(BF16)


def _fourier_in_r4(x, ng, sc, sh, w_in, cs):
    b, n, d = x.shape
    e = w_in.shape[1] // 2
    gd = e // N_FOURIER_GROUPS
    tm = n // 4
    return pl.pallas_call(
        functools.partial(_fourier_in_r4_kernel, e=e, gd=gd),
        grid=(b, 4),
        in_specs=[
            pl.BlockSpec((None, tm, d), lambda bi, i: (bi, i, 0)),
            _resident((1, d), lambda bi, i: (0, 0)),
            pl.BlockSpec((None, 1, d), lambda bi, i: (bi, 0, 0)),
            pl.BlockSpec((None, 1, d), lambda bi, i: (bi, 0, 0)),
            _resident((d, 2 * e), lambda bi, i: (0, 0)),
            _resident((gd, 2 * gd), lambda bi, i: (0, 0)),
        ],
        out_specs=[
            pl.BlockSpec((None, 4, 2, tm // 4, e), lambda bi, i: (bi, 0, 0, i, 0)),
            pl.BlockSpec((None, tm, e), lambda bi, i: (bi, i, 0)),
        ],
        out_shape=[
            jax.ShapeDtypeStruct((b, 4, 2, n // 4, e), BF16),
            jax.ShapeDtypeStruct((b, n, e), BF16),
        ],
        scratch_shapes=[pltpu.VMEM((tm, e), BF16)],
        compiler_params=_params(2),
        name="fourier_in_r4",
    )(x, ng, sc, sh, w_in, cs)


def _seq_dft_r4_kernel(d_ref, y_ref, sg_ref, out_ref):
    tn = _tile(out_ref.shape[1], 512)
    for j in range(out_ref.shape[1] // tn):
        cols = slice(j * tn, (j + 1) * tn)
        f = jnp.dot(d_ref[...], y_ref[:, cols], preferred_element_type=F32)
        out_ref[:, cols] = (f * sg_ref[:, cols].astype(F32)).astype(BF16)


def _seq_dft_r4(dmat, y, sg):
    b, _, n2, e = y.shape
    n = 2 * n2
    return pl.pallas_call(
        _seq_dft_r4_kernel,
        grid=(4, b),
        in_specs=[
            pl.BlockSpec((None, n // 4, n2), lambda kp, bi: (kp, 0, 0)),
            pl.BlockSpec((None, None, n2, e), lambda kp, bi: (bi, kp, 0, 0)),
            pl.BlockSpec((None, n // 4, e), lambda kp, bi: (bi, kp, 0)),
        ],
        out_specs=pl.BlockSpec((None, n // 4, e), lambda kp, bi: (bi, kp, 0)),
        out_shape=jax.ShapeDtypeStruct((b, n, e), BF16),
        compiler_params=_params(2),
        name="seq_dft_r4",
    )(dmat, y, sg)


def _seq_dft_r4_matrices(n):
    kk = 4 * jnp.arange(n // 4, dtype=jnp.int32)[None, :, None] + jnp.arange(4, dtype=jnp.int32)[:, None, None]
    rho = jnp.arange(4, dtype=jnp.int32)[:, None]
    m = jnp.arange(n // 16, dtype=jnp.int32)[None, :]
    nn = (4 * m + rho).reshape(-1)[None, None, :]
    ang = ((kk * nn) % n).astype(F32) * (2.0 * math.pi / n)
    return (jnp.concatenate([jnp.cos(ang), -jnp.sin(ang)], axis=2) * (n ** -0.5)).astype(BF16)


def _to_residue_major(x, axis):
    n = x.shape[axis]
    shape = x.shape[:axis] + (n // 4, 4) + x.shape[axis + 1:]
    return jnp.swapaxes(x.reshape(shape), axis, axis + 1).reshape(x.shape)


def _from_residue_major(x, axis):
    n = x.shape[axis]
    shape = x.shape[:axis] + (4, n // 4) + x.shape[axis + 1:]
    return jnp.swapaxes(x.reshape(shape), axis, axis + 1).reshape(x.shape)


def _dft_cos_sin(n):
    k = jnp.arange(n, dtype=jnp.int32)
    if n >= 1024 and n % 64 == 0:
        m1 = jnp.arange(n // 64, dtype=jnp.int32) * 64
        m2 = jnp.arange(64, dtype=jnp.int32)
        ang_a = ((k[:, None] * m1[None, :]) % n).astype(F32) * (2.0 * math.pi / n)
        ang_b = ((k[:, None] * m2[None, :]) % n).astype(F32) * (2.0 * math.pi / n)
        ca, sa = jnp.cos(ang_a)[:, :, None], jnp.sin(ang_a)[:, :, None]
        cb, sb = jnp.cos(ang_b)[:, None, :], jnp.sin(ang_b)[:, None, :]
        return (ca * cb - sa * sb).reshape(n, n), (sa * cb + ca * sb).reshape(n, n)
    ang = ((k[:, None] * k[None, :]) % n).astype(F32) * (2.0 * math.pi / n)
    return jnp.cos(ang), jnp.sin(ang)


def _seq_dft_matrix(n):
    c, s = _dft_cos_sin(n)
    return (jnp.concatenate([c, -s], axis=1) * (n ** -0.5)).astype(BF16)


def _chan_dft_matrix(gd):
    c, s = _dft_cos_sin(gd)
    return (jnp.concatenate([c, s], axis=1) * (gd ** -0.5)).astype(BF16)


def _rope(x, cos, sin_signed, first_half):
    fwd = pltpu.roll(x, LANES - 32, axis=1)
    bwd = pltpu.roll(x, 32, axis=1)
    return x * cos + jnp.where(first_half, fwd, bwd) * sin_signed


def _attn_in_kernel(*refs, e, hd, rope, q_scale):
    if rope:
        (x_ref, ng_ref, sc_ref, sh_ref, w_ref, cq_ref, sq_ref, ck_ref, sk_ref,
         q_ref, k_ref, v_ref, sg_ref) = refs
        lane = lax.broadcasted_iota(jnp.int32, (1, LANES), 1)
        first_half = (lane % 64) < 32
    else:
        x_ref, ng_ref, sc_ref, sh_ref, w_ref, q_ref, k_ref, v_ref, sg_ref = refs
    hb = _modulated_norm(x_ref[...], ng_ref[...], sc_ref[...], sh_ref[...])
    hw = 2 * hd
    for j in range(e // hw):
        q = jnp.dot(hb, w_ref[:, j * hw:(j + 1) * hw], preferred_element_type=F32)
        k = jnp.dot(hb, w_ref[:, e + j * hw:e + (j + 1) * hw], preferred_element_type=F32)
        for mp in range(2):
            cols = slice(j * hw + mp * hd, j * hw + (mp + 1) * hd)
            qm, km = q[:, mp * hd:(mp + 1) * hd], k[:, mp * hd:(mp + 1) * hd]
            if rope:
                qm = _rope(qm, cq_ref[...], sq_ref[...], first_half)
                km = _rope(km, ck_ref[...], sk_ref[...], first_half)
            else:
                qm = qm * q_scale
            q_ref[:, cols] = qm.astype(q_ref.dtype)
            k_ref[:, cols] = km.astype(k_ref.dtype)
    tn = _tile(e, 512)
    for j in range(e // tn):
        cols = slice(j * tn, (j + 1) * tn)
        v = jnp.dot(hb, w_ref[:, 2 * e + j * tn:2 * e + (j + 1) * tn], preferred_element_type=F32)
        v_ref[:, cols] = v.astype(v_ref.dtype)
        g = jnp.dot(hb, w_ref[:, 3 * e + j * tn:3 * e + (j + 1) * tn], preferred_element_type=F32)
        sg_ref[:, cols] = _silu(g).astype(BF16)


def _attn_in(x, ng, sc, sh, w_in, q_scale, rope_tables, kv_dtype):
    b, n, d = x.shape
    e = w_in.shape[1] // 4
    hd = e // (2 * N_HEADS)
    assert hd == LANES
    tm = _tile(n, 512)
    rope = rope_tables is not None
    in_specs = [
        pl.BlockSpec((None, tm, d), lambda bi, i: (bi, i, 0)),
        _resident((1, d), lambda bi, i: (0, 0)),
        pl.BlockSpec((None, 1, d), lambda bi, i: (bi, 0, 0)),
        pl.BlockSpec((None, 1, d), lambda bi, i: (bi, 0, 0)),
        _resident((d, 4 * e), lambda bi, i: (0, 0)),
    ]
    args = [x, ng, sc, sh, w_in]
    if rope:
        in_specs += [pl.BlockSpec((tm, hd), lambda bi, i: (i, 0))] * 4
        args += list(rope_tables)
    tok = pl.BlockSpec((None, tm, e), lambda bi, i: (bi, i, 0))
    return pl.pallas_call(
        functools.partial(_attn_in_kernel, e=e, hd=hd, rope=rope, q_scale=q_scale),
        grid=(b, n // tm),
        in_specs=in_specs,
        out_specs=[tok, tok, tok, tok],
        out_shape=[
            jax.ShapeDtypeStruct((b, n, e), BF16),
            jax.ShapeDtypeStruct((b, n, e), kv_dtype),
            jax.ShapeDtypeStruct((b, n, e), kv_dtype),
            jax.ShapeDtypeStruct((b, n, e), BF16),
        ],
        compiler_params=_params(2),
        name="attn_in_rope" if rope else "attn_in",
    )(*args)


def _lane_blocks(x):
    return [x[:, j * LANES:(j + 1) * LANES] for j in range(x.shape[1] // LANES)]


def _tree(op, xs):
    xs = list(xs)
    while len(xs) > 1:
        xs = [op(xs[i], xs[i + 1]) if i + 1 < len(xs) else xs[i] for i in range(0, len(xs), 2)]
    return xs[0]


def _diff_attn_kernel(*refs, n_new, n_past, ck, hd, lam_init, pipelined):
    if n_past:
        (q_ref, qn_ref, k_ref, v_ref, pk_ref, pv_ref, sg_ref, lam_ref, sub_ref, out_ref,
         s_ref, p_ref, mcur_ref, l_ref, r_ref, pkb_ref, pvb_ref) = refs
    else:
        (q_ref, qn_ref, k_ref, v_ref, sg_ref, lam_ref, sub_ref, out_ref,
         s_ref, p_ref, mcur_ref, l_ref, r_ref) = refs
    n_new_chunks = n_new // ck
    n_chunks = n_new_chunks + (1 if n_past else 0)
    tq = q_ref.shape[0]
    nt = (((1,), (1,)), ((), ()))
    first = pl.program_id(2) == 0

    def rows_of(c):
        if isinstance(c, int):
            return slice(c * ck, (c + 1) * ck)
        return pl.ds(pl.multiple_of(c * ck, ck), ck)

    def new_keys(c):
        return k_ref[rows_of(c), :]

    def new_values(c):
        return v_ref[rows_of(c), :].astype(BF16)

    def past_keys():
        return pkb_ref[...]

    def past_values():
        return pvb_ref[...]

    def past_at(mp):
        return (mp, n_new_chunks, slice(None), slice(0, n_past))

    def over_chunks(body, unroll=1):
        def step(c, carry):
            body(lambda mp: (mp, c), lambda: new_keys(c), lambda: new_values(c))
            return carry
        lax.fori_loop(0, n_new_chunks, step, 0, unroll=min(unroll, n_new_chunks))
        if n_past:
            body(past_at, past_keys, past_values)

    def row_max(mp, rows):
        blocks = []
        for c in range(n_new_chunks):
            blocks += _lane_blocks(s_ref[mp, c, rows, :])
        if n_past:
            blocks += _lane_blocks(s_ref[mp, n_new_chunks, rows, 0:n_past])
        m = jnp.max(_tree(jnp.maximum, blocks), axis=-1, keepdims=True)
        mcur_ref[mp, rows, :] = jnp.broadcast_to(m, (rows.stop - rows.start, LANES))

    band = min(tq, 64)
    max_units = [(mp, slice(r, r + band)) for r in range(0, tq, band) for mp in range(2)]

    def scores_chunk(src_ref, at, load_keys):
        kc = load_keys().astype(BF16)
        for mp in range(2):
            cols = slice(mp * hd, (mp + 1) * hd)
            s_ref[at(mp)] = lax.dot_general(src_ref[:, cols], kc[:, cols], nt, preferred_element_type=F32)

    def probs_chunk(at):
        for mp in range(2):
            m = mcur_ref[mp]
            blocks = [jnp.exp2(blk - m) for blk in _lane_blocks(s_ref[at(mp)])]
            l_ref[mp] += _tree(jnp.add, blocks)
            p_ref[at(mp)] = jnp.concatenate(blocks, axis=1).astype(BF16)

    @pl.when(first)
    def _():
        if n_past:
            pkb_ref[...] = pk_ref[...].astype(BF16)
            pvb_ref[...] = pv_ref[...].astype(BF16)
        over_chunks(lambda at, load_keys, load_values: scores_chunk(q_ref, at, load_keys))
        for unit in max_units:
            row_max(*unit)

    l_ref[...] = jnp.zeros(l_ref.shape, F32)

    def fused_chunk(at, load_keys, load_values):
        probs_chunk(at)
        if pipelined:
            scores_chunk(qn_ref, at, load_keys)

    over_chunks(fused_chunk, unroll=2)

    lam_p = lam_ref[...]
    lam = (jnp.exp(jnp.sum(lam_p[0:1] * lam_p[1:2], axis=-1, keepdims=True))
           - jnp.exp(jnp.sum(lam_p[2:3] * lam_p[3:4], axis=-1, keepdims=True)) + lam_init)
    l1 = jnp.sum(l_ref[0], axis=-1, keepdims=True)
    l2 = jnp.sum(l_ref[1], axis=-1, keepdims=True)
    r_ref[...] = jnp.broadcast_to(lam * l1 / l2, (tq, LANES)).astype(BF16)

    def values_chunk(at, load_values):
        r = r_ref[...]
        a = jnp.concatenate([x - r * y for x, y in zip(_lane_blocks(p_ref[at(0)]),
                                                       _lane_blocks(p_ref[at(1)]))], axis=1)
        return jnp.dot(a, load_values(), preferred_element_type=F32)

    pending = list(max_units) if pipelined else []
    per_chunk = -(-len(pending) // n_chunks)
    o = None
    for c in range(n_chunks):
        if c < n_new_chunks:
            d = values_chunk(lambda mp, c=c: (mp, c), lambda c=c: new_values(c))
        else:
            d = values_chunk(past_at, past_values)
        o = d if o is None else o + d
        for unit in pending[:per_chunk]:
            row_max(*unit)
        pending = pending[per_chunk:]

    o = o * (1.0 / l1)
    o = o * lax.rsqrt(jnp.mean(o * o, axis=-1, keepdims=True) + EPS) * sub_ref[...]
    out_ref[...] = ((o * (1.0 - lam_init)) * sg_ref[...].astype(F32)).astype(BF16)


def _diff_attn(q, k, v, sg, lam_p, subln, lam_init, past=None):
    b, n, e = q.shape
    hw = e // N_HEADS
    hd = hw // 2
    tq = _tile(n, 512)
    ck = _tile(n, 512)
    n_tiles = n // tq
    n_past = 0 if past is None else past[0].shape[2]
    assert n_past <= ck and n_past % LANES == 0
    n_chunks = n // ck + (1 if n_past else 0)
    tok = pl.BlockSpec((None, tq, hw), lambda bi, h, i: (bi, i, h))
    tok_next = pl.BlockSpec((None, tq, hw), lambda bi, h, i: (bi, jnp.minimum(i + 1, n_tiles - 1), h))
    keys = pl.BlockSpec((None, n, hw), lambda bi, h, i: (bi, 0, h))
    in_specs = [tok, tok_next, keys, keys]
    args = [q, q, k, v]
    scratch = [
        pltpu.VMEM((2, n_chunks, tq, ck), F32),
        pltpu.VMEM((2, n_chunks, tq, ck), BF16),
        pltpu.VMEM((2, tq, LANES), F32),
        pltpu.VMEM((2, tq, LANES), F32),
        pltpu.VMEM((tq, LANES), BF16),
    ]
    if n_past:
        cache_k, cache_v, layer = past
        cached = pl.BlockSpec((None, None, n_past, hw), lambda bi, h, i: (bi, layer, 0, h))
        in_specs += [cached, cached]
        args += [cache_k, cache_v]
        scratch += [pltpu.VMEM((n_past, hw), BF16), pltpu.VMEM((n_past, hw), BF16)]
    in_specs += [tok,
                 _resident((4, hd), lambda bi, h, i: (0, 0)),
                 _resident((1, hw), lambda bi, h, i: (0, 0))]
    args += [sg, lam_p, subln]
    return pl.pallas_call(
        functools.partial(_diff_attn_kernel, n_new=n, n_past=n_past, ck=ck, hd=hd, lam_init=lam_init,
                          pipelined=n_tiles > 1),
        grid=(b, N_HEADS, n_tiles),
        in_specs=in_specs,
        out_specs=tok,
        out_shape=jax.ShapeDtypeStruct((b, n, e), BF16),
        scratch_shapes=scratch,
        compiler_params=_params(3),
        name="diff_attn_past" if n_past else "diff_attn",
    )(*args)


def _rope_tables(n, hd, q_scale):
    axis_dim = hd // 2
    rows = n // GRID_W
    row = jnp.broadcast_to(jnp.arange(rows, dtype=F32)[:, None], (rows, GRID_W)).reshape(-1)
    colp = jnp.broadcast_to(jnp.arange(GRID_W, dtype=F32)[None, :], (rows, GRID_W)).reshape(-1)
    inv = ROPE_BASE ** (-jnp.arange(0, axis_dim, 2, dtype=F32) / axis_dim)
    ar = row[:, None] * inv[None, :]
    ac = colp[:, None] * inv[None, :]
    ang = jnp.concatenate([ar, ar, ac, ac], axis=-1)
    cos, sin = jnp.cos(ang), jnp.sin(ang)
    half = axis_dim // 2
    sign = jnp.where((jnp.arange(hd) % axis_dim) < half, -1.0, 1.0).astype(F32)
    sin_signed = sin * sign[None, :]
    return cos * q_scale, sin_signed * q_scale, cos, sin_signed


def _out_proj_kernel(*refs, final):
    if final:
        a_ref, w_ref, x_ref, gt_ref, fg_ref, y_ref = refs
    else:
        a_ref, w_ref, x_ref, gt_ref, y_ref = refs
    y = x_ref[...] + gt_ref[...] * jnp.dot(a_ref[...], w_ref[...], preferred_element_type=F32)
    if final:
        y = (y * lax.rsqrt(jnp.mean(y * y, axis=-1, keepdims=True) + EPS)) * fg_ref[...]
    y_ref[...] = y


def _out_proj(a, w_out, x, gate, final_g=None):
    b, n, d = x.shape
    e = a.shape[2]
    tm = _tile(n, 512)
    final = final_g is not None
    in_specs = [
        pl.BlockSpec((None, tm, e), lambda bi, i: (bi, i, 0)),
        _resident((e, d), lambda bi, i: (0, 0)),
        pl.BlockSpec((None, tm, d), lambda bi, i: (bi, i, 0)),
        pl.BlockSpec((None, 1, d), lambda bi, i: (bi, 0, 0)),
    ]
    args = [a, w_out, x, gate]
    if final:
        in_specs.append(_resident((1, d), lambda bi, i: (0, 0)))
        args.append(final_g)
    return pl.pallas_call(
        functools.partial(_out_proj_kernel, final=final),
        grid=(b, n // tm),
        in_specs=in_specs,
        out_specs=pl.BlockSpec((None, tm, d), lambda bi, i: (bi, i, 0)),
        out_shape=jax.ShapeDtypeStruct((b, n, d), F32),
        compiler_params=_params(2),
        name="out_proj_final" if final else "out_proj",
    )(*args)


def kernel(x_prompt, x_sample, cache_k, cache_v, c, c_ctx, norm_g, w_ada, b_ada, w_in_fourier, w_out_fourier, w_in_attn, w_out_attn, lam_q1, lam_k1, lam_q2, lam_k2, subln_g, final_norm_g):
    depth, d = norm_g.shape
    bp, n_ctx, _ = x_prompt.shape
    bs, n_lat, _ = x_sample.shape
    past_len = cache_k.shape[2]
    e = w_out_attn.shape[1]
    hd = e // (2 * N_HEADS)
    gd = e // N_FOURIER_GROUPS

    n_cond = 1 + bs
    rows = -(-n_cond // 8) * 8
    cond = jnp.concatenate([c_ctx[None, :], c, jnp.zeros((rows - n_cond, d), F32)], axis=0)
    mods = _adaln(cond, w_ada, b_ada).reshape(depth, rows, 3, d)

    q_scale = hd ** -0.5 * math.log2(math.e)
    tables = _rope_tables(n_lat, hd, q_scale)
    cs = _chan_dft_matrix(gd)
    dmat_p = _seq_dft_matrix(n_ctx)
    cache_k4 = cache_k.reshape(bs, cache_k.shape[1], past_len, e)
    cache_v4 = cache_v.reshape(bs, cache_v.shape[1], past_len, e)

    xp, xs = x_prompt, x_sample
    radix4 = n_lat % 256 == 0
    if radix4:
        xs = _to_residue_major(xs, 1)
        tables = tuple(_to_residue_major(t, 0) for t in tables)
        dmat_s = _seq_dft_r4_matrices(n_lat)
    else:
        dmat_s = _seq_dft_matrix(n_lat)
    new_k, new_v = [], []
    for i in range(depth):
        ng = norm_g[i][None, :]
        sh_p, sc_p, gt_p = (jnp.broadcast_to(mods[i, 0:1, t][:, None, :], (bp, 1, d)) for t in range(3))
        sh_s, sc_s, gt_s = (mods[i, 1:1 + bs, t][:, None, :] for t in range(3))
        j = i // N_MIXERS
        last = i == depth - 1
        fg = final_norm_g[None, :] if last else None
        if i % N_MIXERS == 0:
            w_in = w_in_fourier[j].astype(BF16)
            w_out = w_out_fourier[j].astype(BF16)
            ab_p, sg_p = _fourier_in(xp, ng, sc_p, sh_p, w_in, cs)
            a_p = _seq_dft(dmat_p, ab_p.reshape(bp, 2 * n_ctx, e), sg_p)
            if radix4:
                y_s, sg_s = _fourier_in_r4(xs, ng, sc_s, sh_s, w_in, cs)
                a_s = _seq_dft_r4(dmat_s, y_s.reshape(bs, 4, n_lat // 2, e), sg_s)
            else:
                ab_s, sg_s = _fourier_in(xs, ng, sc_s, sh_s, w_in, cs)
                a_s = _seq_dft(dmat_s, ab_s.reshape(bs, 2 * n_lat, e), sg_s)
        else:
            lam_init = 0.8 - 0.6 * math.exp(-0.3 * i)
            w_in = w_in_attn[j].astype(BF16)
            w_out = w_out_attn[j].astype(BF16)
            lam_p = jnp.stack([lam_q1[j], lam_k1[j], lam_q2[j], lam_k2[j]], axis=0)
            sub = subln_g[j][None, :]
            q_p, k_p, v_p, sg_p = _attn_in(xp, ng, sc_p, sh_p, w_in, q_scale, None, F32)
            new_k.append(k_p.reshape(bp, n_ctx, N_HEADS, 2 * hd))
            new_v.append(v_p.reshape(bp, n_ctx, N_HEADS, 2 * hd))
            a_p = _diff_attn(q_p, k_p, v_p, sg_p, lam_p, sub, lam_init)
            q_s, k_s, v_s, sg_s = _attn_in(xs, ng, sc_s, sh_s, w_in, q_scale, tables, BF16)
            a_s = _diff_attn(q_s, k_s, v_s, sg_s, lam_p, sub, lam_init,
                             past=(cache_k4, cache_v4, j))
        xp = _out_proj(a_p, w_out, xp, gt_p, fg)
        xs = _out_proj(a_s, w_out, xs, gt_s, fg)
    if radix4:
        xs = _from_residue_major(xs, 1)
    return (xp, xs, jnp.stack(new_k, axis=1), jnp.stack(new_v, axis=1))
```

```python
import functools
import math

import jax
import jax.numpy as jnp
from jax import lax
from jax.experimental import pallas as pl
from jax.experimental.pallas import tpu as pltpu

N_HEADS = 8
N_FOURIER_GROUPS = 8
GRID_W = 64
ROPE_BASE = 10000.0
EPS = 1e-6
N_MIXERS = 2

V7X_VMEM_BYTES = 64 * 1024 * 1024
VMEM_LIMIT_BYTES = V7X_VMEM_BYTES - 8 * 1024 * 1024
LANES = 128

BF16 = jnp.bfloat16
F32 = jnp.float32


def _tile(n, pref):
    t = min(n, pref)
    assert n % t == 0, (n, t)
    return t


def _params(n_axes):
    return pltpu.CompilerParams(
        dimension_semantics=("arbitrary",) * n_axes,
        vmem_limit_bytes=VMEM_LIMIT_BYTES,
    )


def _resident(block_shape, index_map):
    return pl.BlockSpec(block_shape, index_map, pipeline_mode=pl.Buffered(1))


def _adaln_kernel(cond_ref, w_ref, b_ref, out_ref):
    cond = cond_ref[...]
    act = cond * jax.nn.sigmoid(cond)
    out_ref[...] = jnp.dot(act, w_ref[...], preferred_element_type=F32,
                           precision=lax.Precision.HIGHEST) + b_ref[...]


def _adaln(cond, w_ada, b_ada):
    depth, d, d3 = w_ada.shape
    r = cond.shape[0]
    tn = _tile(d3, 1024)
    return pl.pallas_call(
        _adaln_kernel,
        grid=(depth, d3 // tn),
        in_specs=[
            pl.BlockSpec((r, d), lambda i, j: (0, 0)),
            pl.BlockSpec((None, d, tn), lambda i, j: (i, 0, j)),
            pl.BlockSpec((None, 1, tn), lambda i, j: (i, 0, j)),
        ],
        out_specs=pl.BlockSpec((None, r, tn), lambda i, j: (i, 0, j)),
        out_shape=jax.ShapeDtypeStruct((depth, r, d3), F32),
        compiler_params=_params(2),
        name="adaln",
    )(cond, w_ada, b_ada.reshape(depth, 1, d3))


def _modulated_norm(x, ng, sc, sh):
    ms = jnp.mean(x * x, axis=-1, keepdims=True)
    h = (x * lax.rsqrt(ms + EPS)) * (ng * (1.0 + sc)) + sh
    return h.astype(BF16)


def _silu(x):
    return x * jax.nn.sigmoid(x)


def _fourier_in_kernel(x_ref, ng_ref, sc_ref, sh_ref, w_ref, cs_ref, ab_ref, sg_ref, *, e, gd):
    hb = _modulated_norm(x_ref[...], ng_ref[...], sc_ref[...], sh_ref[...])
    for grp in range(e // gd):
        cols = slice(grp * gd, (grp + 1) * gd)
        u = jnp.dot(hb, w_ref[:, cols], preferred_element_type=F32)
        ab = jnp.dot(u.astype(BF16), cs_ref[...], preferred_element_type=F32)
        ab_ref[0, :, cols] = ab[:, :gd].astype(BF16)
        ab_ref[1, :, cols] = ab[:, gd:].astype(BF16)
    tn = _tile(e, 512)
    for j in range(e // tn):
        cols = slice(j * tn, (j + 1) * tn)
        g = jnp.dot(hb, w_ref[:, e + j * tn:e + (j + 1) * tn], preferred_element_type=F32)
        sg_ref[:, cols] = _silu(g).astype(BF16)


def _fourier_in(x, ng, sc, sh, w_in, cs):
    b, n, d = x.shape
    e = w_in.shape[1] // 2
    gd = e // N_FOURIER_GROUPS
    tm = _tile(n, 512)
    return pl.pallas_call(
        functools.partial(_fourier_in_kernel, e=e, gd=gd),
        grid=(b, n // tm),
        in_specs=[
            pl.BlockSpec((None, tm, d), lambda bi, i: (bi, i, 0)),
            _resident((1, d), lambda bi, i: (0, 0)),
            pl.BlockSpec((None, 1, d), lambda bi, i: (bi, 0, 0)),
            pl.BlockSpec((None, 1, d), lambda bi, i: (bi, 0, 0)),
            _resident((d, 2 * e), lambda bi, i: (0, 0)),
            _resident((gd, 2 * gd), lambda bi, i: (0, 0)),
        ],
        out_specs=[
            pl.BlockSpec((None, 2, tm, e), lambda bi, i: (bi, 0, i, 0)),
            pl.BlockSpec((None, tm, e), lambda bi, i: (bi, i, 0)),
        ],
        out_shape=[
            jax.ShapeDtypeStruct((b, 2, n, e), BF16),
            jax.ShapeDtypeStruct((b, n, e), BF16),
        ],
        compiler_params=_params(2),
        name="fourier_in",
    )(x, ng, sc, sh, w_in, cs)


def _seq_dft_kernel(d_ref, ab_ref, sg_ref, out_ref, acc_ref):
    k = pl.program_id(2)
    part = jnp.dot(d_ref[...], ab_ref[...], preferred_element_type=F32)

    @pl.when(k == 0)
    def _():
        acc_ref[...] = part

    @pl.when(k > 0)
    def _():
        acc_ref[...] += part

    @pl.when(k == pl.num_programs(2) - 1)
    def _():
        out_ref[...] = (acc_ref[...] * sg_ref[...].astype(F32)).astype(BF16)


def _seq_dft(dmat, ab, sg):
    b, n2, e = ab.shape
    n = n2 // 2
    tm = _tile(n, 1024)
    tk = _tile(n2, 1024)
    return pl.pallas_call(
        _seq_dft_kernel,
        grid=(b, n // tm, n2 // tk),
        in_specs=[
            pl.BlockSpec((tm, tk), lambda bi, i, k: (i, k)),
            pl.BlockSpec((None, tk, e), lambda bi, i, k: (bi, k, 0)),
            pl.BlockSpec((None, tm, e), lambda bi, i, k: (bi, i, 0)),
        ],
        out_specs=pl.BlockSpec((None, tm, e), lambda bi, i, k: (bi, i, 0)),
        out_shape=jax.ShapeDtypeStruct((b, n, e), BF16),
        scratch_shapes=[pltpu.VMEM((tm, e), F32)],
        compiler_params=_params(3),
        name="seq_dft",
    )(dmat, ab, sg)


def _fourier_in_r4_kernel(x_ref, ng_ref, sc_ref, sh_ref, w_ref, cs_ref, y_ref, sg_ref, u_ref, *, e, gd):
    hb = _modulated_norm(x_ref[...], ng_ref[...], sc_ref[...], sh_ref[...])
    q = x_ref.shape[0] // 4
    tn = _tile(e, 512)
    for j in range(e // tn):
        cols = slice(j * tn, (j + 1) * tn)
        u_ref[:, cols] = jnp.dot(hb, w_ref[:, cols], preferred_element_type=F32).astype(BF16)
    for grp in range(e // gd):
        cols = slice(grp * gd, (grp + 1) * gd)
        ab = jnp.dot(u_ref[:, cols], cs_ref[...], preferred_element_type=F32)
        a = [ab[j * q:(j + 1) * q, :gd] for j in range(4)]
        b = [ab[j * q:(j + 1) * q, gd:] for j in range(4)]
        s02a, d02a, s13a, d13a = a[0] + a[2], a[0] - a[2], a[1] + a[3], a[1] - a[3]
        s02b, d02b, s13b, d13b = b[0] + b[2], b[0] - b[2], b[1] + b[3], b[1] - b[3]
        ya = [s02a + s13a, d02a - d13b, s02a - s13a, d02a + d13b]
        yb = [s02b + s13b, d02b + d13a, s02b - s13b, d02b - d13a]
        for kappa in range(4):
            y_ref[kappa, 0, :, cols] = ya[kappa].astype(BF16)
            y_ref[kappa, 1, :, cols] = yb[kappa].astype(BF16)
    for j in range(e // tn):
        cols = slice(j * tn, (j + 1) * tn)
        g = jnp.dot(hb, w_ref[:, e + j * tn:e + (j + 1) * tn], preferred_element_type=F32)
        sg_ref[:, cols] = _silu(g).astype(BF16)


def _fourier_in_r4(x, ng, sc, sh, w_in, cs):
    b, n, d = x.shape
    e = w_in.shape[1] // 2
    gd = e // N_FOURIER_GROUPS
    tm = n // 4
    return pl.pallas_call(
        functools.partial(_fourier_in_r4_kernel, e=e, gd=gd),
        grid=(b, 4),
        in_specs=[
            pl.BlockSpec((None, tm, d), lambda bi, i: (bi, i, 0)),
            _resident((1, d), lambda bi, i: (0, 0)),
            pl.BlockSpec((None, 1, d), lambda bi, i: (bi, 0, 0)),
            pl.BlockSpec((None, 1, d), lambda bi, i: (bi, 0, 0)),
            _resident((d, 2 * e), lambda bi, i: (0, 0)),
            _resident((gd, 2 * gd), lambda bi, i: (0, 0)),
        ],
        out_specs=[
            pl.BlockSpec((None, 4, 2, tm // 4, e), lambda bi, i: (bi, 0, 0, i, 0)),
            pl.BlockSpec((None, tm, e), lambda bi, i: (bi, i, 0)),
        ],
        out_shape=[
            jax.ShapeDtypeStruct((b, 4, 2, n // 4, e), BF16),
            jax.ShapeDtypeStruct((b, n, e), BF16),
        ],
        scratch_shapes=[pltpu.VMEM((tm, e), BF16)],
        compiler_params=_params(2),
        name="fourier_in_r4",
    )(x, ng, sc, sh, w_in, cs)


def _seq_dft_r4_kernel(d_ref, y_ref, sg_ref, out_ref):
    tn = _tile(out_ref.shape[1], 512)
    for j in range(out_ref.shape[1] // tn):
        cols = slice(j * tn, (j + 1) * tn)
        f = jnp.dot(d_ref[...], y_ref[:, cols], preferred_element_type=F32)
        out_ref[:, cols] = (f * sg_ref[:, cols].astype(F32)).astype(BF16)


def _seq_dft_r4(dmat, y, sg):
    b, _, n2, e = y.shape
    n = 2 * n2
    return pl.pallas_call(
        _seq_dft_r4_kernel,
        grid=(4, b),
        in_specs=[
            pl.BlockSpec((None, n // 4, n2), lambda kp, bi: (kp, 0, 0)),
            pl.BlockSpec((None, None, n2, e), lambda kp, bi: (bi, kp, 0, 0)),
            pl.BlockSpec((None, n // 4, e), lambda kp, bi: (bi, kp, 0)),
        ],
        out_specs=pl.BlockSpec((None, n // 4, e), lambda kp, bi: (bi, kp, 0)),
        out_shape=jax.ShapeDtypeStruct((b, n, e), BF16),
        compiler_params=_params(2),
        name="seq_dft_r4",
    )(dmat, y, sg)


def _seq_dft_r4_matrices(n):
    kk = 4 * jnp.arange(n // 4, dtype=jnp.int32)[None, :, None] + jnp.arange(4, dtype=jnp.int32)[:, None, None]
    rho = jnp.arange(4, dtype=jnp.int32)[:, None]
    m = jnp.arange(n // 16, dtype=jnp.int32)[None, :]
    nn = (4 * m + rho).reshape(-1)[None, None, :]
    ang = ((kk * nn) % n).astype(F32) * (2.0 * math.pi / n)
    return (jnp.concatenate([jnp.cos(ang), -jnp.sin(ang)], axis=2) * (n ** -0.5)).astype(BF16)


def _to_residue_major(x, axis):
    n = x.shape[axis]
    shape = x.shape[:axis] + (n // 4, 4) + x.shape[axis + 1:]
    return jnp.swapaxes(x.reshape(shape), axis, axis + 1).reshape(x.shape)


def _from_residue_major(x, axis):
    n = x.shape[axis]
    shape = x.shape[:axis] + (4, n // 4) + x.shape[axis + 1:]
    return jnp.swapaxes(x.reshape(shape), axis, axis + 1).reshape(x.shape)


def _dft_cos_sin(n):
    k = jnp.arange(n, dtype=jnp.int32)
    if n >= 1024 and n % 64 == 0:
        m1 = jnp.arange(n // 64, dtype=jnp.int32) * 64
        m2 = jnp.arange(64, dtype=jnp.int32)
        ang_a = ((k[:, None] * m1[None, :]) % n).astype(F32) * (2.0 * math.pi / n)
        ang_b = ((k[:, None] * m2[None, :]) % n).astype(F32) * (2.0 * math.pi / n)
        ca, sa = jnp.cos(ang_a)[:, :, None], jnp.sin(ang_a)[:, :, None]
        cb, sb = jnp.cos(ang_b)[:, None, :], jnp.sin(ang_b)[:, None, :]
        return (ca * cb - sa * sb).reshape(n, n), (sa * cb + ca * sb).reshape(n, n)
    ang = ((k[:, None] * k[None, :]) % n).astype(F32) * (2.0 * math.pi / n)
    return jnp.cos(ang), jnp.sin(ang)


def _seq_dft_matrix(n):
    c, s = _dft_cos_sin(n)
    return (jnp.concatenate([c, -s], axis=1) * (n ** -0.5)).astype(BF16)


def _chan_dft_matrix(gd):
    c, s = _dft_cos_sin(gd)
    return (jnp.concatenate([c, s], axis=1) * (gd ** -0.5)).astype(BF16)


def _rope(x, cos, sin_signed, first_half):
    fwd = pltpu.roll(x, LANES - 32, axis=1)
    bwd = pltpu.roll(x, 32, axis=1)
    return x * cos + jnp.where(first_half, fwd, bwd) * sin_signed


def _attn_in_kernel(*refs, e, hd, rope, q_scale):
    if rope:
        (x_ref, ng_ref, sc_ref, sh_ref, w_ref, cq_ref, sq_ref, ck_ref, sk_ref,
         q_ref, k_ref, v_ref, sg_ref) = refs
        lane = lax.broadcasted_iota(jnp.int32, (1, LANES), 1)
        first_half = (lane % 64) < 32
    else:
        x_ref, ng_ref, sc_ref, sh_ref, w_ref, q_ref, k_ref, v_ref, sg_ref = refs
    hb = _modulated_norm(x_ref[...], ng_ref[...], sc_ref[...], sh_ref[...])
    hw = 2 * hd
    for j in range(e // hw):
        q = jnp.dot(hb, w_ref[:, j * hw:(j + 1) * hw], preferred_element_type=F32)
        k = jnp.dot(hb, w_ref[:, e + j * hw:e + (j + 1) * hw], preferred_element_type=F32)
        for mp in range(2):
            cols = slice(j * hw + mp * hd, j * hw + (mp + 1) * hd)
            qm, km = q[:, mp * hd:(mp + 1) * hd], k[:, mp * hd:(mp + 1) * hd]
            if rope:
                qm = _rope(qm, cq_ref[...], sq_ref[...], first_half)
                km = _rope(km, ck_ref[...], sk_ref[...], first_half)
            else:
                qm = qm * q_scale
            q_ref[:, cols] = qm.astype(q_ref.dtype)
            k_ref[:, cols] = km.astype(k_ref.dtype)
    tn = _tile(e, 512)
    for j in range(e // tn):
        cols = slice(j * tn, (j + 1) * tn)
        v = jnp.dot(hb, w_ref[:, 2 * e + j * tn:2 * e + (j + 1) * tn], preferred_element_type=F32)
        v_ref[:, cols] = v.astype(v_ref.dtype)
        g = jnp.dot(hb, w_ref[:, 3 * e + j * tn:3 * e + (j + 1) * tn], preferred_element_type=F32)
        sg_ref[:, cols] = _silu(g).astype(BF16)


def _attn_in(x, ng, sc, sh, w_in, q_scale, rope_tables, kv_dtype):
    b, n, d = x.shape
    e = w_in.shape[1] // 4
    hd = e // (2 * N_HEADS)
    assert hd == LANES
    tm = _tile(n, 512)
    rope = rope_tables is not None
    in_specs = [
        pl.BlockSpec((None, tm, d), lambda bi, i: (bi, i, 0)),
        _resident((1, d), lambda bi, i: (0, 0)),
        pl.BlockSpec((None, 1, d), lambda bi, i: (bi, 0, 0)),
        pl.BlockSpec((None, 1, d), lambda bi, i: (bi, 0, 0)),
        _resident((d, 4 * e), lambda bi, i: (0, 0)),
    ]
    args = [x, ng, sc, sh, w_in]
    if rope:
        in_specs += [pl.BlockSpec((tm, hd), lambda bi, i: (i, 0))] * 4
        args += list(rope_tables)
    tok = pl.BlockSpec((None, tm, e), lambda bi, i: (bi, i, 0))
    return pl.pallas_call(
        functools.partial(_attn_in_kernel, e=e, hd=hd, rope=rope, q_scale=q_scale),
        grid=(b, n // tm),
        in_specs=in_specs,
        out_specs=[tok, tok, tok, tok],
        out_shape=[
            jax.ShapeDtypeStruct((b, n, e), BF16),
            jax.ShapeDtypeStruct((b, n, e), kv_dtype),
            jax.ShapeDtypeStruct((b, n, e), kv_dtype),
            jax.ShapeDtypeStruct((b, n, e), BF16),
        ],
        compiler_params=_params(2),
        name="attn_in_rope" if rope else "attn_in",
    )(*args)


def _lane_blocks(x):
    return [x[:, j * LANES:(j + 1) * LANES] for j in range(x.shape[1] // LANES)]


def _tree(op, xs):
    xs = list(xs)
    while len(xs) > 1:
        xs = [op(xs[i], xs[i + 1]) if i + 1 < len(xs) else xs[i] for i in range(0, len(xs), 2)]
    return xs[0]


def _diff_attn_kernel(*refs, n_new, n_past, ck, hd, lam_init, pipelined):
    if n_past:
        (q_ref, qn_ref, k_ref, v_ref, pk_ref, pv_ref, sg_ref, lam_ref, sub_ref, out_ref,
         s_ref, p_ref, mcur_ref, l_ref, r_ref, pkb_ref, pvb_ref) = refs
    else:
        (q_ref, qn_ref, k_ref, v_ref, sg_ref, lam_ref, sub_ref, out_ref,
         s_ref, p_ref, mcur_ref, l_ref, r_ref) = refs
    n_new_chunks = n_new // ck
    n_chunks = n_new_chunks + (1 if n_past else 0)
    tq = q_ref.shape[0]
    nt = (((1,), (1,)), ((), ()))
    first = pl.program_id(2) == 0

    def rows_of(c):
        if isinstance(c, int):
            return slice(c * ck, (c + 1) * ck)
        return pl.ds(pl.multiple_of(c * ck, ck), ck)

    def new_keys(c):
        return k_ref[rows_of(c), :]

    def new_values(c):
        return v_ref[rows_of(c), :].astype(BF16)

    def past_keys():
        return pkb_ref[...]

    def past_values():
        return pvb_ref[...]

    def past_at(mp):
        return (mp, n_new_chunks, slice(None), slice(0, n_past))

    def over_chunks(body, unroll=1):
        def step(c, carry):
            body(lambda mp: (mp, c), lambda: new_keys(c), lambda: new_values(c))
            return carry
        lax.fori_loop(0, n_new_chunks, step, 0, unroll=min(unroll, n_new_chunks))
        if n_past:
            body(past_at, past_keys, past_values)

    def row_max(mp, rows):
        blocks = []
        for c in range(n_new_chunks):
            blocks += _lane_blocks(s_ref[mp, c, rows, :])
        if n_past:
            blocks += _lane_blocks(s_ref[mp, n_new_chunks, rows, 0:n_past])
        m = jnp.max(_tree(jnp.maximum, blocks), axis=-1, keepdims=True)
        mcur_ref[mp, rows, :] = jnp.broadcast_to(m, (rows.stop - rows.start, LANES))

    band = min(tq, 64)
    max_units = [(mp, slice(r, r + band)) for r in range(0, tq, band) for mp in range(2)]

    def scores_chunk(src_ref, at, load_keys):
        kc = load_keys().astype(BF16)
        for mp in range(2):
            cols = slice(mp * hd, (mp + 1) * hd)
            s_ref[at(mp)] = lax.dot_general(src_ref[:, cols], kc[:, cols], nt, preferred_element_type=F32)

    def probs_chunk(at):
        for mp in range(2):
            m = mcur_ref[mp]
            blocks = [jnp.exp2(blk - m) for blk in _lane_blocks(s_ref[at(mp)])]
            l_ref[mp] += _tree(jnp.add, blocks)
            p_ref[at(mp)] = jnp.concatenate(blocks, axis=1).astype(BF16)

    @pl.when(first)
    def _():
        if n_past:
            pkb_ref[...] = pk_ref[...].astype(BF16)
            pvb_ref[...] = pv_ref[...].astype(BF16)
        over_chunks(lambda at, load_keys, load_values: scores_chunk(q_ref, at, load_keys))
        for unit in max_units:
            row_max(*unit)

    l_ref[...] = jnp.zeros(l_ref.shape, F32)

    def fused_chunk(at, load_keys, load_values):
        probs_chunk(at)
        if pipelined:
            scores_chunk(qn_ref, at, load_keys)

    over_chunks(fused_chunk, unroll=2)

    lam_p = lam_ref[...]
    lam = (jnp.exp(jnp.sum(lam_p[0:1] * lam_p[1:2], axis=-1, keepdims=True))
           - jnp.exp(jnp.sum(lam_p[2:3] * lam_p[3:4], axis=-1, keepdims=True)) + lam_init)
    l1 = jnp.sum(l_ref[0], axis=-1, keepdims=True)
    l2 = jnp.sum(l_ref[1], axis=-1, keepdims=True)
    r_ref[...] = jnp.broadcast_to(lam * l1 / l2, (tq, LANES)).astype(BF16)

    def values_chunk(at, load_values):
        r = r_ref[...]
        a = jnp.concatenate([x - r * y for x, y in zip(_lane_blocks(p_ref[at(0)]),
                                                       _lane_blocks(p_ref[at(1)]))], axis=1)
        return jnp.dot(a, load_values(), preferred_element_type=F32)

    pending = list(max_units) if pipelined else []
    per_chunk = -(-len(pending) // n_chunks)
    o = None
    for c in range(n_chunks):
        if c < n_new_chunks:
            d = values_chunk(lambda mp, c=c: (mp, c), lambda c=c: new_values(c))
        else:
            d = values_chunk(past_at, past_values)
        o = d if o is None else o + d
        for unit in pending[:per_chunk]:
            row_max(*unit)
        pending = pending[per_chunk:]

    o = o * (1.0 / l1)
    o = o * lax.rsqrt(jnp.mean(o * o, axis=-1, keepdims=True) + EPS) * sub_ref[...]
    out_ref[...] = ((o * (1.0 - lam_init)) * sg_ref[...].astype(F32)).astype(BF16)


def _diff_attn(q, k, v, sg, lam_p, subln, lam_init, past=None):
    b, n, e = q.shape
    hw = e // N_HEADS
    hd = hw // 2
    tq = _tile(n, 512)
    ck = _tile(n, 512)
    n_tiles = n // tq
    n_past = 0 if past is None else past[0].shape[2]
    assert n_past <= ck and n_past % LANES == 0
    n_chunks = n // ck + (1 if n_past else 0)
    tok = pl.BlockSpec((None, tq, hw), lambda bi, h, i: (bi, i, h))
    tok_next = pl.BlockSpec((None, tq, hw), lambda bi, h, i: (bi, jnp.minimum(i + 1, n_tiles - 1), h))
    keys = pl.BlockSpec((None, n, hw), lambda bi, h, i: (bi, 0, h))
    in_specs = [tok, tok_next, keys, keys]
    args = [q, q, k, v]
    scratch = [
        pltpu.VMEM((2, n_chunks, tq, ck), F32),
        pltpu.VMEM((2, n_chunks, tq, ck), BF16),
        pltpu.VMEM((2, tq, LANES), F32),
        pltpu.VMEM((2, tq, LANES), F32),
        pltpu.VMEM((tq, LANES), BF16),
    ]
    if n_past:
        cache_k, cache_v, layer = past
        cached = pl.BlockSpec((None, None, n_past, hw), lambda bi, h, i: (bi, layer, 0, h))
        in_specs += [cached, cached]
        args += [cache_k, cache_v]
        scratch += [pltpu.VMEM((n_past, hw), BF16), pltpu.VMEM((n_past, hw), BF16)]
    in_specs += [tok,
                 _resident((4, hd), lambda bi, h, i: (0, 0)),
                 _resident((1, hw), lambda bi, h, i: (0, 0))]
    args += [sg, lam_p, subln]
    return pl.pallas_call(
        functools.partial(_diff_attn_kernel, n_new=n, n_past=n_past, ck=ck, hd=hd, lam_init=lam_init,
                          pipelined=n_tiles > 1),
        grid=(b, N_HEADS, n_tiles),
        in_specs=in_specs,
        out_specs=tok,
        out_shape=jax.ShapeDtypeStruct((b, n, e), BF16),
        scratch_shapes=scratch,
        compiler_params=_params(3),
        name="diff_attn_past" if n_past else "diff_attn",
    )(*args)


SUBLANES = 8


def _fold_rows(op, x):
    return op(x.reshape(x.shape[0] // SUBLANES, SUBLANES, x.shape[1]), axis=0)


def _diff_attn_t_kernel(*refs, n_new, n_past, ck, hd, lam_init, pipelined):
    if n_past:
        (q_ref, qn_ref, k_ref, v_ref, pk_ref, pv_ref, sg_ref, lam_ref, sub_ref, out_ref,
         s_ref, p_ref, mcur_ref, vt_ref, pkb_ref) = refs
    else:
        (q_ref, qn_ref, k_ref, v_ref, sg_ref, lam_ref, sub_ref, out_ref,
         s_ref, p_ref, mcur_ref, vt_ref) = refs
    n_new_chunks = n_new // ck
    n_chunks = n_new_chunks + (1 if n_past else 0)
    tq = q_ref.shape[0]
    nt = (((1,), (1,)), ((), ()))
    first = pl.program_id(2) == 0

    def rows_of(c):
        if isinstance(c, int):
            return slice(c * ck, (c + 1) * ck)
        return pl.ds(pl.multiple_of(c * ck, ck), ck)

    def past_at(mp):
        return (mp, n_new_chunks, slice(0, n_past), slice(None))

    def scores_chunk(src_ref, at, kc, mrun):
        out = []
        for mp in range(2):
            cols = slice(mp * hd, (mp + 1) * hd)
            s = lax.dot_general(kc[:, cols], src_ref[:, cols], nt, preferred_element_type=F32)
            s_ref[at(mp)] = s
            out.append(jnp.maximum(mrun[mp], _fold_rows(jnp.max, s)))
        return tuple(out)

    def probs_chunk(at, lsum):
        out = []
        for mp in range(2):
            s = s_ref[at(mp)]
            s3 = s.reshape(s.shape[0] // SUBLANES, SUBLANES, tq)
            p3 = jnp.exp2(s3 - mcur_ref[mp][None])
            p_ref[at(mp)] = p3.reshape(s.shape).astype(BF16)
            out.append(lsum[mp] + jnp.sum(p3, axis=0))
        return tuple(out)

    def set_max(mrun):
        for mp in range(2):
            m = jnp.max(mrun[mp], axis=0, keepdims=True)
            mcur_ref[mp] = jnp.broadcast_to(m, (SUBLANES, tq))

    neg = jnp.full((SUBLANES, tq), -jnp.inf, F32)
    zero = jnp.zeros((SUBLANES, tq), F32)

    @pl.when(first)
    def _():
        for c in range(n_new_chunks):
            vt_ref[:, rows_of(c)] = v_ref[rows_of(c), :].astype(F32).T.astype(BF16)
        if n_past:
            pkb_ref[...] = pk_ref[...].astype(BF16)
            vt_ref[:, n_new:n_new + n_past] = pv_ref[...].T.astype(BF16)

        def step(c, mrun):
            return scores_chunk(q_ref, lambda mp: (mp, c), k_ref[rows_of(c), :].astype(BF16), mrun)
        mrun = lax.fori_loop(0, n_new_chunks, step, (neg, neg))
        if n_past:
            mrun = scores_chunk(q_ref, past_at, pkb_ref[...], mrun)
        set_max(mrun)

    def fused(at, load_keys, carry):
        lsum = probs_chunk(at, carry[:2])
        mrun = scores_chunk(qn_ref, at, load_keys(), carry[2:]) if pipelined else carry[2:]
        return lsum + mrun

    def fused_step(c, carry):
        return fused(lambda mp: (mp, c), lambda: k_ref[rows_of(c), :].astype(BF16), carry)

    carry = lax.fori_loop(0, n_new_chunks, fused_step, (zero, zero, neg, neg),
                          unroll=min(2, n_new_chunks))
    if n_past:
        carry = fused(past_at, lambda: pkb_ref[...], carry)
    if pipelined:
        set_max(carry[2:])

    lam_p = lam_ref[...]
    lam = (jnp.exp(jnp.sum(lam_p[0:1] * lam_p[1:2], axis=-1, keepdims=True))
           - jnp.exp(jnp.sum(lam_p[2:3] * lam_p[3:4], axis=-1, keepdims=True)) + lam_init)
    l1 = jnp.sum(carry[0], axis=0, keepdims=True)
    l2 = jnp.sum(carry[1], axis=0, keepdims=True)
    r = (lam * l1 / l2).astype(BF16)

    o = None
    for c in range(n_chunks):
        if c < n_new_chunks:
            at, vt = (lambda mp, c=c: (mp, c)), vt_ref[:, rows_of(c)]
        else:
            at, vt = past_at, vt_ref[:, n_new:n_new + n_past]
        a = p_ref[at(0)] - r * p_ref[at(1)]
        d = jnp.dot(vt, a, preferred_element_type=F32)
        o = d if o is None else o + d

    o = o * (1.0 / l1)
    o = o * lax.rsqrt(jnp.mean(o * o, axis=0, keepdims=True) + EPS)
    o = jnp.concatenate([blk * sub_ref[...] for blk in _lane_blocks(o)], axis=1)
    out_ref[...] = ((o.T * (1.0 - lam_init)) * sg_ref[...].astype(F32)).astype(BF16)


def _diff_attn_t(q, k, v, sg, lam_p, subln, lam_init, past=None):
    b, n, e = q.shape
    hw = e // N_HEADS
    hd = hw // 2
    tq = _tile(n, 512)
    ck = _tile(n, 512)
    n_tiles = n // tq
    n_past = 0 if past is None else past[0].shape[2]
    assert n_past <= ck and n_past % LANES == 0
    n_chunks = n // ck + (1 if n_past else 0)
    tok = pl.BlockSpec((None, tq, hw), lambda bi, h, i: (bi, i, h))
    tok_next = pl.BlockSpec((None, tq, hw), lambda bi, h, i: (bi, jnp.minimum(i + 1, n_tiles - 1), h))
    keys = pl.BlockSpec((None, n, hw), lambda bi, h, i: (bi, 0, h))
    in_specs = [tok, tok_next, keys, keys]
    args = [q, q, k, v]
    scratch = [
        pltpu.VMEM((2, n_chunks, ck, tq), F32),
        pltpu.VMEM((2, n_chunks, ck, tq), BF16),
        pltpu.VMEM((2, SUBLANES, tq), F32),
        pltpu.VMEM((hw, n + n_past), BF16),
    ]
    if n_past:
        cache_k, cache_v, layer = past
        cached = pl.BlockSpec((None, None, n_past, hw), lambda bi, h, i: (bi, layer, 0, h))
        in_specs += [cached, cached]
        args += [cache_k, cache_v]
        scratch += [pltpu.VMEM((n_past, hw), BF16)]
    in_specs += [tok,
                 _resident((4, hd), lambda bi, h, i: (0, 0)),
                 _resident((hw, LANES), lambda bi, h, i: (0, 0))]
    args += [sg, lam_p, jnp.broadcast_to(subln.reshape(hw, 1), (hw, LANES))]
    return pl.pallas_call(
        functools.partial(_diff_attn_t_kernel, n_new=n, n_past=n_past, ck=ck, hd=hd, lam_init=lam_init,
                          pipelined=n_tiles > 1),
        grid=(b, N_HEADS, n_tiles),
        in_specs=in_specs,
        out_specs=tok,
        out_shape=jax.ShapeDtypeStruct((b, n, e), BF16),
        scratch_shapes=scratch,
        compiler_params=_params(3),
        name="diff_attn_past" if n_past else "diff_attn",
    )(*args)


def _rope_tables(n, hd, q_scale):
    axis_dim = hd // 2
    rows = n // GRID_W
    row = jnp.broadcast_to(jnp.arange(rows, dtype=F32)[:, None], (rows, GRID_W)).reshape(-1)
    colp = jnp.broadcast_to(jnp.arange(GRID_W, dtype=F32)[None, :], (rows, GRID_W)).reshape(-1)
    inv = ROPE_BASE ** (-jnp.arange(0, axis_dim, 2, dtype=F32) / axis_dim)
    ar = row[:, None] * inv[None, :]
    ac = colp[:, None] * inv[None, :]
    ang = jnp.concatenate([ar, ar, ac, ac], axis=-1)
    cos, sin = jnp.cos(ang), jnp.sin(ang)
    half = axis_dim // 2
    sign = jnp.where((jnp.arange(hd) % axis_dim) < half, -1.0, 1.0).astype(F32)
    sin_signed = sin * sign[None, :]
    return cos * q_scale, sin_signed * q_scale, cos, sin_signed


def _out_proj_kernel(*refs, final):
    if final:
        a_ref, w_ref, x_ref, gt_ref, fg_ref, y_ref = refs
    else:
        a_ref, w_ref, x_ref, gt_ref, y_ref = refs
    y = x_ref[...] + gt_ref[...] * jnp.dot(a_ref[...], w_ref[...], preferred_element_type=F32)
    if final:
        y = (y * lax.rsqrt(jnp.mean(y * y, axis=-1, keepdims=True) + EPS)) * fg_ref[...]
    y_ref[...] = y


def _out_proj(a, w_out, x, gate, final_g=None):
    b, n, d = x.shape
    e = a.shape[2]
    tm = _tile(n, 512)
    final = final_g is not None
    in_specs = [
        pl.BlockSpec((None, tm, e), lambda bi, i: (bi, i, 0)),
        _resident((e, d), lambda bi, i: (0, 0)),
        pl.BlockSpec((None, tm, d), lambda bi, i: (bi, i, 0)),
        pl.BlockSpec((None, 1, d), lambda bi, i: (bi, 0, 0)),
    ]
    args = [a, w_out, x, gate]
    if final:
        in_specs.append(_resident((1, d), lambda bi, i: (0, 0)))
        args.append(final_g)
    return pl.pallas_call(
        functools.partial(_out_proj_kernel, final=final),
        grid=(b, n // tm),
        in_specs=in_specs,
        out_specs=pl.BlockSpec((None, tm, d), lambda bi, i: (bi, i, 0)),
        out_shape=jax.ShapeDtypeStruct((b, n, d), F32),
        compiler_params=_params(2),
        name="out_proj_final" if final else "out_proj",
    )(*args)


def kernel(x_prompt, x_sample, cache_k, cache_v, c, c_ctx, norm_g, w_ada, b_ada, w_in_fourier, w_out_fourier, w_in_attn, w_out_attn, lam_q1, lam_k1, lam_q2, lam_k2, subln_g, final_norm_g):
    depth, d = norm_g.shape
    bp, n_ctx, _ = x_prompt.shape
    bs, n_lat, _ = x_sample.shape
    past_len = cache_k.shape[2]
    e = w_out_attn.shape[1]
    hd = e // (2 * N_HEADS)
    gd = e // N_FOURIER_GROUPS

    n_cond = 1 + bs
    rows = -(-n_cond // 8) * 8
    cond = jnp.concatenate([c_ctx[None, :], c, jnp.zeros((rows - n_cond, d), F32)], axis=0)
    mods = _adaln(cond, w_ada, b_ada).reshape(depth, rows, 3, d)

    q_scale = hd ** -0.5 * math.log2(math.e)
    tables = _rope_tables(n_lat, hd, q_scale)
    cs = _chan_dft_matrix(gd)
    dmat_p = _seq_dft_matrix(n_ctx)
    cache_k4 = cache_k.reshape(bs, cache_k.shape[1], past_len, e)
    cache_v4 = cache_v.reshape(bs, cache_v.shape[1], past_len, e)

    xp, xs = x_prompt, x_sample
    radix4 = n_lat % 256 == 0
    if radix4:
        xs = _to_residue_major(xs, 1)
        tables = tuple(_to_residue_major(t, 0) for t in tables)
        dmat_s = _seq_dft_r4_matrices(n_lat)
    else:
        dmat_s = _seq_dft_matrix(n_lat)
    new_k, new_v = [], []
    for i in range(depth):
        ng = norm_g[i][None, :]
        sh_p, sc_p, gt_p = (jnp.broadcast_to(mods[i, 0:1, t][:, None, :], (bp, 1, d)) for t in range(3))
        sh_s, sc_s, gt_s = (mods[i, 1:1 + bs, t][:, None, :] for t in range(3))
        j = i // N_MIXERS
        last = i == depth - 1
        fg = final_norm_g[None, :] if last else None
        if i % N_MIXERS == 0:
            w_in = w_in_fourier[j].astype(BF16)
            w_out = w_out_fourier[j].astype(BF16)
            ab_p, sg_p = _fourier_in(xp, ng, sc_p, sh_p, w_in, cs)
            a_p = _seq_dft(dmat_p, ab_p.reshape(bp, 2 * n_ctx, e), sg_p)
            if radix4:
                y_s, sg_s = _fourier_in_r4(xs, ng, sc_s, sh_s, w_in, cs)
                a_s = _seq_dft_r4(dmat_s, y_s.reshape(bs, 4, n_lat // 2, e), sg_s)
            else:
                ab_s, sg_s = _fourier_in(xs, ng, sc_s, sh_s, w_in, cs)
                a_s = _seq_dft(dmat_s, ab_s.reshape(bs, 2 * n_lat, e), sg_s)
        else:
            lam_init = 0.8 - 0.6 * math.exp(-0.3 * i)
            w_in = w_in_attn[j].astype(BF16)
            w_out = w_out_attn[j].astype(BF16)
            lam_p = jnp.stack([lam_q1[j], lam_k1[j], lam_q2[j], lam_k2[j]], axis=0)
            sub = subln_g[j][None, :]
            q_p, k_p, v_p, sg_p = _attn_in(xp, ng, sc_p, sh_p, w_in, q_scale, None, F32)
            new_k.append(k_p.reshape(bp, n_ctx, N_HEADS, 2 * hd))
            new_v.append(v_p.reshape(bp, n_ctx, N_HEADS, 2 * hd))
            a_p = _diff_attn_t(q_p, k_p, v_p, sg_p, lam_p, sub, lam_init)
            q_s, k_s, v_s, sg_s = _attn_in(xs, ng, sc_s, sh_s, w_in, q_scale, tables, BF16)
            a_s = _diff_attn_t(q_s, k_s, v_s, sg_s, lam_p, sub, lam_init,
                             past=(cache_k4, cache_v4, j))
        xp = _out_proj(a_p, w_out, xp, gt_p, fg)
        xs = _out_proj(a_s, w_out, xs, gt_s, fg)
    if radix4:
        xs = _from_residue_major(xs, 1)
    return (xp, xs, jnp.stack(new_k, axis=1), jnp.stack(new_v, axis=1))
```

```python
import functools
import math

import jax
import jax.numpy as jnp
from jax import lax
from jax.experimental import pallas as pl
from jax.experimental.pallas import tpu as pltpu

N_HEADS = 8
N_FOURIER_GROUPS = 8
GRID_W = 64
ROPE_BASE = 10000.0
EPS = 1e-6
N_MIXERS = 2

V7X_VMEM_BYTES = 64 * 1024 * 1024
VMEM_LIMIT_BYTES = V7X_VMEM_BYTES - 8 * 1024 * 1024
LANES = 128

BF16 = jnp.bfloat16
F32 = jnp.float32


def _tile(n, pref):
    t = min(n, pref)
    assert n % t == 0, (n, t)
    return t


def _params(n_axes):
    return pltpu.CompilerParams(
        dimension_semantics=("arbitrary",) * n_axes,
        vmem_limit_bytes=VMEM_LIMIT_BYTES,
    )


def _resident(block_shape, index_map):
    return pl.BlockSpec(block_shape, index_map, pipeline_mode=pl.Buffered(1))


def _adaln_kernel(cond_ref, w_ref, b_ref, out_ref):
    cond = cond_ref[...]
    act = cond * jax.nn.sigmoid(cond)
    out_ref[...] = jnp.dot(act, w_ref[...], preferred_element_type=F32,
                           precision=lax.Precision.HIGHEST) + b_ref[...]


def _adaln(cond, w_ada, b_ada):
    depth, d, d3 = w_ada.shape
    r = cond.shape[0]
    tn = _tile(d3, 1024)
    return pl.pallas_call(
        _adaln_kernel,
        grid=(depth, d3 // tn),
        in_specs=[
            pl.BlockSpec((r, d), lambda i, j: (0, 0)),
            pl.BlockSpec((None, d, tn), lambda i, j: (i, 0, j)),
            pl.BlockSpec((None, 1, tn), lambda i, j: (i, 0, j)),
        ],
        out_specs=pl.BlockSpec((None, r, tn), lambda i, j: (i, 0, j)),
        out_shape=jax.ShapeDtypeStruct((depth, r, d3), F32),
        compiler_params=_params(2),
        name="adaln",
    )(cond, w_ada, b_ada.reshape(depth, 1, d3))


def _modulated_norm(x, ng, sc, sh):
    ms = jnp.mean(x * x, axis=-1, keepdims=True)
    h = (x * lax.rsqrt(ms + EPS)) * (ng * (1.0 + sc)) + sh
    return h.astype(BF16)


def _silu(x):
    return x * jax.nn.sigmoid(x)


def _fourier_in_kernel(x_ref, ng_ref, sc_ref, sh_ref, w_ref, cs_ref, ab_ref, sg_ref, *, e, gd):
    hb = _modulated_norm(x_ref[...], ng_ref[...], sc_ref[...], sh_ref[...])
    for grp in range(e // gd):
        cols = slice(grp * gd, (grp + 1) * gd)
        u = jnp.dot(hb, w_ref[:, cols], preferred_element_type=F32)
        ab = jnp.dot(u.astype(BF16), cs_ref[...], preferred_element_type=F32)
        ab_ref[0, :, cols] = ab[:, :gd].astype(BF16)
        ab_ref[1, :, cols] = ab[:, gd:].astype(BF16)
    tn = _tile(e, 512)
    for j in range(e // tn):
        cols = slice(j * tn, (j + 1) * tn)
        g = jnp.dot(hb, w_ref[:, e + j * tn:e + (j + 1) * tn], preferred_element_type=F32)
        sg_ref[:, cols] = _silu(g).astype(BF16)


def _fourier_in(x, ng, sc, sh, w_in, cs):
    b, n, d = x.shape
    e = w_in.shape[1] // 2
    gd = e // N_FOURIER_GROUPS
    tm = _tile(n, 512)
    return pl.pallas_call(
        functools.partial(_fourier_in_kernel, e=e, gd=gd),
        grid=(b, n // tm),
        in_specs=[
            pl.BlockSpec((None, tm, d), lambda bi, i: (bi, i, 0)),
            _resident((1, d), lambda bi, i: (0, 0)),
            pl.BlockSpec((None, 1, d), lambda bi, i: (bi, 0, 0)),
            pl.BlockSpec((None, 1, d), lambda bi, i: (bi, 0, 0)),
            _resident((d, 2 * e), lambda bi, i: (0, 0)),
            _resident((gd, 2 * gd), lambda bi, i: (0, 0)),
        ],
        out_specs=[
            pl.BlockSpec((None, 2, tm, e), lambda bi, i: (bi, 0, i, 0)),
            pl.BlockSpec((None, tm, e), lambda bi, i: (bi, i, 0)),
        ],
        out_shape=[
            jax.ShapeDtypeStruct((b, 2, n, e), BF16),
            jax.ShapeDtypeStruct((b, n, e), BF16),
        ],
        compiler_params=_params(2),
        name="fourier_in",
    )(x, ng, sc, sh, w_in, cs)


def _seq_dft_kernel(d_ref, ab_ref, sg_ref, out_ref, acc_ref):
    k = pl.program_id(2)
    part = jnp.dot(d_ref[...], ab_ref[...], preferred_element_type=F32)

    @pl.when(k == 0)
    def _():
        acc_ref[...] = part

    @pl.when(k > 0)
    def _():
        acc_ref[...] += part

    @pl.when(k == pl.num_programs(2) - 1)
    def _():
        out_ref[...] = (acc_ref[...] * sg_ref[...].astype(F32)).astype(BF16)


def _seq_dft(dmat, ab, sg):
    b, n2, e = ab.shape
    n = n2 // 2
    tm = _tile(n, 1024)
    tk = _tile(n2, 1024)
    return pl.pallas_call(
        _seq_dft_kernel,
        grid=(b, n // tm, n2 // tk),
        in_specs=[
            pl.BlockSpec((tm, tk), lambda bi, i, k: (i, k)),
            pl.BlockSpec((None, tk, e), lambda bi, i, k: (bi, k, 0)),
            pl.BlockSpec((None, tm, e), lambda bi, i, k: (bi, i, 0)),
        ],
        out_specs=pl.BlockSpec((None, tm, e), lambda bi, i, k: (bi, i, 0)),
        out_shape=jax.ShapeDtypeStruct((b, n, e), BF16),
        scratch_shapes=[pltpu.VMEM((tm, e), F32)],
        compiler_params=_params(3),
        name="seq_dft",
    )(dmat, ab, sg)


def _fourier_in_r4_kernel(x_ref, ng_ref, sc_ref, sh_ref, w_ref, cs_ref, y_ref, sg_ref, u_ref, *, e, gd):
    hb = _modulated_norm(x_ref[...], ng_ref[...], sc_ref[...], sh_ref[...])
    q = x_ref.shape[0] // 4
    tn = _tile(e, 512)
    for j in range(e // tn):
        cols = slice(j * tn, (j + 1) * tn)
        u_ref[:, cols] = jnp.dot(hb, w_ref[:, cols], preferred_element_type=F32).astype(BF16)
    for grp in range(e // gd):
        cols = slice(grp * gd, (grp + 1) * gd)
        ab = jnp.dot(u_ref[:, cols], cs_ref[...], preferred_element_type=F32)
        a = [ab[j * q:(j + 1) * q, :gd] for j in range(4)]
        b = [ab[j * q:(j + 1) * q, gd:] for j in range(4)]
        s02a, d02a, s13a, d13a = a[0] + a[2], a[0] - a[2], a[1] + a[3], a[1] - a[3]
        s02b, d02b, s13b, d13b = b[0] + b[2], b[0] - b[2], b[1] + b[3], b[1] - b[3]
        ya = [s02a + s13a, d02a - d13b, s02a - s13a, d02a + d13b]
        yb = [s02b + s13b, d02b + d13a, s02b - s13b, d02b - d13a]
        for kappa in range(4):
            y_ref[kappa, 0, :, cols] = ya[kappa].astype(BF16)
            y_ref[kappa, 1, :, cols] = yb[kappa].astype(BF16)
    for j in range(e // tn):
        cols = slice(j * tn, (j + 1) * tn)
        g = jnp.dot(hb, w_ref[:, e + j * tn:e + (j + 1) * tn], preferred_element_type=F32)
        sg_ref[:, cols] = _silu(g).astype(BF16)


def _fourier_in_r4(x, ng, sc, sh, w_in, cs):
    b, n, d = x.shape
    e = w_in.shape[1] // 2
    gd = e // N_FOURIER_GROUPS
    tm = n // 4
    return pl.pallas_call(
        functools.partial(_fourier_in_r4_kernel, e=e, gd=gd),
        grid=(b, 4),
        in_specs=[
            pl.BlockSpec((None, tm, d), lambda bi, i: (bi, i, 0)),
            _resident((1, d), lambda bi, i: (0, 0)),
            pl.BlockSpec((None, 1, d), lambda bi, i: (bi, 0, 0)),
            pl.BlockSpec((None, 1, d), lambda bi, i: (bi, 0, 0)),
            _resident((d, 2 * e), lambda bi, i: (0, 0)),
            _resident((gd, 2 * gd), lambda bi, i: (0, 0)),
        ],
        out_specs=[
            pl.BlockSpec((None, 4, 2, tm // 4, e), lambda bi, i: (bi, 0, 0, i, 0)),
            pl.BlockSpec((None, tm, e), lambda bi, i: (bi, i, 0)),
        ],
        out_shape=[
            jax.ShapeDtypeStruct((b, 4, 2, n // 4, e), BF16),
            jax.ShapeDtypeStruct((b, n, e), BF16),
        ],
        scratch_shapes=[pltpu.VMEM((tm, e), BF16)],
        compiler_params=_params(2),
        name="fourier_in_r4",
    )(x, ng, sc, sh, w_in, cs)


def _seq_dft_r4_kernel(d_ref, y_ref, sg_ref, out_ref):
    tn = _tile(out_ref.shape[1], 512)
    for j in range(out_ref.shape[1] // tn):
        cols = slice(j * tn, (j + 1) * tn)
        f = jnp.dot(d_ref[...], y_ref[:, cols], preferred_element_type=F32)
        out_ref[:, cols] = (f * sg_ref[:, cols].astype(F32)).astype(BF16)


def _seq_dft_r4(dmat, y, sg):
    b, _, n2, e = y.shape
    n = 2 * n2
    return pl.pallas_call(
        _seq_dft_r4_kernel,
        grid=(4, b),
        in_specs=[
            pl.BlockSpec((None, n // 4, n2), lambda kp, bi: (kp, 0, 0)),
            pl.BlockSpec((None, None, n2, e), lambda kp, bi: (bi, kp, 0, 0)),
            pl.BlockSpec((None, n // 4, e), lambda kp, bi: (bi, kp, 0)),
        ],
        out_specs=pl.BlockSpec((None, n // 4, e), lambda kp, bi: (bi, kp, 0)),
        out_shape=jax.ShapeDtypeStruct((b, n, e), BF16),
        compiler_params=_params(2),
        name="seq_dft_r4",
    )(dmat, y, sg)


def _seq_dft_r4_matrices(n):
    kk = 4 * jnp.arange(n // 4, dtype=jnp.int32)[None, :, None] + jnp.arange(4, dtype=jnp.int32)[:, None, None]
    rho = jnp.arange(4, dtype=jnp.int32)[None, None, :]
    m4 = 4 * jnp.arange(n // 16, dtype=jnp.int32)[None, None, :]
    ang_a = ((kk * m4) % n).astype(F32) * (2.0 * math.pi / n)
    ang_b = ((kk * rho) % n).astype(F32) * (2.0 * math.pi / n)
    ca, sa = jnp.cos(ang_a)[:, :, None, :], jnp.sin(ang_a)[:, :, None, :]
    cb, sb = jnp.cos(ang_b)[:, :, :, None], jnp.sin(ang_b)[:, :, :, None]
    c = (ca * cb - sa * sb).reshape(4, n // 4, n // 4)
    s = (sa * cb + ca * sb).reshape(4, n // 4, n // 4)
    return (jnp.concatenate([c, -s], axis=2) * (n ** -0.5)).astype(BF16)


def _to_residue_major(x, axis):
    n = x.shape[axis]
    shape = x.shape[:axis] + (n // 4, 4) + x.shape[axis + 1:]
    return jnp.swapaxes(x.reshape(shape), axis, axis + 1).reshape(x.shape)


def _from_residue_major(x, axis):
    n = x.shape[axis]
    shape = x.shape[:axis] + (4, n // 4) + x.shape[axis + 1:]
    return jnp.swapaxes(x.reshape(shape), axis, axis + 1).reshape(x.shape)


def _dft_cos_sin(n):
    k = jnp.arange(n, dtype=jnp.int32)
    if n >= 1024 and n % 64 == 0:
        m1 = jnp.arange(n // 64, dtype=jnp.int32) * 64
        m2 = jnp.arange(64, dtype=jnp.int32)
        ang_a = ((k[:, None] * m1[None, :]) % n).astype(F32) * (2.0 * math.pi / n)
        ang_b = ((k[:, None] * m2[None, :]) % n).astype(F32) * (2.0 * math.pi / n)
        ca, sa = jnp.cos(ang_a)[:, :, None], jnp.sin(ang_a)[:, :, None]
        cb, sb = jnp.cos(ang_b)[:, None, :], jnp.sin(ang_b)[:, None, :]
        return (ca * cb - sa * sb).reshape(n, n), (sa * cb + ca * sb).reshape(n, n)
    ang = ((k[:, None] * k[None, :]) % n).astype(F32) * (2.0 * math.pi / n)
    return jnp.cos(ang), jnp.sin(ang)


def _seq_dft_matrix(n):
    c, s = _dft_cos_sin(n)
    return (jnp.concatenate([c, -s], axis=1) * (n ** -0.5)).astype(BF16)


def _chan_dft_matrix(gd):
    c, s = _dft_cos_sin(gd)
    return (jnp.concatenate([c, s], axis=1) * (gd ** -0.5)).astype(BF16)


def _rope(x, cos, sin_signed, first_half):
    fwd = pltpu.roll(x, LANES - 32, axis=1)
    bwd = pltpu.roll(x, 32, axis=1)
    return x * cos + jnp.where(first_half, fwd, bwd) * sin_signed


def _attn_in_kernel(*refs, e, hd, rope, q_scale):
    if rope:
        (x_ref, ng_ref, sc_ref, sh_ref, w_ref, cq_ref, sq_ref, ck_ref, sk_ref,
         q_ref, k_ref, v_ref, sg_ref) = refs
        lane = lax.broadcasted_iota(jnp.int32, (1, LANES), 1)
        first_half = (lane % 64) < 32
    else:
        x_ref, ng_ref, sc_ref, sh_ref, w_ref, q_ref, k_ref, v_ref, sg_ref = refs
    hb = _modulated_norm(x_ref[...], ng_ref[...], sc_ref[...], sh_ref[...])
    hw = 2 * hd
    for j in range(e // hw):
        q = jnp.dot(hb, w_ref[:, j * hw:(j + 1) * hw], preferred_element_type=F32)
        k = jnp.dot(hb, w_ref[:, e + j * hw:e + (j + 1) * hw], preferred_element_type=F32)
        for mp in range(2):
            cols = slice(j * hw + mp * hd, j * hw + (mp + 1) * hd)
            qm, km = q[:, mp * hd:(mp + 1) * hd], k[:, mp * hd:(mp + 1) * hd]
            if rope:
                qm = _rope(qm, cq_ref[...], sq_ref[...], first_half)
                km = _rope(km, ck_ref[...], sk_ref[...], first_half)
            else:
                qm = qm * q_scale
            q_ref[:, cols] = qm.astype(q_ref.dtype)
            k_ref[:, cols] = km.astype(k_ref.dtype)
    tn = _tile(e, 512)
    for j in range(e // tn):
        cols = slice(j * tn, (j + 1) * tn)
        v = jnp.dot(hb, w_ref[:, 2 * e + j * tn:2 * e + (j + 1) * tn], preferred_element_type=F32)
        v_ref[:, cols] = v.astype(v_ref.dtype)
        g = jnp.dot(hb, w_ref[:, 3 * e + j * tn:3 * e + (j + 1) * tn], preferred_element_type=F32)
        sg_ref[:, cols] = _silu(g).astype(BF16)


def _attn_in(x, ng, sc, sh, w_in, q_scale, rope_tables, kv_dtype):
    b, n, d = x.shape
    e = w_in.shape[1] // 4
    hd = e // (2 * N_HEADS)
    assert hd == LANES
    tm = _tile(n, 512)
    rope = rope_tables is not None
    in_specs = [
        pl.BlockSpec((None, tm, d), lambda bi, i: (bi, i, 0)),
        _resident((1, d), lambda bi, i: (0, 0)),
        pl.BlockSpec((None, 1, d), lambda bi, i: (bi, 0, 0)),
        pl.BlockSpec((None, 1, d), lambda bi, i: (bi, 0, 0)),
        _resident((d, 4 * e), lambda bi, i: (0, 0)),
    ]
    args = [x, ng, sc, sh, w_in]
    if rope:
        in_specs += [pl.BlockSpec((tm, hd), lambda bi, i: (i, 0))] * 4
        args += list(rope_tables)
    tok = pl.BlockSpec((None, tm, e), lambda bi, i: (bi, i, 0))
    return pl.pallas_call(
        functools.partial(_attn_in_kernel, e=e, hd=hd, rope=rope, q_scale=q_scale),
        grid=(b, n // tm),
        in_specs=in_specs,
        out_specs=[tok, tok, tok, tok],
        out_shape=[
            jax.ShapeDtypeStruct((b, n, e), BF16),
            jax.ShapeDtypeStruct((b, n, e), kv_dtype),
            jax.ShapeDtypeStruct((b, n, e), kv_dtype),
            jax.ShapeDtypeStruct((b, n, e), BF16),
        ],
        compiler_params=_params(2),
        name="attn_in_rope" if rope else "attn_in",
    )(*args)


def _lane_blocks(x):
    return [x[:, j * LANES:(j + 1) * LANES] for j in range(x.shape[1] // LANES)]


def _tree(op, xs):
    xs = list(xs)
    while len(xs) > 1:
        xs = [op(xs[i], xs[i + 1]) if i + 1 < len(xs) else xs[i] for i in range(0, len(xs), 2)]
    return xs[0]


def _diff_attn_kernel(*refs, n_new, n_past, ck, hd, lam_init, pipelined):
    if n_past:
        (q_ref, qn_ref, k_ref, v_ref, pk_ref, pv_ref, sg_ref, lam_ref, sub_ref, out_ref,
         s_ref, p_ref, mcur_ref, l_ref, r_ref, pkb_ref, pvb_ref) = refs
    else:
        (q_ref, qn_ref, k_ref, v_ref, sg_ref, lam_ref, sub_ref, out_ref,
         s_ref, p_ref, mcur_ref, l_ref, r_ref) = refs
    n_new_chunks = n_new // ck
    n_chunks = n_new_chunks + (1 if n_past else 0)
    tq = q_ref.shape[0]
    nt = (((1,), (1,)), ((), ()))
    first = pl.program_id(2) == 0

    def rows_of(c):
        if isinstance(c, int):
            return slice(c * ck, (c + 1) * ck)
        return pl.ds(pl.multiple_of(c * ck, ck), ck)

    def new_keys(c):
        return k_ref[rows_of(c), :]

    def new_values(c):
        return v_ref[rows_of(c), :].astype(BF16)

    def past_keys():
        return pkb_ref[...]

    def past_values():
        return pvb_ref[...]

    def past_at(mp):
        return (mp, n_new_chunks, slice(None), slice(0, n_past))

    def over_chunks(body, unroll=1):
        def step(c, carry):
            body(lambda mp: (mp, c), lambda: new_keys(c), lambda: new_values(c))
            return carry
        lax.fori_loop(0, n_new_chunks, step, 0, unroll=min(unroll, n_new_chunks))
        if n_past:
            body(past_at, past_keys, past_values)

    def row_max(mp, rows):
        blocks = []
        for c in range(n_new_chunks):
            blocks += _lane_blocks(s_ref[mp, c, rows, :])
        if n_past:
            blocks += _lane_blocks(s_ref[mp, n_new_chunks, rows, 0:n_past])
        m = jnp.max(_tree(jnp.maximum, blocks), axis=-1, keepdims=True)
        mcur_ref[mp, rows, :] = jnp.broadcast_to(m, (rows.stop - rows.start, LANES))

    band = min(tq, 64)
    max_units = [(mp, slice(r, r + band)) for r in range(0, tq, band) for mp in range(2)]

    def scores_chunk(src_ref, at, load_keys):
        kc = load_keys().astype(BF16)
        for mp in range(2):
            cols = slice(mp * hd, (mp + 1) * hd)
            s_ref[at(mp)] = lax.dot_general(src_ref[:, cols], kc[:, cols], nt, preferred_element_type=F32)

    def probs_chunk(at):
        for mp in range(2):
            m = mcur_ref[mp]
            blocks = [jnp.exp2(blk - m) for blk in _lane_blocks(s_ref[at(mp)])]
            l_ref[mp] += _tree(jnp.add, blocks)
            p_ref[at(mp)] = jnp.concatenate(blocks, axis=1).astype(BF16)

    @pl.when(first)
    def _():
        if n_past:
            pkb_ref[...] = pk_ref[...].astype(BF16)
            pvb_ref[...] = pv_ref[...].astype(BF16)
        over_chunks(lambda at, load_keys, load_values: scores_chunk(q_ref, at, load_keys))
        for unit in max_units:
            row_max(*unit)

    l_ref[...] = jnp.zeros(l_ref.shape, F32)

    def fused_chunk(at, load_keys, load_values):
        probs_chunk(at)
        if pipelined:
            scores_chunk(qn_ref, at, load_keys)

    over_chunks(fused_chunk, unroll=2)

    lam_p = lam_ref[...]
    lam = (jnp.exp(jnp.sum(lam_p[0:1] * lam_p[1:2], axis=-1, keepdims=True))
           - jnp.exp(jnp.sum(lam_p[2:3] * lam_p[3:4], axis=-1, keepdims=True)) + lam_init)
    l1 = jnp.sum(l_ref[0], axis=-1, keepdims=True)
    l2 = jnp.sum(l_ref[1], axis=-1, keepdims=True)
    r_ref[...] = jnp.broadcast_to(lam * l1 / l2, (tq, LANES)).astype(BF16)

    def values_chunk(at, load_values):
        r = r_ref[...]
        a = jnp.concatenate([x - r * y for x, y in zip(_lane_blocks(p_ref[at(0)]),
                                                       _lane_blocks(p_ref[at(1)]))], axis=1)
        return jnp.dot(a, load_values(), preferred_element_type=F32)

    pending = list(max_units) if pipelined else []
    per_chunk = -(-len(pending) // n_chunks)
    o = None
    for c in range(n_chunks):
        if c < n_new_chunks:
            d = values_chunk(lambda mp, c=c: (mp, c), lambda c=c: new_values(c))
        else:
            d = values_chunk(past_at, past_values)
        o = d if o is None else o + d
        for unit in pending[:per_chunk]:
            row_max(*unit)
        pending = pending[per_chunk:]

    o = o * (1.0 / l1)
    o = o * lax.rsqrt(jnp.mean(o * o, axis=-1, keepdims=True) + EPS) * sub_ref[...]
    out_ref[...] = ((o * (1.0 - lam_init)) * sg_ref[...].astype(F32)).astype(BF16)


def _diff_attn(q, k, v, sg, lam_p, subln, lam_init, past=None):
    b, n, e = q.shape
    hw = e // N_HEADS
    hd = hw // 2
    tq = _tile(n, 512)
    ck = _tile(n, 512)
    n_tiles = n // tq
    n_past = 0 if past is None else past[0].shape[2]
    assert n_past <= ck and n_past % LANES == 0
    n_chunks = n // ck + (1 if n_past else 0)
    tok = pl.BlockSpec((None, tq, hw), lambda bi, h, i: (bi, i, h))
    tok_next = pl.BlockSpec((None, tq, hw), lambda bi, h, i: (bi, jnp.minimum(i + 1, n_tiles - 1), h))
    keys = pl.BlockSpec((None, n, hw), lambda bi, h, i: (bi, 0, h))
    in_specs = [tok, tok_next, keys, keys]
    args = [q, q, k, v]
    scratch = [
        pltpu.VMEM((2, n_chunks, tq, ck), F32),
        pltpu.VMEM((2, n_chunks, tq, ck), BF16),
        pltpu.VMEM((2, tq, LANES), F32),
        pltpu.VMEM((2, tq, LANES), F32),
        pltpu.VMEM((tq, LANES), BF16),
    ]
    if n_past:
        cache_k, cache_v, layer = past
        cached = pl.BlockSpec((None, None, n_past, hw), lambda bi, h, i: (bi, layer, 0, h))
        in_specs += [cached, cached]
        args += [cache_k, cache_v]
        scratch += [pltpu.VMEM((n_past, hw), BF16), pltpu.VMEM((n_past, hw), BF16)]
    in_specs += [tok,
                 _resident((4, hd), lambda bi, h, i: (0, 0)),
                 _resident((1, hw), lambda bi, h, i: (0, 0))]
    args += [sg, lam_p, subln]
    return pl.pallas_call(
        functools.partial(_diff_attn_kernel, n_new=n, n_past=n_past, ck=ck, hd=hd, lam_init=lam_init,
                          pipelined=n_tiles > 1),
        grid=(b, N_HEADS, n_tiles),
        in_specs=in_specs,
        out_specs=tok,
        out_shape=jax.ShapeDtypeStruct((b, n, e), BF16),
        scratch_shapes=scratch,
        compiler_params=_params(3),
        name="diff_attn_past" if n_past else "diff_attn",
    )(*args)


SUBLANES = 8


def _fold_rows(op, x):
    return op(x.reshape(x.shape[0] // SUBLANES, SUBLANES, x.shape[1]), axis=0)


def _diff_attn_t_kernel(*refs, n_new, n_past, ck, hd, lam_init, pipelined):
    if n_past:
        (q_ref, qn_ref, k_ref, v_ref, pk_ref, pv_ref, sg_ref, lam_ref, sub_ref, out_ref,
         s_ref, p_ref, mcur_ref, vt_ref, pkb_ref) = refs
    else:
        (q_ref, qn_ref, k_ref, v_ref, sg_ref, lam_ref, sub_ref, out_ref,
         s_ref, p_ref, mcur_ref, vt_ref) = refs
    n_new_chunks = n_new // ck
    n_chunks = n_new_chunks + (1 if n_past else 0)
    tq = q_ref.shape[0]
    nt = (((1,), (1,)), ((), ()))
    first = pl.program_id(2) == 0

    def rows_of(c):
        if isinstance(c, int):
            return slice(c * ck, (c + 1) * ck)
        return pl.ds(pl.multiple_of(c * ck, ck), ck)

    def past_at(mp):
        return (mp, n_new_chunks, slice(0, n_past), slice(None))

    def scores_chunk(src_ref, at, kc, mrun):
        out = []
        for mp in range(2):
            cols = slice(mp * hd, (mp + 1) * hd)
            s = lax.dot_general(kc[:, cols], src_ref[:, cols], nt, preferred_element_type=F32)
            s_ref[at(mp)] = s
            out.append(jnp.maximum(mrun[mp], _fold_rows(jnp.max, s)))
        return tuple(out)

    def probs_chunk(at, lsum):
        out = []
        for mp in range(2):
            s = s_ref[at(mp)]
            s3 = s.reshape(s.shape[0] // SUBLANES, SUBLANES, tq)
            p3 = jnp.exp2(s3 - mcur_ref[mp][None])
            p_ref[at(mp)] = p3.reshape(s.shape).astype(BF16)
            out.append(lsum[mp] + jnp.sum(p3, axis=0))
        return tuple(out)

    def set_max(mrun):
        for mp in range(2):
            m = jnp.max(mrun[mp], axis=0, keepdims=True)
            mcur_ref[mp] = jnp.broadcast_to(m, (SUBLANES, tq))

    neg = jnp.full((SUBLANES, tq), -jnp.inf, F32)
    zero = jnp.zeros((SUBLANES, tq), F32)

    @pl.when(first)
    def _():
        for c in range(n_new_chunks):
            vt_ref[:, rows_of(c)] = v_ref[rows_of(c), :].astype(F32).T.astype(BF16)
        if n_past:
            pkb_ref[...] = pk_ref[...].astype(BF16)
            vt_ref[:, n_new:n_new + n_past] = pv_ref[...].T.astype(BF16)

        def step(c, mrun):
            return scores_chunk(q_ref, lambda mp: (mp, c), k_ref[rows_of(c), :].astype(BF16), mrun)
        mrun = lax.fori_loop(0, n_new_chunks, step, (neg, neg))
        if n_past:
            mrun = scores_chunk(q_ref, past_at, pkb_ref[...], mrun)
        set_max(mrun)

    def fused(at, load_keys, carry):
        lsum = probs_chunk(at, carry[:2])
        mrun = scores_chunk(qn_ref, at, load_keys(), carry[2:]) if pipelined else carry[2:]
        return lsum + mrun

    def fused_step(c, carry):
        return fused(lambda mp: (mp, c), lambda: k_ref[rows_of(c), :].astype(BF16), carry)

    carry = lax.fori_loop(0, n_new_chunks, fused_step, (zero, zero, neg, neg),
                          unroll=min(2, n_new_chunks))
    if n_past:
        carry = fused(past_at, lambda: pkb_ref[...], carry)
    if pipelined:
        set_max(carry[2:])

    lam_p = lam_ref[...]
    lam = (jnp.exp(jnp.sum(lam_p[0:1] * lam_p[1:2], axis=-1, keepdims=True))
           - jnp.exp(jnp.sum(lam_p[2:3] * lam_p[3:4], axis=-1, keepdims=True)) + lam_init)
    l1 = jnp.sum(carry[0], axis=0, keepdims=True)
    l2 = jnp.sum(carry[1], axis=0, keepdims=True)
    r = (lam * l1 / l2).astype(BF16)

    o = None
    for c in range(n_chunks):
        if c < n_new_chunks:
            at, vt = (lambda mp, c=c: (mp, c)), vt_ref[:, rows_of(c)]
        else:
            at, vt = past_at, vt_ref[:, n_new:n_new + n_past]
        a = p_ref[at(0)] - r * p_ref[at(1)]
        d = jnp.dot(vt, a, preferred_element_type=F32)
        o = d if o is None else o + d

    o = o * (1.0 / l1)
    o = o * lax.rsqrt(jnp.mean(o * o, axis=0, keepdims=True) + EPS)
    o = jnp.concatenate([blk * sub_ref[...] for blk in _lane_blocks(o)], axis=1)
    out_ref[...] = ((o.T * (1.0 - lam_init)) * sg_ref[...].astype(F32)).astype(BF16)


def _diff_attn_t(q, k, v, sg, lam_p, subln, lam_init, past=None):
    b, n, e = q.shape
    hw = e // N_HEADS
    hd = hw // 2
    tq = _tile(n, 512)
    ck = _tile(n, 512)
    n_tiles = n // tq
    n_past = 0 if past is None else past[0].shape[2]
    assert n_past <= ck and n_past % LANES == 0
    n_chunks = n // ck + (1 if n_past else 0)
    tok = pl.BlockSpec((None, tq, hw), lambda bi, h, i: (bi, i, h))
    tok_next = pl.BlockSpec((None, tq, hw), lambda bi, h, i: (bi, jnp.minimum(i + 1, n_tiles - 1), h))
    keys = pl.BlockSpec((None, n, hw), lambda bi, h, i: (bi, 0, h))
    in_specs = [tok, tok_next, keys, keys]
    args = [q, q, k, v]
    scratch = [
        pltpu.VMEM((2, n_chunks, ck, tq), F32),
        pltpu.VMEM((2, n_chunks, ck, tq), BF16),
        pltpu.VMEM((2, SUBLANES, tq), F32),
        pltpu.VMEM((hw, n + n_past), BF16),
    ]
    if n_past:
        cache_k, cache_v, layer = past
        cached = pl.BlockSpec((None, None, n_past, hw), lambda bi, h, i: (bi, layer, 0, h))
        in_specs += [cached, cached]
        args += [cache_k, cache_v]
        scratch += [pltpu.VMEM((n_past, hw), BF16)]
    in_specs += [tok,
                 _resident((4, hd), lambda bi, h, i: (0, 0)),
                 _resident((hw, LANES), lambda bi, h, i: (0, 0))]
    args += [sg, lam_p, jnp.broadcast_to(subln.reshape(hw, 1), (hw, LANES))]
    return pl.pallas_call(
        functools.partial(_diff_attn_t_kernel, n_new=n, n_past=n_past, ck=ck, hd=hd, lam_init=lam_init,
                          pipelined=n_tiles > 1),
        grid=(b, N_HEADS, n_tiles),
        in_specs=in_specs,
        out_specs=tok,
        out_shape=jax.ShapeDtypeStruct((b, n, e), BF16),
        scratch_shapes=scratch,
        compiler_params=_params(3),
        name="diff_attn_past" if n_past else "diff_attn",
    )(*args)


def _diff_attn3_kernel(*refs, n_new, n_past, ck, hd, lam_init, n_tiles):
    if n_past:
        (q_ref, qn_ref, k_ref, v_ref, pk_ref, pv_ref, sg_ref, lam_ref, sub_ref, out_ref,
         s_ref, p_ref, mcur_ref, vt_ref, r_ref, il_ref, o_ref) = refs
    else:
        (q_ref, qn_ref, k_ref, v_ref, sg_ref, lam_ref, sub_ref, out_ref,
         s_ref, p_ref, mcur_ref, vt_ref, r_ref, il_ref, o_ref) = refs
    n_new_chunks = n_new // ck
    tq = q_ref.shape[0]
    nt = (((1,), (1,)), ((), ()))
    i = pl.program_id(2)
    first = i == 0
    last = i == n_tiles - 1
    pipelined = n_tiles > 1

    def rows_of(c):
        if isinstance(c, int):
            return slice(c * ck, (c + 1) * ck)
        return pl.ds(pl.multiple_of(c * ck, ck), ck)

    def new_chunk(c):
        return (lambda mp: (mp, c)), (lambda: k_ref[rows_of(c), :].astype(BF16)), (lambda: vt_ref[c])

    past_chunk = ((lambda mp: (mp, n_new_chunks, slice(0, n_past), slice(None))),
                  (lambda: pk_ref[...].astype(BF16)),
                  (lambda: vt_ref[n_new_chunks, :, 0:n_past]))

    def scores_chunk(src_ref, chunk, mrun):
        at, load_keys, _ = chunk
        kc = load_keys()
        out = []
        for mp in range(2):
            cols = slice(mp * hd, (mp + 1) * hd)
            s = lax.dot_general(kc[:, cols], src_ref[:, cols], nt, preferred_element_type=F32)
            s_ref[at(mp)] = s
            out.append(jnp.maximum(mrun[mp], _fold_rows(jnp.max, s)))
        return tuple(out)

    def probs_chunk(chunk, lsum):
        at = chunk[0]
        out = []
        for mp in range(2):
            s = s_ref[at(mp)]
            s3 = s.reshape(s.shape[0] // SUBLANES, SUBLANES, tq)
            p3 = jnp.exp2(s3 - mcur_ref[mp][None])
            p_ref[at(mp)] = p3.reshape(s.shape).astype(BF16)
            out.append(lsum[mp] + jnp.sum(p3, axis=0))
        return tuple(out)

    def values_chunk(chunk):
        at, _, load_vt = chunk
        a = p_ref[at(0)] - r_ref[0:1, :].astype(BF16) * p_ref[at(1)]
        return jnp.dot(load_vt(), a, preferred_element_type=F32)

    def finish_stats(carry):
        l1 = jnp.sum(carry[0], axis=0, keepdims=True)
        l2 = jnp.sum(carry[1], axis=0, keepdims=True)
        lam_p = lam_ref[...]
        lam = (jnp.exp(jnp.sum(lam_p[0:1] * lam_p[1:2], axis=-1, keepdims=True))
               - jnp.exp(jnp.sum(lam_p[2:3] * lam_p[3:4], axis=-1, keepdims=True)) + lam_init)
        r_ref[...] = jnp.broadcast_to(lam * l1 / l2, (SUBLANES, tq))
        il_ref[...] = jnp.broadcast_to(1.0 / l1, (SUBLANES, tq))
        if pipelined:
            for mp in range(2):
                m = jnp.max(carry[2 + mp], axis=0, keepdims=True)
                mcur_ref[mp] = jnp.broadcast_to(m, (SUBLANES, tq))

    def write_tile(o, tile):
        rows = pl.ds(pl.multiple_of(tile * tq, tq), tq)
        o = o * il_ref[0:1, :]
        o = o * lax.rsqrt(jnp.mean(o * o, axis=0, keepdims=True) + EPS)
        o = jnp.concatenate([blk * sub_ref[...] for blk in _lane_blocks(o)], axis=1)
        out_ref[rows, :] = ((o.T * (1.0 - lam_init)) * sg_ref[rows, :].astype(F32)).astype(BF16)

    neg = jnp.full((SUBLANES, tq), -jnp.inf, F32)
    zero = jnp.zeros((SUBLANES, tq), F32)

    def stage_loop(with_values):
        def body(chunk, carry):
            if with_values:
                o_ref[...] += values_chunk(chunk)
            lsum = probs_chunk(chunk, carry[:2])
            mrun = scores_chunk(qn_ref, chunk, carry[2:]) if pipelined else carry[2:]
            return lsum + mrun
        carry = lax.fori_loop(0, n_new_chunks, lambda c, carry: body(new_chunk(c), carry),
                              (zero, zero, neg, neg), unroll=min(2, n_new_chunks))
        if n_past:
            carry = body(past_chunk, carry)
        return carry

    @pl.when(first)
    def _():
        for c in range(n_new_chunks):
            vt_ref[c] = v_ref[rows_of(c), :].astype(F32).T.astype(BF16)
        if n_past:
            vt_ref[n_new_chunks, :, 0:n_past] = pv_ref[...].astype(F32).T.astype(BF16)
        mrun = lax.fori_loop(0, n_new_chunks, lambda c, m: scores_chunk(q_ref, new_chunk(c), m), (neg, neg))
        if n_past:
            mrun = scores_chunk(q_ref, past_chunk, mrun)
        for mp in range(2):
            mcur_ref[mp] = jnp.broadcast_to(jnp.max(mrun[mp], axis=0, keepdims=True), (SUBLANES, tq))
        finish_stats(stage_loop(with_values=False))

    if pipelined:
        @pl.when(jnp.logical_not(first))
        def _():
            o_ref[...] = jnp.zeros(o_ref.shape, F32)
            carry = stage_loop(with_values=True)
            write_tile(o_ref[...], i - 1)
            finish_stats(carry)

    @pl.when(last)
    def _():
        o = None
        for c in range(n_new_chunks):
            d = values_chunk(new_chunk(c))
            o = d if o is None else o + d
        if n_past:
            o = o + values_chunk(past_chunk)
        write_tile(o, i)


def _diff_attn3(q, k, v, sg, lam_p, subln, lam_init, past=None):
    b, n, e = q.shape
    hw = e // N_HEADS
    hd = hw // 2
    tq = _tile(n, 512)
    ck = _tile(n, 512)
    n_tiles = n // tq
    n_past = 0 if past is None else past[0].shape[2]
    assert n_past <= ck and n_past % LANES == 0
    n_chunks = n // ck + (1 if n_past else 0)
    tok = pl.BlockSpec((None, tq, hw), lambda bi, h, i: (bi, i, h))
    tok_next = pl.BlockSpec((None, tq, hw), lambda bi, h, i: (bi, jnp.minimum(i + 1, n_tiles - 1), h))
    head = pl.BlockSpec((None, n, hw), lambda bi, h, i: (bi, 0, h))
    in_specs = [tok, tok_next, head, head]
    args = [q, q, k, v]
    scratch = [
        pltpu.VMEM((2, n_chunks, ck, tq), F32),
        pltpu.VMEM((2, n_chunks, ck, tq), BF16),
        pltpu.VMEM((2, SUBLANES, tq), F32),
        pltpu.VMEM((n_chunks, hw, ck), BF16),
        pltpu.VMEM((SUBLANES, tq), F32),
        pltpu.VMEM((SUBLANES, tq), F32),
        pltpu.VMEM((hw, tq), F32),
    ]
    if n_past:
        cache_k, cache_v, layer = past
        cached = pl.BlockSpec((None, None, n_past, hw), lambda bi, h, i: (bi, layer, 0, h))
        in_specs += [cached, cached]
        args += [cache_k, cache_v]
    in_specs += [head,
                 _resident((4, hd), lambda bi, h, i: (0, 0)),
                 _resident((hw, LANES), lambda bi, h, i: (0, 0))]
    args += [sg, lam_p, jnp.broadcast_to(subln.reshape(hw, 1), (hw, LANES))]
    return pl.pallas_call(
        functools.partial(_diff_attn3_kernel, n_new=n, n_past=n_past, ck=ck, hd=hd, lam_init=lam_init,
                          n_tiles=n_tiles),
        grid=(b, N_HEADS, n_tiles),
        in_specs=in_specs,
        out_specs=head,
        out_shape=jax.ShapeDtypeStruct((b, n, e), BF16),
        scratch_shapes=scratch,
        compiler_params=_params(3),
        name="diff_attn_past" if n_past else "diff_attn",
    )(*args)


def _rope_tables(n, hd, q_scale):
    axis_dim = hd // 2
    rows = n // GRID_W
    row = jnp.broadcast_to(jnp.arange(rows, dtype=F32)[:, None], (rows, GRID_W)).reshape(-1)
    colp = jnp.broadcast_to(jnp.arange(GRID_W, dtype=F32)[None, :], (rows, GRID_W)).reshape(-1)
    inv = ROPE_BASE ** (-jnp.arange(0, axis_dim, 2, dtype=F32) / axis_dim)
    ar = row[:, None] * inv[None, :]
    ac = colp[:, None] * inv[None, :]
    ang = jnp.concatenate([ar, ar, ac, ac], axis=-1)
    cos, sin = jnp.cos(ang), jnp.sin(ang)
    half = axis_dim // 2
    sign = jnp.where((jnp.arange(hd) % axis_dim) < half, -1.0, 1.0).astype(F32)
    sin_signed = sin * sign[None, :]
    return cos * q_scale, sin_signed * q_scale, cos, sin_signed


def _out_proj_kernel(*refs, final):
    if final:
        a_ref, w_ref, x_ref, gt_ref, fg_ref, y_ref = refs
    else:
        a_ref, w_ref, x_ref, gt_ref, y_ref = refs
    y = x_ref[...] + gt_ref[...] * jnp.dot(a_ref[...], w_ref[...], preferred_element_type=F32)
    if final:
        y = (y * lax.rsqrt(jnp.mean(y * y, axis=-1, keepdims=True) + EPS)) * fg_ref[...]
    y_ref[...] = y


def _out_proj(a, w_out, x, gate, final_g=None):
    b, n, d = x.shape
    e = a.shape[2]
    tm = _tile(n, 512)
    final = final_g is not None
    in_specs = [
        pl.BlockSpec((None, tm, e), lambda bi, i: (bi, i, 0)),
        _resident((e, d), lambda bi, i: (0, 0)),
        pl.BlockSpec((None, tm, d), lambda bi, i: (bi, i, 0)),
        pl.BlockSpec((None, 1, d), lambda bi, i: (bi, 0, 0)),
    ]
    args = [a, w_out, x, gate]
    if final:
        in_specs.append(_resident((1, d), lambda bi, i: (0, 0)))
        args.append(final_g)
    return pl.pallas_call(
        functools.partial(_out_proj_kernel, final=final),
        grid=(b, n // tm),
        in_specs=in_specs,
        out_specs=pl.BlockSpec((None, tm, d), lambda bi, i: (bi, i, 0)),
        out_shape=jax.ShapeDtypeStruct((b, n, d), F32),
        compiler_params=_params(2),
        name="out_proj_final" if final else "out_proj",
    )(*args)


def kernel(x_prompt, x_sample, cache_k, cache_v, c, c_ctx, norm_g, w_ada, b_ada, w_in_fourier, w_out_fourier, w_in_attn, w_out_attn, lam_q1, lam_k1, lam_q2, lam_k2, subln_g, final_norm_g):
    depth, d = norm_g.shape
    bp, n_ctx, _ = x_prompt.shape
    bs, n_lat, _ = x_sample.shape
    past_len = cache_k.shape[2]
    e = w_out_attn.shape[1]
    hd = e // (2 * N_HEADS)
    gd = e // N_FOURIER_GROUPS

    n_cond = 1 + bs
    rows = -(-n_cond // 8) * 8
    cond = jnp.concatenate([c_ctx[None, :], c, jnp.zeros((rows - n_cond, d), F32)], axis=0)
    mods = _adaln(cond, w_ada, b_ada).reshape(depth, rows, 3, d)

    q_scale = hd ** -0.5 * math.log2(math.e)
    tables = _rope_tables(n_lat, hd, q_scale)
    cs = _chan_dft_matrix(gd)
    dmat_p = _seq_dft_matrix(n_ctx)
    cache_k4 = cache_k.astype(BF16).reshape(bs, cache_k.shape[1], past_len, e)
    cache_v4 = cache_v.astype(BF16).reshape(bs, cache_v.shape[1], past_len, e)

    xp, xs = x_prompt, x_sample
    radix4 = n_lat % 256 == 0
    if radix4:
        xs = _to_residue_major(xs, 1)
        tables = tuple(_to_residue_major(t, 0) for t in tables)
        dmat_s = _seq_dft_r4_matrices(n_lat)
    else:
        dmat_s = _seq_dft_matrix(n_lat)
    new_k, new_v = [], []
    for i in range(depth):
        ng = norm_g[i][None, :]
        sh_p, sc_p, gt_p = (jnp.broadcast_to(mods[i, 0:1, t][:, None, :], (bp, 1, d)) for t in range(3))
        sh_s, sc_s, gt_s = (mods[i, 1:1 + bs, t][:, None, :] for t in range(3))
        j = i // N_MIXERS
        last = i == depth - 1
        fg = final_norm_g[None, :] if last else None
        if i % N_MIXERS == 0:
            w_in = w_in_fourier[j].astype(BF16)
            w_out = w_out_fourier[j].astype(BF16)
            ab_p, sg_p = _fourier_in(xp, ng, sc_p, sh_p, w_in, cs)
            a_p = _seq_dft(dmat_p, ab_p.reshape(bp, 2 * n_ctx, e), sg_p)
            if radix4:
                y_s, sg_s = _fourier_in_r4(xs, ng, sc_s, sh_s, w_in, cs)
                a_s = _seq_dft_r4(dmat_s, y_s.reshape(bs, 4, n_lat // 2, e), sg_s)
            else:
                ab_s, sg_s = _fourier_in(xs, ng, sc_s, sh_s, w_in, cs)
                a_s = _seq_dft(dmat_s, ab_s.reshape(bs, 2 * n_lat, e), sg_s)
        else:
            lam_init = 0.8 - 0.6 * math.exp(-0.3 * i)
            w_in = w_in_attn[j].astype(BF16)
            w_out = w_out_attn[j].astype(BF16)
            lam_p = jnp.stack([lam_q1[j], lam_k1[j], lam_q2[j], lam_k2[j]], axis=0)
            sub = subln_g[j][None, :]
            q_p, k_p, v_p, sg_p = _attn_in(xp, ng, sc_p, sh_p, w_in, q_scale, None, F32)
            new_k.append(k_p.reshape(bp, n_ctx, N_HEADS, 2 * hd))
            new_v.append(v_p.reshape(bp, n_ctx, N_HEADS, 2 * hd))
            a_p = _diff_attn3(q_p, k_p, v_p, sg_p, lam_p, sub, lam_init)
            q_s, k_s, v_s, sg_s = _attn_in(xs, ng, sc_s, sh_s, w_in, q_scale, tables, BF16)
            a_s = _diff_attn3(q_s, k_s, v_s, sg_s, lam_p, sub, lam_init,
                             past=(cache_k4, cache_v4, j))
        xp = _out_proj(a_p, w_out, xp, gt_p, fg)
        xs = _out_proj(a_s, w_out, xs, gt_s, fg)
    if radix4:
        xs = _from_residue_major(xs, 1)
    return (xp, xs, jnp.stack(new_k, axis=1), jnp.stack(new_v, axis=1))
```

```python
import functools
import math

import jax
import jax.numpy as jnp
from jax import lax
from jax.experimental import pallas as pl
from jax.experimental.pallas import tpu as pltpu

N_HEADS = 8
N_FOURIER_GROUPS = 8
GRID_W = 64
ROPE_BASE = 10000.0
EPS = 1e-6
N_MIXERS = 2

V7X_VMEM_BYTES = 64 * 1024 * 1024
VMEM_LIMIT_BYTES = V7X_VMEM_BYTES - 8 * 1024 * 1024
LANES = 128
SUBLANES = 8

BF16 = jnp.bfloat16
F32 = jnp.float32


def _tile(n, pref):
    t = min(n, pref)
    assert n % t == 0, (n, t)
    return t


def _params(n_axes):
    return pltpu.CompilerParams(
        dimension_semantics=("arbitrary",) * n_axes,
        vmem_limit_bytes=VMEM_LIMIT_BYTES,
    )


def _resident(block_shape, index_map):
    return pl.BlockSpec(block_shape, index_map, pipeline_mode=pl.Buffered(1))


def _to_bf16_kernel(x_ref, o_ref):
    o_ref[...] = x_ref[...].astype(BF16)


def _to_bf16(w):
    l, r, c = w.shape
    tr = _tile(r, max(8, (4 * 1024 * 1024) // (4 * c)))
    return pl.pallas_call(
        _to_bf16_kernel,
        grid=(l, r // tr),
        in_specs=[pl.BlockSpec((None, tr, c), lambda li, i: (li, i, 0))],
        out_specs=pl.BlockSpec((None, tr, c), lambda li, i: (li, i, 0)),
        out_shape=jax.ShapeDtypeStruct(w.shape, BF16),
        compiler_params=_params(2),
        name="to_bf16",
    )(w)


def _adaln_kernel(cond_ref, w_ref, b_ref, out_ref):
    cond = cond_ref[...]
    act = cond * jax.nn.sigmoid(cond)
    out_ref[...] = jnp.dot(act, w_ref[...], preferred_element_type=F32,
                           precision=lax.Precision.HIGHEST) + b_ref[...]


def _adaln(cond, w_ada, b_ada):
    depth, d, d3 = w_ada.shape
    r = cond.shape[0]
    tn = _tile(d3, 1024)
    return pl.pallas_call(
        _adaln_kernel,
        grid=(depth, d3 // tn),
        in_specs=[
            pl.BlockSpec((r, d), lambda i, j: (0, 0)),
            pl.BlockSpec((None, d, tn), lambda i, j: (i, 0, j)),
            pl.BlockSpec((None, 1, tn), lambda i, j: (i, 0, j)),
        ],
        out_specs=pl.BlockSpec((None, r, tn), lambda i, j: (i, 0, j)),
        out_shape=jax.ShapeDtypeStruct((depth, r, d3), F32),
        compiler_params=_params(2),
        name="adaln",
    )(cond, w_ada, b_ada.reshape(depth, 1, d3))


def _modulated_norm(x, ng, sc, sh):
    ms = jnp.mean(x * x, axis=-1, keepdims=True)
    h = (x * lax.rsqrt(ms + EPS)) * (ng * (1.0 + sc)) + sh
    return h.astype(BF16)


def _silu(x):
    return x * jax.nn.sigmoid(x)


def _fourier_in_kernel(x_ref, ng_ref, sc_ref, sh_ref, w_ref, cs_ref, ab_ref, sg_ref, *, e, gd):
    hb = _modulated_norm(x_ref[...], ng_ref[...], sc_ref[...], sh_ref[...])
    for grp in range(e // gd):
        cols = slice(grp * gd, (grp + 1) * gd)
        u = jnp.dot(hb, w_ref[:, cols], preferred_element_type=F32)
        ab = jnp.dot(u.astype(BF16), cs_ref[...], preferred_element_type=F32)
        ab_ref[0, :, cols] = ab[:, :gd].astype(BF16)
        ab_ref[1, :, cols] = ab[:, gd:].astype(BF16)
    tn = _tile(e, 512)
    for j in range(e // tn):
        cols = slice(j * tn, (j + 1) * tn)
        g = jnp.dot(hb, w_ref[:, e + j * tn:e + (j + 1) * tn], preferred_element_type=F32)
        sg_ref[:, cols] = _silu(g).astype(BF16)


def _fourier_in(x, ng, sc, sh, w_in, cs):
    b, n, d = x.shape
    e = w_in.shape[1] // 2
    gd = e // N_FOURIER_GROUPS
    tm = _tile(n, 512)
    return pl.pallas_call(
        functools.partial(_fourier_in_kernel, e=e, gd=gd),
        grid=(b, n // tm),
        in_specs=[
            pl.BlockSpec((None, tm, d), lambda bi, i: (bi, i, 0)),
            _resident((1, d), lambda bi, i: (0, 0)),
            pl.BlockSpec((None, 1, d), lambda bi, i: (bi, 0, 0)),
            pl.BlockSpec((None, 1, d), lambda bi, i: (bi, 0, 0)),
            _resident((d, 2 * e), lambda bi, i: (0, 0)),
            _resident((gd, 2 * gd), lambda bi, i: (0, 0)),
        ],
        out_specs=[
            pl.BlockSpec((None, 2, tm, e), lambda bi, i: (bi, 0, i, 0)),
            pl.BlockSpec((None, tm, e), lambda bi, i: (bi, i, 0)),
        ],
        out_shape=[
            jax.ShapeDtypeStruct((b, 2, n, e), BF16),
            jax.ShapeDtypeStruct((b, n, e), BF16),
        ],
        compiler_params=_params(2),
        name="fourier_in",
    )(x, ng, sc, sh, w_in, cs)


def _seq_dft_kernel(d_ref, ab_ref, sg_ref, out_ref, acc_ref):
    k = pl.program_id(2)
    part = jnp.dot(d_ref[...], ab_ref[...], preferred_element_type=F32)

    @pl.when(k == 0)
    def _():
        acc_ref[...] = part

    @pl.when(k > 0)
    def _():
        acc_ref[...] += part

    @pl.when(k == pl.num_programs(2) - 1)
    def _():
        out_ref[...] = (acc_ref[...] * sg_ref[...].astype(F32)).astype(BF16)


def _seq_dft(dmat, ab, sg):
    b, n2, e = ab.shape
    n = n2 // 2
    tm = _tile(n, 1024)
    tk = _tile(n2, 1024)
    return pl.pallas_call(
        _seq_dft_kernel,
        grid=(b, n // tm, n2 // tk),
        in_specs=[
            pl.BlockSpec((tm, tk), lambda bi, i, k: (i, k)),
            pl.BlockSpec((None, tk, e), lambda bi, i, k: (bi, k, 0)),
            pl.BlockSpec((None, tm, e), lambda bi, i, k: (bi, i, 0)),
        ],
        out_specs=pl.BlockSpec((None, tm, e), lambda bi, i, k: (bi, i, 0)),
        out_shape=jax.ShapeDtypeStruct((b, n, e), BF16),
        scratch_shapes=[pltpu.VMEM((tm, e), F32)],
        compiler_params=_params(3),
        name="seq_dft",
    )(dmat, ab, sg)


SEQ_RADIX = 8


def _twiddle(z, j, r):
    x, y = z
    if j == 0:
        return z
    if 4 * j == r:
        return y, -x
    if 8 * j == r:
        return (x + y) * math.sqrt(0.5), (y - x) * math.sqrt(0.5)
    if 8 * j == 3 * r:
        return (y - x) * math.sqrt(0.5), (-x - y) * math.sqrt(0.5)
    c, sn = math.cos(2.0 * math.pi * j / r), math.sin(2.0 * math.pi * j / r)
    return x * c + y * sn, y * c - x * sn


def _dft_blocks(z):
    r = len(z)
    if r == 1:
        return z
    h = r // 2
    even = _dft_blocks([(z[j][0] + z[j + h][0], z[j][1] + z[j + h][1]) for j in range(h)])
    odd = _dft_blocks([_twiddle((z[j][0] - z[j + h][0], z[j][1] - z[j + h][1]), j, r) for j in range(h)])
    out = [None] * r
    out[0::2], out[1::2] = even, odd
    return out


def _fourier_in_rdx_kernel(x_ref, ng_ref, sc_ref, sh_ref, w_ref, cs_ref, y_ref, sg_ref, u_ref, *, e, gd, radix):
    hb = _modulated_norm(x_ref[...], ng_ref[...], sc_ref[...], sh_ref[...])
    q = x_ref.shape[0] // radix
    tn = _tile(e, 512)
    for j in range(e // tn):
        cols = slice(j * tn, (j + 1) * tn)
        u_ref[:, cols] = jnp.dot(hb, w_ref[:, cols], preferred_element_type=F32).astype(BF16)
    for grp in range(e // gd):
        cols = slice(grp * gd, (grp + 1) * gd)
        ab = jnp.dot(u_ref[:, cols], cs_ref[...], preferred_element_type=F32)
        y = _dft_blocks([(ab[j * q:(j + 1) * q, :gd], -ab[j * q:(j + 1) * q, gd:]) for j in range(radix)])
        for kappa in range(radix):
            y_ref[kappa, 0, :, cols] = y[kappa][0].astype(BF16)
            y_ref[kappa, 1, :, cols] = (-y[kappa][1]).astype(BF16)
    for j in range(e // tn):
        cols = slice(j * tn, (j + 1) * tn)
        g = jnp.dot(hb, w_ref[:, e + j * tn:e + (j + 1) * tn], preferred_element_type=F32)
        sg_ref[:, cols] = _silu(g).astype(BF16)


def _fourier_in_rdx(x, ng, sc, sh, w_in, cs):
    b, n, d = x.shape
    e = w_in.shape[1] // 2
    gd = e // N_FOURIER_GROUPS
    r = SEQ_RADIX
    tm = n // r
    return pl.pallas_call(
        functools.partial(_fourier_in_rdx_kernel, e=e, gd=gd, radix=r),
        grid=(b, r),
        in_specs=[
            pl.BlockSpec((None, tm, d), lambda bi, i: (bi, i, 0)),
            _resident((1, d), lambda bi, i: (0, 0)),
            pl.BlockSpec((None, 1, d), lambda bi, i: (bi, 0, 0)),
            pl.BlockSpec((None, 1, d), lambda bi, i: (bi, 0, 0)),
            _resident((d, 2 * e), lambda bi, i: (0, 0)),
            _resident((gd, 2 * gd), lambda bi, i: (0, 0)),
        ],
        out_specs=[
            pl.BlockSpec((None, r, 2, tm // r, e), lambda bi, i: (bi, 0, 0, i, 0)),
            pl.BlockSpec((None, tm, e), lambda bi, i: (bi, i, 0)),
        ],
        out_shape=[
            jax.ShapeDtypeStruct((b, r, 2, n // r, e), BF16),
            jax.ShapeDtypeStruct((b, n, e), BF16),
        ],
        scratch_shapes=[pltpu.VMEM((tm, e), BF16)],
        compiler_params=_params(2),
        name="fourier_in_rdx",
    )(x, ng, sc, sh, w_in, cs)


def _seq_dft_rdx_kernel(d_ref, y_ref, sg_ref, out_ref):
    tn = _tile(out_ref.shape[1], 512)
    for j in range(out_ref.shape[1] // tn):
        cols = slice(j * tn, (j + 1) * tn)
        f = jnp.dot(d_ref[...], y_ref[:, cols], preferred_element_type=F32)
        out_ref[:, cols] = (f * sg_ref[:, cols].astype(F32)).astype(BF16)


def _seq_dft_rdx(dmat, y, sg):
    b, r, n2, e = y.shape
    nr = n2 // 2
    return pl.pallas_call(
        _seq_dft_rdx_kernel,
        grid=(r, b),
        in_specs=[
            pl.BlockSpec((None, nr, n2), lambda kp, bi: (kp, 0, 0)),
            pl.BlockSpec((None, None, n2, e), lambda kp, bi: (bi, kp, 0, 0)),
            pl.BlockSpec((None, nr, e), lambda kp, bi: (bi, kp, 0)),
        ],
        out_specs=pl.BlockSpec((None, nr, e), lambda kp, bi: (bi, kp, 0)),
        out_shape=jax.ShapeDtypeStruct((b, r * nr, e), BF16),
        compiler_params=_params(2),
        name="seq_dft_rdx",
    )(dmat, y, sg)


def _seq_dft_rdx_matrices(n):
    r = SEQ_RADIX
    kk = r * jnp.arange(n // r, dtype=jnp.int32)[None, :, None] + jnp.arange(r, dtype=jnp.int32)[:, None, None]
    rho = jnp.arange(r, dtype=jnp.int32)[None, None, :]
    mr = r * jnp.arange(n // (r * r), dtype=jnp.int32)[None, None, :]
    ang_a = ((kk * mr) % n).astype(F32) * (2.0 * math.pi / n)
    ang_b = ((kk * rho) % n).astype(F32) * (2.0 * math.pi / n)
    ca, sa = jnp.cos(ang_a)[:, :, None, :], jnp.sin(ang_a)[:, :, None, :]
    cb, sb = jnp.cos(ang_b)[:, :, :, None], jnp.sin(ang_b)[:, :, :, None]
    c = (ca * cb - sa * sb).reshape(r, n // r, n // r)
    s = (sa * cb + ca * sb).reshape(r, n // r, n // r)
    return (jnp.concatenate([c, -s], axis=2) * (n ** -0.5)).astype(BF16)


def _to_residue_major(x, axis):
    n, r = x.shape[axis], SEQ_RADIX
    shape = x.shape[:axis] + (n // r, r) + x.shape[axis + 1:]
    return jnp.swapaxes(x.reshape(shape), axis, axis + 1).reshape(x.shape)


def _from_residue_major(x, axis):
    n, r = x.shape[axis], SEQ_RADIX
    shape = x.shape[:axis] + (r, n // r) + x.shape[axis + 1:]
    return jnp.swapaxes(x.reshape(shape), axis, axis + 1).reshape(x.shape)


def _dft_cos_sin(n):
    k = jnp.arange(n, dtype=jnp.int32)
    if n >= 1024 and n % 64 == 0:
        m1 = jnp.arange(n // 64, dtype=jnp.int32) * 64
        m2 = jnp.arange(64, dtype=jnp.int32)
        ang_a = ((k[:, None] * m1[None, :]) % n).astype(F32) * (2.0 * math.pi / n)
        ang_b = ((k[:, None] * m2[None, :]) % n).astype(F32) * (2.0 * math.pi / n)
        ca, sa = jnp.cos(ang_a)[:, :, None], jnp.sin(ang_a)[:, :, None]
        cb, sb = jnp.cos(ang_b)[:, None, :], jnp.sin(ang_b)[:, None, :]
        return (ca * cb - sa * sb).reshape(n, n), (sa * cb + ca * sb).reshape(n, n)
    ang = ((k[:, None] * k[None, :]) % n).astype(F32) * (2.0 * math.pi / n)
    return jnp.cos(ang), jnp.sin(ang)


def _seq_dft_matrix(n):
    c, s = _dft_cos_sin(n)
    return (jnp.concatenate([c, -s], axis=1) * (n ** -0.5)).astype(BF16)


def _chan_dft_matrix(gd):
    c, s = _dft_cos_sin(gd)
    return (jnp.concatenate([c, s], axis=1) * (gd ** -0.5)).astype(BF16)


def _rope(x, cos, sin_signed, first_half):
    fwd = pltpu.roll(x, LANES - 32, axis=1)
    bwd = pltpu.roll(x, 32, axis=1)
    return x * cos + jnp.where(first_half, fwd, bwd) * sin_signed


def _attn_in_kernel(*refs, e, hd, rope, q_scale):
    if rope:
        (x_ref, ng_ref, sc_ref, sh_ref, w_ref, cq_ref, sq_ref, ck_ref, sk_ref,
         q_ref, k_ref, v_ref, sg_ref) = refs
        lane = lax.broadcasted_iota(jnp.int32, (1, LANES), 1)
        first_half = (lane % 64) < 32
    else:
        x_ref, ng_ref, sc_ref, sh_ref, w_ref, q_ref, k_ref, v_ref, sg_ref = refs
    hb = _modulated_norm(x_ref[...], ng_ref[...], sc_ref[...], sh_ref[...])
    hw = 2 * hd
    for j in range(e // hw):
        q = jnp.dot(hb, w_ref[:, j * hw:(j + 1) * hw], preferred_element_type=F32)
        k = jnp.dot(hb, w_ref[:, e + j * hw:e + (j + 1) * hw], preferred_element_type=F32)
        for mp in range(2):
            cols = slice(j * hw + mp * hd, j * hw + (mp + 1) * hd)
            qm, km = q[:, mp * hd:(mp + 1) * hd], k[:, mp * hd:(mp + 1) * hd]
            if rope:
                qm = _rope(qm, cq_ref[...], sq_ref[...], first_half)
                km = _rope(km, ck_ref[...], sk_ref[...], first_half)
            else:
                qm = qm * q_scale
            q_ref[:, cols] = qm.astype(q_ref.dtype)
            k_ref[:, cols] = km.astype(k_ref.dtype)
    tn = _tile(e, 512)
    for j in range(e // tn):
        cols = slice(j * tn, (j + 1) * tn)
        v = jnp.dot(hb, w_ref[:, 2 * e + j * tn:2 * e + (j + 1) * tn], preferred_element_type=F32)
        v_ref[:, cols] = v.astype(v_ref.dtype)
        g = jnp.dot(hb, w_ref[:, 3 * e + j * tn:3 * e + (j + 1) * tn], preferred_element_type=F32)
        sg_ref[:, cols] = _silu(g).astype(BF16)


def _attn_in(x, ng, sc, sh, w_in, q_scale, rope_tables, kv_dtype):
    b, n, d = x.shape
    e = w_in.shape[1] // 4
    hd = e // (2 * N_HEADS)
    assert hd == LANES
    tm = _tile(n, 512)
    rope = rope_tables is not None
    in_specs = [
        pl.BlockSpec((None, tm, d), lambda bi, i: (bi, i, 0)),
        _resident((1, d), lambda bi, i: (0, 0)),
        pl.BlockSpec((None, 1, d), lambda bi, i: (bi, 0, 0)),
        pl.BlockSpec((None, 1, d), lambda bi, i: (bi, 0, 0)),
        _resident((d, 4 * e), lambda bi, i: (0, 0)),
    ]
    args = [x, ng, sc, sh, w_in]
    if rope:
        in_specs += [pl.BlockSpec((tm, hd), lambda bi, i: (i, 0))] * 4
        args += list(rope_tables)
    tok = pl.BlockSpec((None, tm, e), lambda bi, i: (bi, i, 0))
    return pl.pallas_call(
        functools.partial(_attn_in_kernel, e=e, hd=hd, rope=rope, q_scale=q_scale),
        grid=(b, n // tm),
        in_specs=in_specs,
        out_specs=[tok, tok, tok, tok],
        out_shape=[
            jax.ShapeDtypeStruct((b, n, e), BF16),
            jax.ShapeDtypeStruct((b, n, e), kv_dtype),
            jax.ShapeDtypeStruct((b, n, e), kv_dtype),
            jax.ShapeDtypeStruct((b, n, e), BF16),
        ],
        compiler_params=_params(2),
        name="attn_in_rope" if rope else "attn_in",
    )(*args)


def _lane_blocks(x):
    return [x[:, j * LANES:(j + 1) * LANES] for j in range(x.shape[1] // LANES)]


def _fold_rows(op, x):
    return op(x.reshape(x.shape[0] // SUBLANES, SUBLANES, x.shape[1]), axis=0)


def _diff_attn_kernel(*refs, n_new, n_past, ck, hd, lam_init, n_tiles, heads):
    if n_past:
        (q_ref, qn_ref, k_ref, v_ref, pk_ref, pv_ref, sg_ref, lam_ref, sub_ref, out_ref,
         s_ref, p_ref, mcur_ref, vt_ref, r_ref, il_ref, o_ref) = refs
    else:
        (q_ref, qn_ref, k_ref, v_ref, sg_ref, lam_ref, sub_ref, out_ref,
         s_ref, p_ref, mcur_ref, vt_ref, r_ref, il_ref, o_ref) = refs
    n_new_chunks = n_new // ck
    tq = q_ref.shape[0]
    nt = (((1,), (1,)), ((), ()))
    i = pl.program_id(2)
    first = i == 0
    last = i == n_tiles - 1
    pipelined = n_tiles > 1
    cold = first if not pipelined else jnp.logical_and(
        first, jnp.logical_and(pl.program_id(0) == 0, pl.program_id(1) == 0))
    hw = 2 * hd

    def run_head(col0):
        hcols = slice(col0, col0 + hw)

        def rows_of(c):
            if isinstance(c, int):
                return slice(c * ck, (c + 1) * ck)
            return pl.ds(pl.multiple_of(c * ck, ck), ck)

        def new_chunk(c):
            return (lambda mp: (mp, c)), (lambda: k_ref[rows_of(c), hcols].astype(BF16)), (lambda: vt_ref[c])

        past_chunk = ((lambda mp: (mp, n_new_chunks, slice(0, n_past), slice(None))),
                      (lambda: pk_ref[:, hcols].astype(BF16)),
                      (lambda: vt_ref[n_new_chunks, :, 0:n_past]))

        def scores_chunk(src_ref, chunk, mrun):
            at, load_keys, _ = chunk
            kc = load_keys()
            out = []
            for mp in range(2):
                cols = slice(mp * hd, (mp + 1) * hd)
                qcols = slice(col0 + mp * hd, col0 + (mp + 1) * hd)
                s = lax.dot_general(kc[:, cols], src_ref[:, qcols], nt, preferred_element_type=F32)
                s_ref[at(mp)] = s
                out.append(jnp.maximum(mrun[mp], _fold_rows(jnp.max, s)))
            return tuple(out)

        def probs_chunk(chunk, lsum):
            at = chunk[0]
            out = []
            for mp in range(2):
                s = s_ref[at(mp)]
                s3 = s.reshape(s.shape[0] // SUBLANES, SUBLANES, tq)
                p3 = jnp.exp2(s3 - mcur_ref[mp][None])
                p_ref[at(mp)] = p3.reshape(s.shape).astype(BF16)
                out.append(lsum[mp] + jnp.sum(p3, axis=0))
            return tuple(out)

        def values_chunk(chunk):
            at, _, load_vt = chunk
            a = p_ref[at(0)] - r_ref[0:1, :].astype(BF16) * p_ref[at(1)]
            return jnp.dot(load_vt(), a, preferred_element_type=F32)

        def finish_stats(carry):
            l1 = jnp.sum(carry[0], axis=0, keepdims=True)
            l2 = jnp.sum(carry[1], axis=0, keepdims=True)
            lam_p = lam_ref[...]
            lam = (jnp.exp(jnp.sum(lam_p[0:1] * lam_p[1:2], axis=-1, keepdims=True))
                   - jnp.exp(jnp.sum(lam_p[2:3] * lam_p[3:4], axis=-1, keepdims=True)) + lam_init)
            r_ref[...] = jnp.broadcast_to(lam * l1 / l2, (SUBLANES, tq))
            il_ref[...] = jnp.broadcast_to(1.0 / l1, (SUBLANES, tq))
            if pipelined:
                for mp in range(2):
                    m = jnp.max(carry[2 + mp], axis=0, keepdims=True)
                    mcur_ref[mp] = jnp.broadcast_to(m, (SUBLANES, tq))

        def write_tile(o, tile):
            rows = pl.ds(pl.multiple_of(tile * tq, tq), tq)
            o = o * il_ref[0:1, :]
            o = o * lax.rsqrt(jnp.mean(o * o, axis=0, keepdims=True) + EPS)
            o = jnp.concatenate([blk * sub_ref[...] for blk in _lane_blocks(o)], axis=1)
            out_ref[rows, hcols] = ((o.T * (1.0 - lam_init)) * sg_ref[rows, hcols].astype(F32)).astype(BF16)

        neg = jnp.full((SUBLANES, tq), -jnp.inf, F32)
        zero = jnp.zeros((SUBLANES, tq), F32)

        def stage_loop(with_values):
            def body(chunks, carry):
                if with_values:
                    o_ref[...] += functools.reduce(jnp.add, [values_chunk(ch) for ch in chunks])
                for ch in chunks:
                    lsum = probs_chunk(ch, carry[:2])
                    mrun = scores_chunk(qn_ref, ch, carry[2:]) if pipelined else carry[2:]
                    carry = lsum + mrun
                return carry
            group = min(4, n_new_chunks)
            carry = lax.fori_loop(
                0, n_new_chunks // group,
                lambda cg, carry: body([new_chunk(cg * group + g) for g in range(group)], carry),
                (zero, zero, neg, neg))
            if n_past:
                carry = body([past_chunk], carry)
            return carry

        @pl.when(cold)
        def _():
            mrun = lax.fori_loop(0, n_new_chunks, lambda c, m: scores_chunk(q_ref, new_chunk(c), m), (neg, neg))
            if n_past:
                mrun = scores_chunk(q_ref, past_chunk, mrun)
            for mp in range(2):
                mcur_ref[mp] = jnp.broadcast_to(jnp.max(mrun[mp], axis=0, keepdims=True), (SUBLANES, tq))

        @pl.when(first)
        def _():
            for c in range(n_new_chunks):
                vt_ref[c] = v_ref[rows_of(c), hcols].astype(F32).T.astype(BF16)
            if n_past:
                vt_ref[n_new_chunks, :, 0:n_past] = pv_ref[:, hcols].astype(F32).T.astype(BF16)
            finish_stats(stage_loop(with_values=False))

        if pipelined:
            @pl.when(jnp.logical_not(first))
            def _():
                o_ref[...] = jnp.zeros(o_ref.shape, F32)
                carry = stage_loop(with_values=True)
                write_tile(o_ref[...], i - 1)
                finish_stats(carry)

        @pl.when(last)
        def _():
            o = None
            for c in range(n_new_chunks):
                d = values_chunk(new_chunk(c))
                o = d if o is None else o + d
            if n_past:
                o = o + values_chunk(past_chunk)
            write_tile(o, i)

    for hh in range(heads):
        run_head(hh * hw)


def _diff_attn(q, k, v, sg, lam_p, subln, lam_init, past=None):
    b, n, e = q.shape
    hw = e // N_HEADS
    hd = hw // 2
    tq = _tile(n, 512)
    ck = _tile(n, 512)
    n_tiles = n // tq
    n_past = 0 if past is None else past[0].shape[2]
    assert n_past <= ck and n_past % LANES == 0
    n_chunks = n // ck + (1 if n_past else 0)
    heads = N_HEADS if n_tiles == 1 else 1
    bw = heads * hw
    tok = pl.BlockSpec((None, tq, bw), lambda bi, h, i: (bi, i, h))
    head = pl.BlockSpec((None, n, bw), lambda bi, h, i: (bi, 0, h))

    def ahead(bi, h, i):
        if n_tiles == 1:
            return bi, h, i
        wrap = i == n_tiles - 1
        g = jnp.minimum(bi * N_HEADS + h + 1, b * N_HEADS - 1)
        return (jnp.where(wrap, g // N_HEADS, bi), jnp.where(wrap, g % N_HEADS, h),
                jnp.where(wrap, 0, i + 1))

    def tok_ahead(bi, h, i):
        nb, nh, ni = ahead(bi, h, i)
        return nb, ni, nh

    def head_ahead(bi, h, i):
        nb, nh, _ = ahead(bi, h, i)
        return nb, 0, nh

    in_specs = [tok, pl.BlockSpec((None, tq, bw), tok_ahead), pl.BlockSpec((None, n, bw), head_ahead), head]
    args = [q, q, k, v]
    scratch = [
        pltpu.VMEM((2, n_chunks, ck, tq), F32),
        pltpu.VMEM((2, n_chunks, ck, tq), BF16),
        pltpu.VMEM((2, SUBLANES, tq), F32),
        pltpu.VMEM((n_chunks, hw, ck), BF16),
        pltpu.VMEM((SUBLANES, tq), F32),
        pltpu.VMEM((SUBLANES, tq), F32),
        pltpu.VMEM((hw, tq), F32),
    ]
    if n_past:
        cache_k, cache_v, layer = past
        cached = pl.BlockSpec((None, None, n_past, bw), lambda bi, h, i: (bi, layer, 0, h))

        def cached_ahead(bi, h, i):
            nb, nh, _ = ahead(bi, h, i)
            return nb, layer, 0, nh

        in_specs += [pl.BlockSpec((None, None, n_past, bw), cached_ahead), cached]
        args += [cache_k, cache_v]
    in_specs += [head,
                 _resident((4, hd), lambda bi, h, i: (0, 0)),
                 _resident((hw, LANES), lambda bi, h, i: (0, 0))]
    args += [sg, lam_p, jnp.broadcast_to(subln.reshape(hw, 1), (hw, LANES))]
    return pl.pallas_call(
        functools.partial(_diff_attn_kernel, n_new=n, n_past=n_past, ck=ck, hd=hd, lam_init=lam_init,
                          n_tiles=n_tiles, heads=heads),
        grid=(b, N_HEADS // heads, n_tiles),
        in_specs=in_specs,
        out_specs=head,
        out_shape=jax.ShapeDtypeStruct((b, n, e), BF16),
        scratch_shapes=scratch,
        compiler_params=_params(3),
        name="diff_attn_past" if n_past else "diff_attn",
    )(*args)


def _rope_tables(n, hd, q_scale):
    axis_dim = hd // 2
    rows = n // GRID_W
    row = jnp.broadcast_to(jnp.arange(rows, dtype=F32)[:, None], (rows, GRID_W)).reshape(-1)
    colp = jnp.broadcast_to(jnp.arange(GRID_W, dtype=F32)[None, :], (rows, GRID_W)).reshape(-1)
    inv = ROPE_BASE ** (-jnp.arange(0, axis_dim, 2, dtype=F32) / axis_dim)
    ar = row[:, None] * inv[None, :]
    ac = colp[:, None] * inv[None, :]
    ang = jnp.concatenate([ar, ar, ac, ac], axis=-1)
    cos, sin = jnp.cos(ang), jnp.sin(ang)
    half = axis_dim // 2
    sign = jnp.where((jnp.arange(hd) % axis_dim) < half, -1.0, 1.0).astype(F32)
    sin_signed = sin * sign[None, :]
    return cos * q_scale, sin_signed * q_scale, cos, sin_signed


def _out_proj_kernel(*refs, final):
    if final:
        a_ref, w_ref, x_ref, gt_ref, fg_ref, y_ref = refs
    else:
        a_ref, w_ref, x_ref, gt_ref, y_ref = refs
    y = x_ref[...] + gt_ref[...] * jnp.dot(a_ref[...], w_ref[...], preferred_element_type=F32)
    if final:
        y = (y * lax.rsqrt(jnp.mean(y * y, axis=-1, keepdims=True) + EPS)) * fg_ref[...]
    y_ref[...] = y


def _out_proj(a, w_out, x, gate, final_g=None):
    b, n, d = x.shape
    e = a.shape[2]
    tm = _tile(n, 512)
    final = final_g is not None
    in_specs = [
        pl.BlockSpec((None, tm, e), lambda bi, i: (bi, i, 0)),
        _resident((e, d), lambda bi, i: (0, 0)),
        pl.BlockSpec((None, tm, d), lambda bi, i: (bi, i, 0)),
        pl.BlockSpec((None, 1, d), lambda bi, i: (bi, 0, 0)),
    ]
    args = [a, w_out, x, gate]
    if final:
        in_specs.append(_resident((1, d), lambda bi, i: (0, 0)))
        args.append(final_g)
    return pl.pallas_call(
        functools.partial(_out_proj_kernel, final=final),
        grid=(b, n // tm),
        in_specs=in_specs,
        out_specs=pl.BlockSpec((None, tm, d), lambda bi, i: (bi, i, 0)),
        out_shape=jax.ShapeDtypeStruct((b, n, d), F32),
        compiler_params=_params(2),
        name="out_proj_final" if final else "out_proj",
    )(*args)


def kernel(x_prompt, x_sample, cache_k, cache_v, c, c_ctx, norm_g, w_ada, b_ada, w_in_fourier, w_out_fourier, w_in_attn, w_out_attn, lam_q1, lam_k1, lam_q2, lam_k2, subln_g, final_norm_g):
    depth, d = norm_g.shape
    bp, n_ctx, _ = x_prompt.shape
    bs, n_lat, _ = x_sample.shape
    past_len = cache_k.shape[2]
    e = w_out_attn.shape[1]
    hd = e // (2 * N_HEADS)
    gd = e // N_FOURIER_GROUPS

    n_cond = 1 + bs
    rows = -(-n_cond // 8) * 8
    cond = jnp.concatenate([c_ctx[None, :], c, jnp.zeros((rows - n_cond, d), F32)], axis=0)
    mods = _adaln(cond, w_ada, b_ada).reshape(depth, rows, 3, d)

    q_scale = hd ** -0.5 * math.log2(math.e)
    tables = _rope_tables(n_lat, hd, q_scale)
    cs = _chan_dft_matrix(gd)
    dmat_p = _seq_dft_matrix(n_ctx)
    cache_k4 = cache_k.astype(BF16).reshape(bs, cache_k.shape[1], past_len, e)
    cache_v4 = cache_v.astype(BF16).reshape(bs, cache_v.shape[1], past_len, e)

    w_in_fourier, w_out_fourier = _to_bf16(w_in_fourier), _to_bf16(w_out_fourier)
    w_in_attn, w_out_attn = _to_bf16(w_in_attn), _to_bf16(w_out_attn)

    xp, xs = x_prompt, x_sample
    split_dft = n_lat % (16 * SEQ_RADIX ** 2) == 0
    if split_dft:
        xs = _to_residue_major(xs, 1)
        tables = tuple(_to_residue_major(t, 0) for t in tables)
        dmat_s = _seq_dft_rdx_matrices(n_lat)
    else:
        dmat_s = _seq_dft_matrix(n_lat)
    new_k, new_v = [], []
    for i in range(depth):
        ng = norm_g[i][None, :]
        sh_p, sc_p, gt_p = (jnp.broadcast_to(mods[i, 0:1, t][:, None, :], (bp, 1, d)) for t in range(3))
        sh_s, sc_s, gt_s = (mods[i, 1:1 + bs, t][:, None, :] for t in range(3))
        j = i // N_MIXERS
        last = i == depth - 1
        fg = final_norm_g[None, :] if last else None
        if i % N_MIXERS == 0:
            w_in = w_in_fourier[j]
            w_out = w_out_fourier[j]
            ab_p, sg_p = _fourier_in(xp, ng, sc_p, sh_p, w_in, cs)
            a_p = _seq_dft(dmat_p, ab_p.reshape(bp, 2 * n_ctx, e), sg_p)
            if split_dft:
                y_s, sg_s = _fourier_in_rdx(xs, ng, sc_s, sh_s, w_in, cs)
                a_s = _seq_dft_rdx(dmat_s, y_s.reshape(bs, SEQ_RADIX, 2 * n_lat // SEQ_RADIX, e), sg_s)
            else:
                ab_s, sg_s = _fourier_in(xs, ng, sc_s, sh_s, w_in, cs)
                a_s = _seq_dft(dmat_s, ab_s.reshape(bs, 2 * n_lat, e), sg_s)
        else:
            lam_init = 0.8 - 0.6 * math.exp(-0.3 * i)
            w_in = w_in_attn[j]
            w_out = w_out_attn[j]
            lam_p = jnp.stack([lam_q1[j], lam_k1[j], lam_q2[j], lam_k2[j]], axis=0)
            sub = subln_g[j][None, :]
            q_p, k_p, v_p, sg_p = _attn_in(xp, ng, sc_p, sh_p, w_in, q_scale, None, F32)
            new_k.append(k_p.reshape(bp, n_ctx, N_HEADS, 2 * hd))
            new_v.append(v_p.reshape(bp, n_ctx, N_HEADS, 2 * hd))
            a_p = _diff_attn(q_p, k_p, v_p, sg_p, lam_p, sub, lam_init)
            q_s, k_s, v_s, sg_s = _attn_in(xs, ng, sc_s, sh_s, w_in, q_scale, tables, BF16)
            a_s = _diff_attn(q_s, k_s, v_s, sg_s, lam_p, sub, lam_init,
                             past=(cache_k4, cache_v4, j))
        xp = _out_proj(a_p, w_out, xp, gt_p, fg)
        xs = _out_proj(a_s, w_out, xs, gt_s, fg)
    if split_dft:
        xs = _from_residue_major(xs, 1)
    return (xp, xs, jnp.stack(new_k, axis=1), jnp.stack(new_v, axis=1))
```

```python
import functools
import math

import jax
import jax.numpy as jnp
from jax import lax
from jax.experimental import pallas as pl
from jax.experimental.pallas import tpu as pltpu

N_HEADS = 8
N_FOURIER_GROUPS = 8
GRID_W = 64
ROPE_BASE = 10000.0
EPS = 1e-6
N_MIXERS = 2

V7X_VMEM_BYTES = 64 * 1024 * 1024
VMEM_LIMIT_BYTES = V7X_VMEM_BYTES - 8 * 1024 * 1024
LANES = 128
SUBLANES = 8

BF16 = jnp.bfloat16
F32 = jnp.float32


def _tile(n, pref):
    t = min(n, pref)
    assert n % t == 0, (n, t)
    return t


def _params(n_axes):
    return pltpu.CompilerParams(
        dimension_semantics=("arbitrary",) * n_axes,
        vmem_limit_bytes=VMEM_LIMIT_BYTES,
    )


def _resident(block_shape, index_map):
    return pl.BlockSpec(block_shape, index_map, pipeline_mode=pl.Buffered(1))


def _layer_weights(w):
    stack, layer = w
    return stack.shape[1:], _resident((None,) + stack.shape[1:], lambda *_: (layer, 0, 0)), stack


def _to_bf16_kernel(x_ref, o_ref):
    o_ref[...] = x_ref[...].astype(BF16)


def _to_bf16(w):
    l, r, c = w.shape
    tr = _tile(r, max(8, (4 * 1024 * 1024) // (4 * c)))
    return pl.pallas_call(
        _to_bf16_kernel,
        grid=(l, r // tr),
        in_specs=[pl.BlockSpec((None, tr, c), lambda li, i: (li, i, 0))],
        out_specs=pl.BlockSpec((None, tr, c), lambda li, i: (li, i, 0)),
        out_shape=jax.ShapeDtypeStruct(w.shape, BF16),
        compiler_params=_params(2),
        name="to_bf16",
    )(w)


def _adaln_kernel(cond_ref, w_ref, b_ref, out_ref):
    cond = cond_ref[...]
    act = cond * jax.nn.sigmoid(cond)
    out_ref[...] = jnp.dot(act, w_ref[...], preferred_element_type=F32,
                           precision=lax.Precision.HIGHEST) + b_ref[...]


def _adaln(cond, w_ada, b_ada):
    depth, d, d3 = w_ada.shape
    r = cond.shape[0]
    tn = _tile(d3, 1024)
    return pl.pallas_call(
        _adaln_kernel,
        grid=(depth, d3 // tn),
        in_specs=[
            pl.BlockSpec((r, d), lambda i, j: (0, 0)),
            pl.BlockSpec((None, d, tn), lambda i, j: (i, 0, j)),
            pl.BlockSpec((None, 1, tn), lambda i, j: (i, 0, j)),
        ],
        out_specs=pl.BlockSpec((None, r, tn), lambda i, j: (i, 0, j)),
        out_shape=jax.ShapeDtypeStruct((depth, r, d3), F32),
        compiler_params=_params(2),
        name="adaln",
    )(cond, w_ada, b_ada.reshape(depth, 1, d3))


def _modulated_norm(x, ng, sc, sh):
    ms = jnp.mean(x * x, axis=-1, keepdims=True)
    h = (x * lax.rsqrt(ms + EPS)) * (ng * (1.0 + sc)) + sh
    return h.astype(BF16)


def _silu(x):
    return x * jax.nn.sigmoid(x)


def _fourier_in_kernel(x_ref, ng_ref, sc_ref, sh_ref, w_ref, cs_ref, ab_ref, sg_ref, u_ref, *, e, gd):
    hb = _modulated_norm(x_ref[...], ng_ref[...], sc_ref[...], sh_ref[...])
    tn = _tile(e, 512)
    for j in range(e // tn):
        cols = slice(j * tn, (j + 1) * tn)
        u_ref[:, cols] = jnp.dot(hb, w_ref[:, cols], preferred_element_type=F32).astype(BF16)
    for grp in range(e // gd):
        cols = slice(grp * gd, (grp + 1) * gd)
        ab = jnp.dot(u_ref[:, cols], cs_ref[...], preferred_element_type=F32)
        ab_ref[0, :, cols] = ab[:, :gd].astype(BF16)
        ab_ref[1, :, cols] = ab[:, gd:].astype(BF16)
    for j in range(e // tn):
        cols = slice(j * tn, (j + 1) * tn)
        g = jnp.dot(hb, w_ref[:, e + j * tn:e + (j + 1) * tn], preferred_element_type=F32)
        sg_ref[:, cols] = _silu(g).astype(BF16)


def _fourier_in(x, ng, sc, sh, w_in, cs):
    b, n, d = x.shape
    w_shape, w_spec, w_in = _layer_weights(w_in)
    e = w_shape[1] // 2
    gd = e // N_FOURIER_GROUPS
    tm = _tile(n, 512)
    return pl.pallas_call(
        functools.partial(_fourier_in_kernel, e=e, gd=gd),
        grid=(b, n // tm),
        in_specs=[
            pl.BlockSpec((None, tm, d), lambda bi, i: (bi, i, 0)),
            _resident((1, d), lambda bi, i: (0, 0)),
            pl.BlockSpec((None, 1, d), lambda bi, i: (bi, 0, 0)),
            pl.BlockSpec((None, 1, d), lambda bi, i: (bi, 0, 0)),
            w_spec,
            _resident((gd, 2 * gd), lambda bi, i: (0, 0)),
        ],
        out_specs=[
            pl.BlockSpec((None, 2, tm, e), lambda bi, i: (bi, 0, i, 0)),
            pl.BlockSpec((None, tm, e), lambda bi, i: (bi, i, 0)),
        ],
        out_shape=[
            jax.ShapeDtypeStruct((b, 2, n, e), BF16),
            jax.ShapeDtypeStruct((b, n, e), BF16),
        ],
        scratch_shapes=[pltpu.VMEM((tm, e), BF16)],
        compiler_params=_params(2),
        name="fourier_in",
    )(x, ng, sc, sh, w_in, cs)


def _seq_dft_kernel(d_ref, ab_ref, sg_ref, out_ref, acc_ref):
    k = pl.program_id(2)
    part = jnp.dot(d_ref[...], ab_ref[...], preferred_element_type=F32)

    @pl.when(k == 0)
    def _():
        acc_ref[...] = part

    @pl.when(k > 0)
    def _():
        acc_ref[...] += part

    @pl.when(k == pl.num_programs(2) - 1)
    def _():
        out_ref[...] = (acc_ref[...] * sg_ref[...].astype(F32)).astype(BF16)


def _seq_dft(dmat, ab, sg):
    b, n2, e = ab.shape
    n = n2 // 2
    tm = _tile(n, 1024)
    tk = _tile(n2, 1024)
    return pl.pallas_call(
        _seq_dft_kernel,
        grid=(b, n // tm, n2 // tk),
        in_specs=[
            pl.BlockSpec((tm, tk), lambda bi, i, k: (i, k)),
            pl.BlockSpec((None, tk, e), lambda bi, i, k: (bi, k, 0)),
            pl.BlockSpec((None, tm, e), lambda bi, i, k: (bi, i, 0)),
        ],
        out_specs=pl.BlockSpec((None, tm, e), lambda bi, i, k: (bi, i, 0)),
        out_shape=jax.ShapeDtypeStruct((b, n, e), BF16),
        scratch_shapes=[pltpu.VMEM((tm, e), F32)],
        compiler_params=_params(3),
        name="seq_dft",
    )(dmat, ab, sg)


SEQ_RADIX = 8


def _twiddle(z, j, r):
    x, y = z
    if j == 0:
        return z
    if 4 * j == r:
        return y, -x
    if 8 * j == r:
        return (x + y) * math.sqrt(0.5), (y - x) * math.sqrt(0.5)
    if 8 * j == 3 * r:
        return (y - x) * math.sqrt(0.5), (-x - y) * math.sqrt(0.5)
    c, sn = math.cos(2.0 * math.pi * j / r), math.sin(2.0 * math.pi * j / r)
    return x * c + y * sn, y * c - x * sn


def _dft_blocks(z):
    r = len(z)
    if r == 1:
        return z
    h = r // 2
    even = _dft_blocks([(z[j][0] + z[j + h][0], z[j][1] + z[j + h][1]) for j in range(h)])
    odd = _dft_blocks([_twiddle((z[j][0] - z[j + h][0], z[j][1] - z[j + h][1]), j, r) for j in range(h)])
    out = [None] * r
    out[0::2], out[1::2] = even, odd
    return out


def _fourier_in_rdx_kernel(x_ref, ng_ref, sc_ref, sh_ref, w_ref, cs_ref, y_ref, sg_ref, u_ref, *, e, gd, radix, res):
    hb = _modulated_norm(x_ref[...], ng_ref[...], sc_ref[...], sh_ref[...])
    rows = x_ref.shape[0] // res
    q = rows // radix
    tn = _tile(e, 512)
    for j in range(e // tn):
        cols = slice(j * tn, (j + 1) * tn)
        u_ref[:, cols] = jnp.dot(hb, w_ref[:, cols], preferred_element_type=F32).astype(BF16)
    for grp in range(e // gd):
        cols = slice(grp * gd, (grp + 1) * gd)
        ab = jnp.dot(u_ref[:, cols], cs_ref[...], preferred_element_type=F32)
        for t in range(res):
            blk = [slice(t * rows + j * q, t * rows + (j + 1) * q) for j in range(radix)]
            y = _dft_blocks([(ab[b, :gd], -ab[b, gd:]) for b in blk])
            for kappa in range(radix):
                y_ref[kappa, 0, t * q:(t + 1) * q, cols] = y[kappa][0].astype(BF16)
                y_ref[kappa, 1, t * q:(t + 1) * q, cols] = (-y[kappa][1]).astype(BF16)
    for j in range(e // tn):
        cols = slice(j * tn, (j + 1) * tn)
        g = jnp.dot(hb, w_ref[:, e + j * tn:e + (j + 1) * tn], preferred_element_type=F32)
        sg_ref[:, cols] = _silu(g).astype(BF16)


def _fourier_in_rdx(x, ng, sc, sh, w_in, cs):
    b, n, d = x.shape
    w_shape, w_spec, w_in = _layer_weights(w_in)
    e = w_shape[1] // 2
    gd = e // N_FOURIER_GROUPS
    r = SEQ_RADIX
    res = 2
    tm = res * n // r
    return pl.pallas_call(
        functools.partial(_fourier_in_rdx_kernel, e=e, gd=gd, radix=r, res=res),
        grid=(b, r // res),
        in_specs=[
            pl.BlockSpec((None, tm, d), lambda bi, i: (bi, i, 0)),
            _resident((1, d), lambda bi, i: (0, 0)),
            pl.BlockSpec((None, 1, d), lambda bi, i: (bi, 0, 0)),
            pl.BlockSpec((None, 1, d), lambda bi, i: (bi, 0, 0)),
            w_spec,
            _resident((gd, 2 * gd), lambda bi, i: (0, 0)),
        ],
        out_specs=[
            pl.BlockSpec((None, r, 2, tm // r, e), lambda bi, i: (bi, 0, 0, i, 0)),
            pl.BlockSpec((None, tm, e), lambda bi, i: (bi, i, 0)),
        ],
        out_shape=[
            jax.ShapeDtypeStruct((b, r, 2, n // r, e), BF16),
            jax.ShapeDtypeStruct((b, n, e), BF16),
        ],
        scratch_shapes=[pltpu.VMEM((tm, e), BF16)],
        compiler_params=_params(2),
        name="fourier_in_rdx",
    )(x, ng, sc, sh, w_in, cs)


def _seq_dft_rdx_kernel(d_ref, y_ref, sg_ref, out_ref):
    tn = _tile(out_ref.shape[1], 512)
    for j in range(out_ref.shape[1] // tn):
        cols = slice(j * tn, (j + 1) * tn)
        f = jnp.dot(d_ref[...], y_ref[:, cols], preferred_element_type=F32)
        out_ref[:, cols] = (f * sg_ref[:, cols].astype(F32)).astype(BF16)


def _seq_dft_rdx(dmat, y, sg):
    b, r, n2, e = y.shape
    nr = n2 // 2
    return pl.pallas_call(
        _seq_dft_rdx_kernel,
        grid=(r, b),
        in_specs=[
            pl.BlockSpec((None, nr, n2), lambda kp, bi: (kp, 0, 0)),
            pl.BlockSpec((None, None, n2, e), lambda kp, bi: (bi, kp, 0, 0)),
            pl.BlockSpec((None, nr, e), lambda kp, bi: (bi, kp, 0)),
        ],
        out_specs=pl.BlockSpec((None, nr, e), lambda kp, bi: (bi, kp, 0)),
        out_shape=jax.ShapeDtypeStruct((b, r * nr, e), BF16),
        compiler_params=_params(2),
        name="seq_dft_rdx",
    )(dmat, y, sg)


def _seq_dft_out_kernel(d_ref, y_ref, sg_ref, w_ref, x_ref, gt_ref, out_ref, a_ref):
    tn = _tile(a_ref.shape[1], 512)
    for j in range(a_ref.shape[1] // tn):
        cols = slice(j * tn, (j + 1) * tn)
        f = jnp.dot(d_ref[...], y_ref[:, cols], preferred_element_type=F32)
        a_ref[:, cols] = (f * sg_ref[:, cols].astype(F32)).astype(BF16)
    out_ref[...] = x_ref[...] + gt_ref[...] * jnp.dot(a_ref[...], w_ref[...], preferred_element_type=F32)


def _seq_dft_out(dmat, y, sg, w_out, x, gate):
    b, r, n2, e = y.shape
    nr = n2 // 2
    d = x.shape[2]
    _, w_spec, w_out = _layer_weights(w_out)
    return pl.pallas_call(
        _seq_dft_out_kernel,
        grid=(r, b),
        in_specs=[
            pl.BlockSpec((None, nr, n2), lambda kp, bi: (kp, 0, 0)),
            pl.BlockSpec((None, None, n2, e), lambda kp, bi: (bi, kp, 0, 0)),
            pl.BlockSpec((None, nr, e), lambda kp, bi: (bi, kp, 0)),
            w_spec,
            pl.BlockSpec((None, nr, d), lambda kp, bi: (bi, kp, 0)),
            pl.BlockSpec((None, 1, d), lambda kp, bi: (bi, 0, 0)),
        ],
        out_specs=pl.BlockSpec((None, nr, d), lambda kp, bi: (bi, kp, 0)),
        out_shape=jax.ShapeDtypeStruct(x.shape, F32),
        scratch_shapes=[pltpu.VMEM((nr, e), BF16)],
        compiler_params=_params(2),
        name="seq_dft_out",
    )(dmat, y, sg, w_out, x, gate)


def _seq_dft_rdx_matrices(n):
    r = SEQ_RADIX
    kk = r * jnp.arange(n // r, dtype=jnp.int32)[None, :, None] + jnp.arange(r, dtype=jnp.int32)[:, None, None]
    rho = jnp.arange(r, dtype=jnp.int32)[None, None, :]
    mr = r * jnp.arange(n // (r * r), dtype=jnp.int32)[None, None, :]
    ang_a = ((kk * mr) % n).astype(F32) * (2.0 * math.pi / n)
    ang_b = ((kk * rho) % n).astype(F32) * (2.0 * math.pi / n)
    ca, sa = jnp.cos(ang_a)[:, :, None, :], jnp.sin(ang_a)[:, :, None, :]
    cb, sb = jnp.cos(ang_b)[:, :, :, None], jnp.sin(ang_b)[:, :, :, None]
    c = (ca * cb - sa * sb).reshape(r, n // r, n // r)
    s = (sa * cb + ca * sb).reshape(r, n // r, n // r)
    return (jnp.concatenate([c, -s], axis=2) * (n ** -0.5)).astype(BF16)


def _to_residue_major(x, axis):
    n, r = x.shape[axis], SEQ_RADIX
    shape = x.shape[:axis] + (n // r, r) + x.shape[axis + 1:]
    return jnp.swapaxes(x.reshape(shape), axis, axis + 1).reshape(x.shape)


def _from_residue_major(x, axis):
    n, r = x.shape[axis], SEQ_RADIX
    shape = x.shape[:axis] + (r, n // r) + x.shape[axis + 1:]
    return jnp.swapaxes(x.reshape(shape), axis, axis + 1).reshape(x.shape)


def _dft_cos_sin(n):
    k = jnp.arange(n, dtype=jnp.int32)
    if n >= 1024 and n % 64 == 0:
        m1 = jnp.arange(n // 64, dtype=jnp.int32) * 64
        m2 = jnp.arange(64, dtype=jnp.int32)
        ang_a = ((k[:, None] * m1[None, :]) % n).astype(F32) * (2.0 * math.pi / n)
        ang_b = ((k[:, None] * m2[None, :]) % n).astype(F32) * (2.0 * math.pi / n)
        ca, sa = jnp.cos(ang_a)[:, :, None], jnp.sin(ang_a)[:, :, None]
        cb, sb = jnp.cos(ang_b)[:, None, :], jnp.sin(ang_b)[:, None, :]
        return (ca * cb - sa * sb).reshape(n, n), (sa * cb + ca * sb).reshape(n, n)
    ang = ((k[:, None] * k[None, :]) % n).astype(F32) * (2.0 * math.pi / n)
    return jnp.cos(ang), jnp.sin(ang)


def _seq_dft_matrix(n):
    c, s = _dft_cos_sin(n)
    return (jnp.concatenate([c, -s], axis=1) * (n ** -0.5)).astype(BF16)


def _chan_dft_matrix(gd):
    c, s = _dft_cos_sin(gd)
    return (jnp.concatenate([c, s], axis=1) * (gd ** -0.5)).astype(BF16)


def _rope(x, cos, sin_signed, first_half):
    fwd = pltpu.roll(x, LANES - 32, axis=1)
    bwd = pltpu.roll(x, 32, axis=1)
    return x * cos + jnp.where(first_half, fwd, bwd) * sin_signed


def _attn_in_kernel(*refs, e, hd, rope, q_scale):
    if rope:
        (x_ref, ng_ref, sc_ref, sh_ref, w_ref, wvt_ref, cq_ref, sq_ref, ck_ref, sk_ref,
         q_ref, k_ref, v_ref, sg_ref) = refs
        lane = lax.broadcasted_iota(jnp.int32, (1, LANES), 1)
        first_half = (lane % 64) < 32
    else:
        x_ref, ng_ref, sc_ref, sh_ref, w_ref, q_ref, k_ref, v_ref, sg_ref = refs
    hb = _modulated_norm(x_ref[...], ng_ref[...], sc_ref[...], sh_ref[...])
    hw = 2 * hd
    for j in range(e // hw):
        q = jnp.dot(hb, w_ref[:, j * hw:(j + 1) * hw], preferred_element_type=F32)
        k = jnp.dot(hb, w_ref[:, e + j * hw:e + (j + 1) * hw], preferred_element_type=F32)
        for mp in range(2):
            cols = slice(j * hw + mp * hd, j * hw + (mp + 1) * hd)
            qm, km = q[:, mp * hd:(mp + 1) * hd], k[:, mp * hd:(mp + 1) * hd]
            if rope:
                qm = _rope(qm, cq_ref[...], sq_ref[...], first_half)
                km = _rope(km, ck_ref[...], sk_ref[...], first_half)
            else:
                qm = qm * q_scale
            q_ref[:, cols] = qm.astype(q_ref.dtype)
            k_ref[:, cols] = km.astype(k_ref.dtype)
    tn = _tile(e, 512)
    for j in range(e // tn):
        cols = slice(j * tn, (j + 1) * tn)
        if rope:
            vt = lax.dot_general(wvt_ref[cols, :], hb, (((1,), (1,)), ((), ())), preferred_element_type=F32)
            v_ref[cols, :] = vt.astype(v_ref.dtype)
        else:
            v = jnp.dot(hb, w_ref[:, 2 * e + j * tn:2 * e + (j + 1) * tn], preferred_element_type=F32)
            v_ref[:, cols] = v.astype(v_ref.dtype)
        g = jnp.dot(hb, w_ref[:, 3 * e + j * tn:3 * e + (j + 1) * tn], preferred_element_type=F32)
        sg_ref[:, cols] = _silu(g).astype(BF16)


def _attn_in(x, ng, sc, sh, w_in, q_scale, rope_tables, kv_dtype):
    b, n, d = x.shape
    layer = w_in[1]
    w_shape, w_spec, w_in = _layer_weights(w_in)
    e = w_shape[1] // 4
    hd = e // (2 * N_HEADS)
    assert hd == LANES
    tm = _tile(n, 512)
    rope = rope_tables is not None
    in_specs = [
        pl.BlockSpec((None, tm, d), lambda bi, i: (bi, i, 0)),
        _resident((1, d), lambda bi, i: (0, 0)),
        pl.BlockSpec((None, 1, d), lambda bi, i: (bi, 0, 0)),
        pl.BlockSpec((None, 1, d), lambda bi, i: (bi, 0, 0)),
        w_spec,
    ]
    args = [x, ng, sc, sh, w_in]
    tok = pl.BlockSpec((None, tm, e), lambda bi, i: (bi, i, 0))
    v_spec, v_shape = tok, jax.ShapeDtypeStruct((b, n, e), kv_dtype)
    if rope:
        in_specs += [_resident((e, d), lambda bi, i: (0, 0))]
        in_specs += [pl.BlockSpec((tm, hd), lambda bi, i: (i, 0))] * 4
        args += [w_in[layer, :, 2 * e:3 * e].T] + list(rope_tables)
        v_spec = pl.BlockSpec((None, None, e, tm), lambda bi, i: (bi, i, 0, 0))
        v_shape = jax.ShapeDtypeStruct((b, n // tm, e, tm), kv_dtype)
    return pl.pallas_call(
        functools.partial(_attn_in_kernel, e=e, hd=hd, rope=rope, q_scale=q_scale),
        grid=(b, n // tm),
        in_specs=in_specs,
        out_specs=[tok, tok, v_spec, tok],
        out_shape=[
            jax.ShapeDtypeStruct((b, n, e), BF16),
            jax.ShapeDtypeStruct((b, n, e), kv_dtype),
            v_shape,
            jax.ShapeDtypeStruct((b, n, e), BF16),
        ],
        compiler_params=_params(2),
        name="attn_in_rope" if rope else "attn_in",
    )(*args)


def _lane_blocks(x):
    return [x[:, j * LANES:(j + 1) * LANES] for j in range(x.shape[1] // LANES)]


def _fold_rows(op, x):
    return op(x.reshape(x.shape[0] // SUBLANES, SUBLANES, x.shape[1]), axis=0)


def _diff_attn_kernel(*refs, n_new, n_past, ck, hd, lam_init, n_tiles, heads, vt_in):
    if n_past:
        (q_ref, qn_ref, k_ref, v_ref, pk_ref, pv_ref, sg_ref, lam_ref, sub_ref, out_ref,
         s_ref, p_ref, mcur_ref, vt_ref, r_ref, il_ref, o_ref) = refs
    else:
        (q_ref, qn_ref, k_ref, v_ref, sg_ref, lam_ref, sub_ref, out_ref,
         s_ref, p_ref, mcur_ref, vt_ref, r_ref, il_ref, o_ref) = refs
    n_new_chunks = n_new // ck
    tq = q_ref.shape[0]
    nt = (((1,), (1,)), ((), ()))
    i = pl.program_id(2)
    first = i == 0
    last = i == n_tiles - 1
    pipelined = n_tiles > 1
    cold = first if not pipelined else jnp.logical_and(
        first, jnp.logical_and(pl.program_id(0) == 0, pl.program_id(1) == 0))
    hw = 2 * hd

    def run_head(col0):
        hcols = slice(col0, col0 + hw)

        def rows_of(c):
            if isinstance(c, int):
                return slice(c * ck, (c + 1) * ck)
            return pl.ds(pl.multiple_of(c * ck, ck), ck)

        def new_chunk(c):
            vt_src = v_ref if vt_in else vt_ref
            return (lambda mp: (mp, c)), (lambda: k_ref[rows_of(c), hcols].astype(BF16)), (lambda: vt_src[c])

        past_chunk = ((lambda mp: (mp, n_new_chunks, slice(0, n_past), slice(None))),
                      (lambda: pk_ref[:, hcols].astype(BF16)),
                      (lambda: vt_ref[n_new_chunks, :, 0:n_past]))

        def scores_chunk(src_ref, chunk, mrun):
            at, load_keys, _ = chunk
            kc = load_keys()
            out = []
            for mp in range(2):
                cols = slice(mp * hd, (mp + 1) * hd)
                qcols = slice(col0 + mp * hd, col0 + (mp + 1) * hd)
                s = lax.dot_general(kc[:, cols], src_ref[:, qcols], nt, preferred_element_type=F32)
                s_ref[at(mp)] = s
                out.append(jnp.maximum(mrun[mp], _fold_rows(jnp.max, s)))
            return tuple(out)

        def probs_chunk(chunk, lsum):
            at = chunk[0]
            out = []
            for mp in range(2):
                s = s_ref[at(mp)]
                s3 = s.reshape(s.shape[0] // SUBLANES, SUBLANES, tq)
                p3 = jnp.exp2(s3 - mcur_ref[mp][None])
                p_ref[at(mp)] = p3.reshape(s.shape).astype(BF16)
                out.append(lsum[mp] + jnp.sum(p3, axis=0))
            return tuple(out)

        def values_chunk(chunk):
            at, _, load_vt = chunk
            a = p_ref[at(0)] - r_ref[0:1, :].astype(BF16) * p_ref[at(1)]
            return jnp.dot(load_vt(), a, preferred_element_type=F32)

        def finish_stats(carry):
            l1 = jnp.sum(carry[0], axis=0, keepdims=True)
            l2 = jnp.sum(carry[1], axis=0, keepdims=True)
            lam_p = lam_ref[...]
            lam = (jnp.exp(jnp.sum(lam_p[0:1] * lam_p[1:2], axis=-1, keepdims=True))
                   - jnp.exp(jnp.sum(lam_p[2:3] * lam_p[3:4], axis=-1, keepdims=True)) + lam_init)
            r_ref[...] = jnp.broadcast_to(lam * l1 / l2, (SUBLANES, tq))
            il_ref[...] = jnp.broadcast_to(1.0 / l1, (SUBLANES, tq))
            if pipelined:
                for mp in range(2):
                    m = jnp.max(carry[2 + mp], axis=0, keepdims=True)
                    mcur_ref[mp] = jnp.broadcast_to(m, (SUBLANES, tq))

        def write_tile(o, tile):
            rows = pl.ds(pl.multiple_of(tile * tq, tq), tq)
            o = o * il_ref[0:1, :]
            o = o * lax.rsqrt(jnp.mean(o * o, axis=0, keepdims=True) + EPS)
            o = jnp.concatenate([blk * sub_ref[...] for blk in _lane_blocks(o)], axis=1)
            out_ref[rows, hcols] = ((o.T * (1.0 - lam_init)) * sg_ref[rows, hcols].astype(F32)).astype(BF16)

        neg = jnp.full((SUBLANES, tq), -jnp.inf, F32)
        zero = jnp.zeros((SUBLANES, tq), F32)

        def stage_loop(with_values):
            def body(chunks, carry):
                if with_values:
                    o_ref[...] += functools.reduce(jnp.add, [values_chunk(ch) for ch in chunks])
                for ch in chunks:
                    lsum = probs_chunk(ch, carry[:2])
                    mrun = scores_chunk(qn_ref, ch, carry[2:]) if pipelined else carry[2:]
                    carry = lsum + mrun
                return carry
            group = min(4, n_new_chunks)
            carry = lax.fori_loop(
                0, n_new_chunks // group,
                lambda cg, carry: body([new_chunk(cg * group + g) for g in range(group)], carry),
                (zero, zero, neg, neg))
            if n_past:
                carry = body([past_chunk], carry)
            return carry

        @pl.when(cold)
        def _():
            mrun = lax.fori_loop(0, n_new_chunks, lambda c, m: scores_chunk(q_ref, new_chunk(c), m), (neg, neg))
            if n_past:
                mrun = scores_chunk(q_ref, past_chunk, mrun)
            for mp in range(2):
                mcur_ref[mp] = jnp.broadcast_to(jnp.max(mrun[mp], axis=0, keepdims=True), (SUBLANES, tq))

        @pl.when(first)
        def _():
            if not vt_in:
                for c in range(n_new_chunks):
                    vt_ref[c] = v_ref[rows_of(c), hcols].astype(F32).T.astype(BF16)
            if n_past:
                vt_ref[n_new_chunks, :, 0:n_past] = pv_ref[:, hcols].astype(F32).T.astype(BF16)
            finish_stats(stage_loop(with_values=False))

        if pipelined:
            @pl.when(jnp.logical_not(first))
            def _():
                o_ref[...] = jnp.zeros(o_ref.shape, F32)
                carry = stage_loop(with_values=True)
                write_tile(o_ref[...], i - 1)
                finish_stats(carry)

        @pl.when(last)
        def _():
            o = None
            for c in range(n_new_chunks):
                d = values_chunk(new_chunk(c))
                o = d if o is None else o + d
            if n_past:
                o = o + values_chunk(past_chunk)
            write_tile(o, i)

    for hh in range(heads):
        run_head(hh * hw)


def _diff_attn(q, k, v, sg, lam_p, subln, lam_init, past=None):
    b, n, e = q.shape
    hw = e // N_HEADS
    hd = hw // 2
    tq = _tile(n, 512)
    ck = _tile(n, 512)
    n_tiles = n // tq
    vt_in = v.ndim == 4
    assert not vt_in or v.shape[3] == ck
    n_past = 0 if past is None else past[0].shape[2]
    assert n_past <= ck and n_past % LANES == 0
    n_chunks = n // ck + (1 if n_past else 0)
    heads = N_HEADS if n_tiles == 1 else 1
    bw = heads * hw
    tok = pl.BlockSpec((None, tq, bw), lambda bi, h, i: (bi, i, h))
    head = pl.BlockSpec((None, n, bw), lambda bi, h, i: (bi, 0, h))

    def ahead(bi, h, i):
        if n_tiles == 1:
            return bi, h, i
        wrap = i == n_tiles - 1
        g = jnp.minimum(bi * N_HEADS + h + 1, b * N_HEADS - 1)
        return (jnp.where(wrap, g // N_HEADS, bi), jnp.where(wrap, g % N_HEADS, h),
                jnp.where(wrap, 0, i + 1))

    def tok_ahead(bi, h, i):
        nb, nh, ni = ahead(bi, h, i)
        return nb, ni, nh

    def head_ahead(bi, h, i):
        nb, nh, _ = ahead(bi, h, i)
        return nb, 0, nh

    v_spec = pl.BlockSpec((None, n // ck, bw, ck), lambda bi, h, i: (bi, 0, h, 0)) if vt_in else head
    in_specs = [tok, pl.BlockSpec((None, tq, bw), tok_ahead), pl.BlockSpec((None, n, bw), head_ahead), v_spec]
    args = [q, q, k, v]
    scratch = [
        pltpu.VMEM((2, n_chunks, ck, tq), F32),
        pltpu.VMEM((2, n_chunks, ck, tq), BF16),
        pltpu.VMEM((2, SUBLANES, tq), F32),
        pltpu.VMEM((n_chunks, hw, ck), BF16),
        pltpu.VMEM((SUBLANES, tq), F32),
        pltpu.VMEM((SUBLANES, tq), F32),
        pltpu.VMEM((hw, tq), F32),
    ]
    if n_past:
        cache_k, cache_v, layer = past
        cached = pl.BlockSpec((None, None, n_past, bw), lambda bi, h, i: (bi, layer, 0, h))

        def cached_ahead(bi, h, i):
            nb, nh, _ = ahead(bi, h, i)
            return nb, layer, 0, nh

        in_specs += [pl.BlockSpec((None, None, n_past, bw), cached_ahead), cached]
        args += [cache_k, cache_v]
    in_specs += [head,
                 _resident((4, hd), lambda bi, h, i: (0, 0)),
                 _resident((hw, LANES), lambda bi, h, i: (0, 0))]
    args += [sg, lam_p, jnp.broadcast_to(subln.reshape(hw, 1), (hw, LANES))]
    return pl.pallas_call(
        functools.partial(_diff_attn_kernel, n_new=n, n_past=n_past, ck=ck, hd=hd, lam_init=lam_init,
                          n_tiles=n_tiles, heads=heads, vt_in=vt_in),
        grid=(b, N_HEADS // heads, n_tiles),
        in_specs=in_specs,
        out_specs=head,
        out_shape=jax.ShapeDtypeStruct((b, n, e), BF16),
        scratch_shapes=scratch,
        compiler_params=_params(3),
        name="diff_attn_past" if n_past else "diff_attn",
    )(*args)


def _rope_tables(n, hd, q_scale):
    axis_dim = hd // 2
    rows = n // GRID_W
    row = jnp.broadcast_to(jnp.arange(rows, dtype=F32)[:, None], (rows, GRID_W)).reshape(-1)
    colp = jnp.broadcast_to(jnp.arange(GRID_W, dtype=F32)[None, :], (rows, GRID_W)).reshape(-1)
    inv = ROPE_BASE ** (-jnp.arange(0, axis_dim, 2, dtype=F32) / axis_dim)
    ar = row[:, None] * inv[None, :]
    ac = colp[:, None] * inv[None, :]
    ang = jnp.concatenate([ar, ar, ac, ac], axis=-1)
    cos, sin = jnp.cos(ang), jnp.sin(ang)
    half = axis_dim // 2
    sign = jnp.where((jnp.arange(hd) % axis_dim) < half, -1.0, 1.0).astype(F32)
    sin_signed = sin * sign[None, :]
    return cos * q_scale, sin_signed * q_scale, cos, sin_signed


def _out_proj_kernel(*refs, final):
    if final:
        a_ref, w_ref, x_ref, gt_ref, fg_ref, y_ref = refs
    else:
        a_ref, w_ref, x_ref, gt_ref, y_ref = refs
    y = x_ref[...] + gt_ref[...] * jnp.dot(a_ref[...], w_ref[...], preferred_element_type=F32)
    if final:
        y = (y * lax.rsqrt(jnp.mean(y * y, axis=-1, keepdims=True) + EPS)) * fg_ref[...]
    y_ref[...] = y


def _out_proj(a, w_out, x, gate, final_g=None):
    b, n, d = x.shape
    e = a.shape[2]
    tm = _tile(n, 512)
    final = final_g is not None
    _, w_spec, w_out = _layer_weights(w_out)
    in_specs = [
        pl.BlockSpec((None, tm, e), lambda bi, i: (bi, i, 0)),
        w_spec,
        pl.BlockSpec((None, tm, d), lambda bi, i: (bi, i, 0)),
        pl.BlockSpec((None, 1, d), lambda bi, i: (bi, 0, 0)),
    ]
    args = [a, w_out, x, gate]
    if final:
        in_specs.append(_resident((1, d), lambda bi, i: (0, 0)))
        args.append(final_g)
    return pl.pallas_call(
        functools.partial(_out_proj_kernel, final=final),
        grid=(b, n // tm),
        in_specs=in_specs,
        out_specs=pl.BlockSpec((None, tm, d), lambda bi, i: (bi, i, 0)),
        out_shape=jax.ShapeDtypeStruct((b, n, d), F32),
        compiler_params=_params(2),
        name="out_proj_final" if final else "out_proj",
    )(*args)


def kernel(x_prompt, x_sample, cache_k, cache_v, c, c_ctx, norm_g, w_ada, b_ada, w_in_fourier, w_out_fourier, w_in_attn, w_out_attn, lam_q1, lam_k1, lam_q2, lam_k2, subln_g, final_norm_g):
    depth, d = norm_g.shape
    bp, n_ctx, _ = x_prompt.shape
    bs, n_lat, _ = x_sample.shape
    past_len = cache_k.shape[2]
    e = w_out_attn.shape[1]
    hd = e // (2 * N_HEADS)
    gd = e // N_FOURIER_GROUPS

    n_cond = 1 + bs
    rows = -(-n_cond // 8) * 8
    cond = jnp.concatenate([c_ctx[None, :], c, jnp.zeros((rows - n_cond, d), F32)], axis=0)
    mods = _adaln(cond, w_ada, b_ada).reshape(depth, rows, 3, d)

    q_scale = hd ** -0.5 * math.log2(math.e)
    tables = _rope_tables(n_lat, hd, q_scale)
    cs = _chan_dft_matrix(gd)
    dmat_p = _seq_dft_matrix(n_ctx)
    cache_k4 = cache_k.astype(BF16).reshape(bs, cache_k.shape[1], past_len, e)
    cache_v4 = cache_v.astype(BF16).reshape(bs, cache_v.shape[1], past_len, e)

    w_in_fourier, w_out_fourier = _to_bf16(w_in_fourier), _to_bf16(w_out_fourier)
    w_in_attn, w_out_attn = _to_bf16(w_in_attn), _to_bf16(w_out_attn)

    xp, xs = x_prompt, x_sample
    split_dft = n_lat % (16 * SEQ_RADIX ** 2) == 0
    if split_dft:
        xs = _to_residue_major(xs, 1)
        tables = tuple(_to_residue_major(t, 0) for t in tables)
        dmat_s = _seq_dft_rdx_matrices(n_lat)
    else:
        dmat_s = _seq_dft_matrix(n_lat)
    new_k, new_v = [], []
    for i in range(depth):
        ng = norm_g[i][None, :]
        sh_p, sc_p, gt_p = (jnp.broadcast_to(mods[i, 0:1, t][:, None, :], (bp, 1, d)) for t in range(3))
        sh_s, sc_s, gt_s = (mods[i, 1:1 + bs, t][:, None, :] for t in range(3))
        j = i // N_MIXERS
        last = i == depth - 1
        fg = final_norm_g[None, :] if last else None
        if i % N_MIXERS == 0:
            w_in = (w_in_fourier, j)
            w_out = (w_out_fourier, j)
            ab_p, sg_p = _fourier_in(xp, ng, sc_p, sh_p, w_in, cs)
            a_p = _seq_dft(dmat_p, ab_p.reshape(bp, 2 * n_ctx, e), sg_p)
            if split_dft:
                y_s, sg_s = _fourier_in_rdx(xs, ng, sc_s, sh_s, w_in, cs)
                y_s = y_s.reshape(bs, SEQ_RADIX, 2 * n_lat // SEQ_RADIX, e)
                a_s = None if fg is None else _seq_dft_rdx(dmat_s, y_s, sg_s)
            else:
                ab_s, sg_s = _fourier_in(xs, ng, sc_s, sh_s, w_in, cs)
                a_s = _seq_dft(dmat_s, ab_s.reshape(bs, 2 * n_lat, e), sg_s)
        else:
            lam_init = 0.8 - 0.6 * math.exp(-0.3 * i)
            w_in = (w_in_attn, j)
            w_out = (w_out_attn, j)
            lam_p = jnp.stack([lam_q1[j], lam_k1[j], lam_q2[j], lam_k2[j]], axis=0)
            sub = subln_g[j][None, :]
            q_p, k_p, v_p, sg_p = _attn_in(xp, ng, sc_p, sh_p, w_in, q_scale, None, F32)
            new_k.append(k_p.reshape(bp, n_ctx, N_HEADS, 2 * hd))
            new_v.append(v_p.reshape(bp, n_ctx, N_HEADS, 2 * hd))
            a_p = _diff_attn(q_p, k_p, v_p, sg_p, lam_p, sub, lam_init)
            q_s, k_s, v_s, sg_s = _attn_in(xs, ng, sc_s, sh_s, w_in, q_scale, tables, BF16)
            a_s = _diff_attn(q_s, k_s, v_s, sg_s, lam_p, sub, lam_init,
                             past=(cache_k4, cache_v4, j))
        xp = _out_proj(a_p, w_out, xp, gt_p, fg)
        if a_s is None:
            xs = _seq_dft_out(dmat_s, y_s, sg_s, w_out, xs, gt_s)
        else:
            xs = _out_proj(a_s, w_out, xs, gt_s, fg)
    if split_dft:
        xs = _from_residue_major(xs, 1)
    return (xp, xs, jnp.stack(new_k, axis=1), jnp.stack(new_v, axis=1))
```

```python
import functools
import math

import jax
import jax.numpy as jnp
from jax import lax
from jax.experimental import pallas as pl
from jax.experimental.pallas import tpu as pltpu

N_HEADS = 8
N_FOURIER_GROUPS = 8
GRID_W = 64
ROPE_BASE = 10000.0
EPS = 1e-6
N_MIXERS = 2

V7X_VMEM_BYTES = 64 * 1024 * 1024
VMEM_LIMIT_BYTES = V7X_VMEM_BYTES - 8 * 1024 * 1024
LANES = 128
SUBLANES = 8

BF16 = jnp.bfloat16
F32 = jnp.float32


def _tile(n, pref):
    t = min(n, pref)
    assert n % t == 0, (n, t)
    return t


def _params(n_axes):
    return pltpu.CompilerParams(
        dimension_semantics=("arbitrary",) * n_axes,
        vmem_limit_bytes=VMEM_LIMIT_BYTES,
    )


def _resident(block_shape, index_map):
    return pl.BlockSpec(block_shape, index_map, pipeline_mode=pl.Buffered(1))


def _layer_weights(w):
    stack, layer = w
    return stack.shape[1:], _resident((None,) + stack.shape[1:], lambda *_: (layer, 0, 0)), stack


def _to_bf16_kernel(x_ref, o_ref):
    o_ref[...] = x_ref[...].astype(BF16)


def _to_bf16(w):
    l, r, c = w.shape
    tr = _tile(r, max(8, (4 * 1024 * 1024) // (4 * c)))
    return pl.pallas_call(
        _to_bf16_kernel,
        grid=(l, r // tr),
        in_specs=[pl.BlockSpec((None, tr, c), lambda li, i: (li, i, 0))],
        out_specs=pl.BlockSpec((None, tr, c), lambda li, i: (li, i, 0)),
        out_shape=jax.ShapeDtypeStruct(w.shape, BF16),
        compiler_params=_params(2),
        name="to_bf16",
    )(w)


def _adaln_kernel(cond_ref, w_ref, b_ref, out_ref):
    cond = cond_ref[...]
    act = cond * jax.nn.sigmoid(cond)
    out_ref[...] = jnp.dot(act, w_ref[...], preferred_element_type=F32,
                           precision=lax.Precision.HIGHEST) + b_ref[...]


def _adaln(cond, w_ada, b_ada):
    depth, d, d3 = w_ada.shape
    r = cond.shape[0]
    tn = _tile(d3, 1024)
    return pl.pallas_call(
        _adaln_kernel,
        grid=(depth, d3 // tn),
        in_specs=[
            pl.BlockSpec((r, d), lambda i, j: (0, 0)),
            pl.BlockSpec((None, d, tn), lambda i, j: (i, 0, j)),
            pl.BlockSpec((None, 1, tn), lambda i, j: (i, 0, j)),
        ],
        out_specs=pl.BlockSpec((None, r, tn), lambda i, j: (i, 0, j)),
        out_shape=jax.ShapeDtypeStruct((depth, r, d3), F32),
        compiler_params=_params(2),
        name="adaln",
    )(cond, w_ada, b_ada.reshape(depth, 1, d3))


def _modulated_norm(x, ng, sc, sh):
    ms = jnp.mean(x * x, axis=-1, keepdims=True)
    h = (x * lax.rsqrt(ms + EPS)) * (ng * (1.0 + sc)) + sh
    return h.astype(BF16)


def _silu(x):
    return x * jax.nn.sigmoid(x)


def _fourier_in_kernel(x_ref, ng_ref, sc_ref, sh_ref, w_ref, cs_ref, ab_ref, sg_ref, u_ref, *, e, gd):
    hb = _modulated_norm(x_ref[...], ng_ref[...], sc_ref[...], sh_ref[...])
    tn = _tile(e, 512)
    for j in range(e // tn):
        cols = slice(j * tn, (j + 1) * tn)
        u_ref[:, cols] = jnp.dot(hb, w_ref[:, cols], preferred_element_type=F32).astype(BF16)
    for grp in range(e // gd):
        cols = slice(grp * gd, (grp + 1) * gd)
        ab = jnp.dot(u_ref[:, cols], cs_ref[...], preferred_element_type=F32)
        ab_ref[0, :, cols] = ab[:, :gd].astype(BF16)
        ab_ref[1, :, cols] = ab[:, gd:].astype(BF16)
    for j in range(e // tn):
        cols = slice(j * tn, (j + 1) * tn)
        g = jnp.dot(hb, w_ref[:, e + j * tn:e + (j + 1) * tn], preferred_element_type=F32)
        sg_ref[:, cols] = _silu(g).astype(BF16)


def _fourier_in(x, ng, sc, sh, w_in, cs):
    b, n, d = x.shape
    w_shape, w_spec, w_in = _layer_weights(w_in)
    e = w_shape[1] // 2
    gd = e // N_FOURIER_GROUPS
    tm = _tile(n, 512)
    return pl.pallas_call(
        functools.partial(_fourier_in_kernel, e=e, gd=gd),
        grid=(b, n // tm),
        in_specs=[
            pl.BlockSpec((None, tm, d), lambda bi, i: (bi, i, 0)),
            _resident((1, d), lambda bi, i: (0, 0)),
            pl.BlockSpec((None, 1, d), lambda bi, i: (bi, 0, 0)),
            pl.BlockSpec((None, 1, d), lambda bi, i: (bi, 0, 0)),
            w_spec,
            _resident((gd, 2 * gd), lambda bi, i: (0, 0)),
        ],
        out_specs=[
            pl.BlockSpec((None, 2, tm, e), lambda bi, i: (bi, 0, i, 0)),
            pl.BlockSpec((None, tm, e), lambda bi, i: (bi, i, 0)),
        ],
        out_shape=[
            jax.ShapeDtypeStruct((b, 2, n, e), BF16),
            jax.ShapeDtypeStruct((b, n, e), BF16),
        ],
        scratch_shapes=[pltpu.VMEM((tm, e), BF16)],
        compiler_params=_params(2),
        name="fourier_in",
    )(x, ng, sc, sh, w_in, cs)


def _seq_dft_kernel(d_ref, ab_ref, sg_ref, out_ref, acc_ref):
    k = pl.program_id(2)
    part = jnp.dot(d_ref[...], ab_ref[...], preferred_element_type=F32)

    @pl.when(k == 0)
    def _():
        acc_ref[...] = part

    @pl.when(k > 0)
    def _():
        acc_ref[...] += part

    @pl.when(k == pl.num_programs(2) - 1)
    def _():
        out_ref[...] = (acc_ref[...] * sg_ref[...].astype(F32)).astype(BF16)


def _seq_dft(dmat, ab, sg):
    b, n2, e = ab.shape
    n = n2 // 2
    tm = _tile(n, 1024)
    tk = _tile(n2, 1024)
    return pl.pallas_call(
        _seq_dft_kernel,
        grid=(b, n // tm, n2 // tk),
        in_specs=[
            pl.BlockSpec((tm, tk), lambda bi, i, k: (i, k)),
            pl.BlockSpec((None, tk, e), lambda bi, i, k: (bi, k, 0)),
            pl.BlockSpec((None, tm, e), lambda bi, i, k: (bi, i, 0)),
        ],
        out_specs=pl.BlockSpec((None, tm, e), lambda bi, i, k: (bi, i, 0)),
        out_shape=jax.ShapeDtypeStruct((b, n, e), BF16),
        scratch_shapes=[pltpu.VMEM((tm, e), F32)],
        compiler_params=_params(3),
        name="seq_dft",
    )(dmat, ab, sg)


SEQ_RADIX = 8


def _twiddle(z, j, r):
    x, y = z
    if j == 0:
        return z
    if 4 * j == r:
        return y, -x
    if 8 * j == r:
        return (x + y) * math.sqrt(0.5), (y - x) * math.sqrt(0.5)
    if 8 * j == 3 * r:
        return (y - x) * math.sqrt(0.5), (-x - y) * math.sqrt(0.5)
    c, sn = math.cos(2.0 * math.pi * j / r), math.sin(2.0 * math.pi * j / r)
    return x * c + y * sn, y * c - x * sn


def _dft_blocks(z):
    r = len(z)
    if r == 1:
        return z
    h = r // 2
    even = _dft_blocks([(z[j][0] + z[j + h][0], z[j][1] + z[j + h][1]) for j in range(h)])
    odd = _dft_blocks([_twiddle((z[j][0] - z[j + h][0], z[j][1] - z[j + h][1]), j, r) for j in range(h)])
    out = [None] * r
    out[0::2], out[1::2] = even, odd
    return out


def _fourier_in_rdx_kernel(x_ref, ng_ref, sc_ref, sh_ref, w_ref, cs_ref, y_ref, sg_ref, u_ref, *, e, gd, radix, res):
    hb = _modulated_norm(x_ref[...], ng_ref[...], sc_ref[...], sh_ref[...])
    rows = x_ref.shape[0] // res
    q = rows // radix
    tn = _tile(e, 512)
    for j in range(e // tn):
        cols = slice(j * tn, (j + 1) * tn)
        u_ref[:, cols] = jnp.dot(hb, w_ref[:, cols], preferred_element_type=F32).astype(BF16)
    for grp in range(e // gd):
        cols = slice(grp * gd, (grp + 1) * gd)
        ab = jnp.dot(u_ref[:, cols], cs_ref[...], preferred_element_type=F32)
        for t in range(res):
            blk = [slice(t * rows + j * q, t * rows + (j + 1) * q) for j in range(radix)]
            y = _dft_blocks([(ab[b, :gd], -ab[b, gd:]) for b in blk])
            for kappa in range(radix):
                y_ref[kappa, 0, t * q:(t + 1) * q, cols] = y[kappa][0].astype(BF16)
                y_ref[kappa, 1, t * q:(t + 1) * q, cols] = (-y[kappa][1]).astype(BF16)
    for j in range(e // tn):
        cols = slice(j * tn, (j + 1) * tn)
        g = jnp.dot(hb, w_ref[:, e + j * tn:e + (j + 1) * tn], preferred_element_type=F32)
        sg_ref[:, cols] = _silu(g).astype(BF16)


def _fourier_in_rdx(x, ng, sc, sh, w_in, cs):
    b, n, d = x.shape
    w_shape, w_spec, w_in = _layer_weights(w_in)
    e = w_shape[1] // 2
    gd = e // N_FOURIER_GROUPS
    r = SEQ_RADIX
    res = 2
    tm = res * n // r
    return pl.pallas_call(
        functools.partial(_fourier_in_rdx_kernel, e=e, gd=gd, radix=r, res=res),
        grid=(b, r // res),
        in_specs=[
            pl.BlockSpec((None, tm, d), lambda bi, i: (bi, i, 0)),
            _resident((1, d), lambda bi, i: (0, 0)),
            pl.BlockSpec((None, 1, d), lambda bi, i: (bi, 0, 0)),
            pl.BlockSpec((None, 1, d), lambda bi, i: (bi, 0, 0)),
            w_spec,
            _resident((gd, 2 * gd), lambda bi, i: (0, 0)),
        ],
        out_specs=[
            pl.BlockSpec((None, r, 2, tm // r, e), lambda bi, i: (bi, 0, 0, i, 0)),
            pl.BlockSpec((None, tm, e), lambda bi, i: (bi, i, 0)),
        ],
        out_shape=[
            jax.ShapeDtypeStruct((b, r, 2, n // r, e), BF16),
            jax.ShapeDtypeStruct((b, n, e), BF16),
        ],
        scratch_shapes=[pltpu.VMEM((tm, e), BF16)],
        compiler_params=_params(2),
        name="fourier_in_rdx",
    )(x, ng, sc, sh, w_in, cs)


def _seq_dft_rdx_kernel(d_ref, y_ref, sg_ref, out_ref):
    tn = _tile(out_ref.shape[1], 512)
    for j in range(out_ref.shape[1] // tn):
        cols = slice(j * tn, (j + 1) * tn)
        f = jnp.dot(d_ref[...], y_ref[:, cols], preferred_element_type=F32)
        out_ref[:, cols] = (f * sg_ref[:, cols].astype(F32)).astype(BF16)


def _seq_dft_rdx(dmat, y, sg):
    b, r, n2, e = y.shape
    nr = n2 // 2
    return pl.pallas_call(
        _seq_dft_rdx_kernel,
        grid=(r, b),
        in_specs=[
            pl.BlockSpec((None, nr, n2), lambda kp, bi: (kp, 0, 0)),
            pl.BlockSpec((None, None, n2, e), lambda kp, bi: (bi, kp, 0, 0)),
            pl.BlockSpec((None, nr, e), lambda kp, bi: (bi, kp, 0)),
        ],
        out_specs=pl.BlockSpec((None, nr, e), lambda kp, bi: (bi, kp, 0)),
        out_shape=jax.ShapeDtypeStruct((b, r * nr, e), BF16),
        compiler_params=_params(2),
        name="seq_dft_rdx",
    )(dmat, y, sg)


def _seq_dft_out_kernel(d_ref, y_ref, sg_ref, w_ref, x_ref, gt_ref, out_ref, a_ref):
    tn = _tile(a_ref.shape[1], 512)
    for j in range(a_ref.shape[1] // tn):
        cols = slice(j * tn, (j + 1) * tn)
        f = jnp.dot(d_ref[...], y_ref[:, cols], preferred_element_type=F32)
        a_ref[:, cols] = (f * sg_ref[:, cols].astype(F32)).astype(BF16)
    out_ref[...] = x_ref[...] + gt_ref[...] * jnp.dot(a_ref[...], w_ref[...], preferred_element_type=F32)


def _seq_dft_out(dmat, y, sg, w_out, x, gate):
    b, r, n2, e = y.shape
    nr = n2 // 2
    d = x.shape[2]
    _, w_spec, w_out = _layer_weights(w_out)
    return pl.pallas_call(
        _seq_dft_out_kernel,
        grid=(r, b),
        in_specs=[
            pl.BlockSpec((None, nr, n2), lambda kp, bi: (kp, 0, 0)),
            pl.BlockSpec((None, None, n2, e), lambda kp, bi: (bi, kp, 0, 0)),
            pl.BlockSpec((None, nr, e), lambda kp, bi: (bi, kp, 0)),
            w_spec,
            pl.BlockSpec((None, nr, d), lambda kp, bi: (bi, kp, 0)),
            pl.BlockSpec((None, 1, d), lambda kp, bi: (bi, 0, 0)),
        ],
        out_specs=pl.BlockSpec((None, nr, d), lambda kp, bi: (bi, kp, 0)),
        out_shape=jax.ShapeDtypeStruct(x.shape, F32),
        scratch_shapes=[pltpu.VMEM((nr, e), BF16)],
        compiler_params=_params(2),
        name="seq_dft_out",
    )(dmat, y, sg, w_out, x, gate)


def _seq_dft_rdx_matrices(n):
    r = SEQ_RADIX
    kk = r * jnp.arange(n // r, dtype=jnp.int32)[None, :, None] + jnp.arange(r, dtype=jnp.int32)[:, None, None]
    rho = jnp.arange(r, dtype=jnp.int32)[None, None, :]
    mr = r * jnp.arange(n // (r * r), dtype=jnp.int32)[None, None, :]
    ang_a = ((kk * mr) % n).astype(F32) * (2.0 * math.pi / n)
    ang_b = ((kk * rho) % n).astype(F32) * (2.0 * math.pi / n)
    ca, sa = jnp.cos(ang_a)[:, :, None, :], jnp.sin(ang_a)[:, :, None, :]
    cb, sb = jnp.cos(ang_b)[:, :, :, None], jnp.sin(ang_b)[:, :, :, None]
    c = (ca * cb - sa * sb).reshape(r, n // r, n // r)
    s = (sa * cb + ca * sb).reshape(r, n // r, n // r)
    return (jnp.concatenate([c, -s], axis=2) * (n ** -0.5)).astype(BF16)


def _to_residue_major(x, axis):
    n, r = x.shape[axis], SEQ_RADIX
    shape = x.shape[:axis] + (n // r, r) + x.shape[axis + 1:]
    return jnp.swapaxes(x.reshape(shape), axis, axis + 1).reshape(x.shape)


def _from_residue_major(x, axis):
    n, r = x.shape[axis], SEQ_RADIX
    shape = x.shape[:axis] + (r, n // r) + x.shape[axis + 1:]
    return jnp.swapaxes(x.reshape(shape), axis, axis + 1).reshape(x.shape)


def _dft_cos_sin(n):
    k = jnp.arange(n, dtype=jnp.int32)
    if n >= 1024 and n % 64 == 0:
        m1 = jnp.arange(n // 64, dtype=jnp.int32) * 64
        m2 = jnp.arange(64, dtype=jnp.int32)
        ang_a = ((k[:, None] * m1[None, :]) % n).astype(F32) * (2.0 * math.pi / n)
        ang_b = ((k[:, None] * m2[None, :]) % n).astype(F32) * (2.0 * math.pi / n)
        ca, sa = jnp.cos(ang_a)[:, :, None], jnp.sin(ang_a)[:, :, None]
        cb, sb = jnp.cos(ang_b)[:, None, :], jnp.sin(ang_b)[:, None, :]
        return (ca * cb - sa * sb).reshape(n, n), (sa * cb + ca * sb).reshape(n, n)
    ang = ((k[:, None] * k[None, :]) % n).astype(F32) * (2.0 * math.pi / n)
    return jnp.cos(ang), jnp.sin(ang)


def _seq_dft_matrix(n):
    c, s = _dft_cos_sin(n)
    return (jnp.concatenate([c, -s], axis=1) * (n ** -0.5)).astype(BF16)


def _chan_dft_matrix(gd):
    c, s = _dft_cos_sin(gd)
    return (jnp.concatenate([c, s], axis=1) * (gd ** -0.5)).astype(BF16)


def _rope(x, cos, sin_signed, first_half):
    fwd = pltpu.roll(x, LANES - 32, axis=1)
    bwd = pltpu.roll(x, 32, axis=1)
    return x * cos + jnp.where(first_half, fwd, bwd) * sin_signed


def _attn_in_kernel(*refs, e, hd, rope, q_scale):
    if rope:
        (x_ref, ng_ref, sc_ref, sh_ref, w_ref, wvt_ref, cq_ref, sq_ref, ck_ref, sk_ref,
         q_ref, k_ref, v_ref, sg_ref) = refs
        lane = lax.broadcasted_iota(jnp.int32, (1, LANES), 1)
        first_half = (lane % 64) < 32
    else:
        x_ref, ng_ref, sc_ref, sh_ref, w_ref, q_ref, k_ref, v_ref, sg_ref = refs
    hb = _modulated_norm(x_ref[...], ng_ref[...], sc_ref[...], sh_ref[...])
    hw = 2 * hd
    for j in range(e // hw):
        q = jnp.dot(hb, w_ref[:, j * hw:(j + 1) * hw], preferred_element_type=F32)
        k = jnp.dot(hb, w_ref[:, e + j * hw:e + (j + 1) * hw], preferred_element_type=F32)
        for mp in range(2):
            cols = slice(j * hw + mp * hd, j * hw + (mp + 1) * hd)
            qm, km = q[:, mp * hd:(mp + 1) * hd], k[:, mp * hd:(mp + 1) * hd]
            if rope:
                qm = _rope(qm, cq_ref[...], sq_ref[...], first_half)
                km = _rope(km, ck_ref[...], sk_ref[...], first_half)
            else:
                qm = qm * q_scale
            q_ref[:, cols] = qm.astype(q_ref.dtype)
            k_ref[:, cols] = km.astype(k_ref.dtype)
    tn = _tile(e, 512)
    for j in range(e // tn):
        cols = slice(j * tn, (j + 1) * tn)
        if rope:
            vt = lax.dot_general(wvt_ref[cols, :], hb, (((1,), (1,)), ((), ())), preferred_element_type=F32)
            v_ref[cols, :] = vt.astype(v_ref.dtype)
        else:
            v = jnp.dot(hb, w_ref[:, 2 * e + j * tn:2 * e + (j + 1) * tn], preferred_element_type=F32)
            v_ref[:, cols] = v.astype(v_ref.dtype)
        g = jnp.dot(hb, w_ref[:, 3 * e + j * tn:3 * e + (j + 1) * tn], preferred_element_type=F32)
        sg_ref[:, cols] = _silu(g).astype(BF16)


def _attn_in(x, ng, sc, sh, w_in, q_scale, rope_tables, kv_dtype):
    b, n, d = x.shape
    layer = w_in[1]
    w_shape, w_spec, w_in = _layer_weights(w_in)
    e = w_shape[1] // 4
    hd = e // (2 * N_HEADS)
    assert hd == LANES
    tm = _tile(n, 512)
    rope = rope_tables is not None
    in_specs = [
        pl.BlockSpec((None, tm, d), lambda bi, i: (bi, i, 0)),
        _resident((1, d), lambda bi, i: (0, 0)),
        pl.BlockSpec((None, 1, d), lambda bi, i: (bi, 0, 0)),
        pl.BlockSpec((None, 1, d), lambda bi, i: (bi, 0, 0)),
        w_spec,
    ]
    args = [x, ng, sc, sh, w_in]
    tok = pl.BlockSpec((None, tm, e), lambda bi, i: (bi, i, 0))
    v_spec, v_shape = tok, jax.ShapeDtypeStruct((b, n, e), kv_dtype)
    if rope:
        in_specs += [_resident((e, d), lambda bi, i: (0, 0))]
        in_specs += [pl.BlockSpec((tm, hd), lambda bi, i: (i, 0))] * 4
        args += [w_in[layer, :, 2 * e:3 * e].T] + list(rope_tables)
        v_spec = pl.BlockSpec((None, None, e, tm), lambda bi, i: (bi, i, 0, 0))
        v_shape = jax.ShapeDtypeStruct((b, n // tm, e, tm), kv_dtype)
    return pl.pallas_call(
        functools.partial(_attn_in_kernel, e=e, hd=hd, rope=rope, q_scale=q_scale),
        grid=(b, n // tm),
        in_specs=in_specs,
        out_specs=[tok, tok, v_spec, tok],
        out_shape=[
            jax.ShapeDtypeStruct((b, n, e), BF16),
            jax.ShapeDtypeStruct((b, n, e), kv_dtype),
            v_shape,
            jax.ShapeDtypeStruct((b, n, e), BF16),
        ],
        compiler_params=_params(2),
        name="attn_in_rope" if rope else "attn_in",
    )(*args)


def _lane_blocks(x):
    return [x[:, j * LANES:(j + 1) * LANES] for j in range(x.shape[1] // LANES)]


def _fold_rows(op, x):
    return op(x.reshape(x.shape[0] // SUBLANES, SUBLANES, x.shape[1]), axis=0)


def _diff_attn_kernel(*refs, n_new, n_past, ck, hd, lam_init, n_tiles, heads, vt_in):
    if n_past:
        (q_ref, qn_ref, k_ref, v_ref, pk_ref, pv_ref, sg_ref, lam_ref, sub_ref, out_ref,
         s_ref, p_ref, mcur_ref, vt_ref, r_ref, il_ref, o_ref) = refs
    else:
        (q_ref, qn_ref, k_ref, v_ref, sg_ref, lam_ref, sub_ref, out_ref,
         s_ref, p_ref, mcur_ref, vt_ref, r_ref, il_ref, o_ref) = refs
    n_new_chunks = n_new // ck
    tq = q_ref.shape[0]
    nt = (((1,), (1,)), ((), ()))
    i = pl.program_id(2)
    first = i == 0
    last = i == n_tiles - 1
    pipelined = n_tiles > 1
    cold = first if not pipelined else jnp.logical_and(
        first, jnp.logical_and(pl.program_id(0) == 0, pl.program_id(1) == 0))
    hw = 2 * hd

    def run_head(col0):
        hcols = slice(col0, col0 + hw)

        def rows_of(c):
            if isinstance(c, int):
                return slice(c * ck, (c + 1) * ck)
            return pl.ds(pl.multiple_of(c * ck, ck), ck)

        def new_chunk(c):
            vt_src = v_ref if vt_in else vt_ref
            return (lambda mp: (mp, c)), (lambda: k_ref[rows_of(c), hcols].astype(BF16)), (lambda: vt_src[c])

        past_chunk = ((lambda mp: (mp, n_new_chunks, slice(0, n_past), slice(None))),
                      (lambda: pk_ref[:, hcols].astype(BF16)),
                      (lambda: vt_ref[n_new_chunks, :, 0:n_past]))

        def scores_chunk(src_ref, chunk, mrun):
            at, load_keys, _ = chunk
            kc = load_keys()
            out = []
            for mp in range(2):
                cols = slice(mp * hd, (mp + 1) * hd)
                qcols = slice(col0 + mp * hd, col0 + (mp + 1) * hd)
                s = lax.dot_general(kc[:, cols], src_ref[:, qcols], nt, preferred_element_type=F32)
                s_ref[at(mp)] = s
                out.append(jnp.maximum(mrun[mp], _fold_rows(jnp.max, s)))
            return tuple(out)

        def probs_chunk(chunk, lsum):
            at = chunk[0]
            out = []
            for mp in range(2):
                s = s_ref[at(mp)]
                s3 = s.reshape(s.shape[0] // SUBLANES, SUBLANES, tq)
                p3 = jnp.exp2(s3 - mcur_ref[mp][None])
                p_ref[at(mp)] = p3.reshape(s.shape).astype(BF16)
                out.append(lsum[mp] + jnp.sum(p3, axis=0))
            return tuple(out)

        def values_chunk(chunk):
            at, _, load_vt = chunk
            a = p_ref[at(0)] - r_ref[0:1, :].astype(BF16) * p_ref[at(1)]
            return jnp.dot(load_vt(), a, preferred_element_type=F32)

        def finish_stats(carry):
            l1 = jnp.sum(carry[0], axis=0, keepdims=True)
            l2 = jnp.sum(carry[1], axis=0, keepdims=True)
            lam_p = lam_ref[...]
            lam = (jnp.exp(jnp.sum(lam_p[0:1] * lam_p[1:2], axis=-1, keepdims=True))
                   - jnp.exp(jnp.sum(lam_p[2:3] * lam_p[3:4], axis=-1, keepdims=True)) + lam_init)
            r_ref[...] = jnp.broadcast_to(lam * l1 / l2, (SUBLANES, tq))
            il_ref[...] = jnp.broadcast_to(1.0 / l1, (SUBLANES, tq))
            if pipelined:
                for mp in range(2):
                    m = jnp.max(carry[2 + mp], axis=0, keepdims=True)
                    mcur_ref[mp] = jnp.broadcast_to(m, (SUBLANES, tq))

        def write_tile(o, tile):
            rows = pl.ds(pl.multiple_of(tile * tq, tq), tq)
            o = o * il_ref[0:1, :]
            o = o * lax.rsqrt(jnp.mean(o * o, axis=0, keepdims=True) + EPS)
            o = jnp.concatenate([blk * sub_ref[...] for blk in _lane_blocks(o)], axis=1)
            out_ref[rows, hcols] = ((o.T * (1.0 - lam_init)) * sg_ref[rows, hcols].astype(F32)).astype(BF16)

        neg = jnp.full((SUBLANES, tq), -jnp.inf, F32)
        zero = jnp.zeros((SUBLANES, tq), F32)

        def stage_loop(with_values):
            def body(chunks, carry):
                if with_values:
                    o_ref[...] += functools.reduce(jnp.add, [values_chunk(ch) for ch in chunks])
                for ch in chunks:
                    lsum = probs_chunk(ch, carry[:2])
                    mrun = scores_chunk(qn_ref, ch, carry[2:]) if pipelined else carry[2:]
                    carry = lsum + mrun
                return carry
            chunks = [new_chunk(c) for c in range(n_new_chunks)] + ([past_chunk] if n_past else [])
            return body(chunks, (zero, zero, neg, neg))

        @pl.when(cold)
        def _():
            mrun = lax.fori_loop(0, n_new_chunks, lambda c, m: scores_chunk(q_ref, new_chunk(c), m), (neg, neg))
            if n_past:
                mrun = scores_chunk(q_ref, past_chunk, mrun)
            for mp in range(2):
                mcur_ref[mp] = jnp.broadcast_to(jnp.max(mrun[mp], axis=0, keepdims=True), (SUBLANES, tq))

        @pl.when(first)
        def _():
            if not vt_in:
                for c in range(n_new_chunks):
                    vt_ref[c] = v_ref[rows_of(c), hcols].astype(F32).T.astype(BF16)
            if n_past:
                vt_ref[n_new_chunks, :, 0:n_past] = pv_ref[:, hcols].astype(F32).T.astype(BF16)
            finish_stats(stage_loop(with_values=False))

        if pipelined:
            @pl.when(jnp.logical_not(first))
            def _():
                o_ref[...] = jnp.zeros(o_ref.shape, F32)
                carry = stage_loop(with_values=True)
                write_tile(o_ref[...], i - 1)
                finish_stats(carry)

        @pl.when(last)
        def _():
            o = None
            for c in range(n_new_chunks):
                d = values_chunk(new_chunk(c))
                o = d if o is None else o + d
            if n_past:
                o = o + values_chunk(past_chunk)
            write_tile(o, i)

    for hh in range(heads):
        run_head(hh * hw)


def _diff_attn(q, k, v, sg, lam_p, subln, lam_init, past=None):
    b, n, e = q.shape
    hw = e // N_HEADS
    hd = hw // 2
    tq = _tile(n, 512)
    ck = _tile(n, 512)
    n_tiles = n // tq
    vt_in = v.ndim == 4
    assert not vt_in or v.shape[3] == ck
    n_past = 0 if past is None else past[0].shape[2]
    assert n_past <= ck and n_past % LANES == 0
    n_chunks = n // ck + (1 if n_past else 0)
    heads = N_HEADS if n_tiles == 1 else 1
    bw = heads * hw
    tok = pl.BlockSpec((None, tq, bw), lambda bi, h, i: (bi, i, h))
    head = pl.BlockSpec((None, n, bw), lambda bi, h, i: (bi, 0, h))

    def ahead(bi, h, i):
        if n_tiles == 1:
            return bi, h, i
        wrap = i == n_tiles - 1
        g = jnp.minimum(bi * N_HEADS + h + 1, b * N_HEADS - 1)
        return (jnp.where(wrap, g // N_HEADS, bi), jnp.where(wrap, g % N_HEADS, h),
                jnp.where(wrap, 0, i + 1))

    def tok_ahead(bi, h, i):
        nb, nh, ni = ahead(bi, h, i)
        return nb, ni, nh

    def head_ahead(bi, h, i):
        nb, nh, _ = ahead(bi, h, i)
        return nb, 0, nh

    v_spec = pl.BlockSpec((None, n // ck, bw, ck), lambda bi, h, i: (bi, 0, h, 0)) if vt_in else head
    in_specs = [tok, pl.BlockSpec((None, tq, bw), tok_ahead), pl.BlockSpec((None, n, bw), head_ahead), v_spec]
    args = [q, q, k, v]
    scratch = [
        pltpu.VMEM((2, n_chunks, ck, tq), F32),
        pltpu.VMEM((2, n_chunks, ck, tq), BF16),
        pltpu.VMEM((2, SUBLANES, tq), F32),
        pltpu.VMEM((n_chunks, hw, ck), BF16),
        pltpu.VMEM((SUBLANES, tq), F32),
        pltpu.VMEM((SUBLANES, tq), F32),
        pltpu.VMEM((hw, tq), F32),
    ]
    if n_past:
        cache_k, cache_v, layer = past
        cached = pl.BlockSpec((None, None, n_past, bw), lambda bi, h, i: (bi, layer, 0, h))

        def cached_ahead(bi, h, i):
            nb, nh, _ = ahead(bi, h, i)
            return nb, layer, 0, nh

        in_specs += [pl.BlockSpec((None, None, n_past, bw), cached_ahead), cached]
        args += [cache_k, cache_v]
    in_specs += [head,
                 _resident((4, hd), lambda bi, h, i: (0, 0)),
                 _resident((hw, LANES), lambda bi, h, i: (0, 0))]
    args += [sg, lam_p, jnp.broadcast_to(subln.reshape(hw, 1), (hw, LANES))]
    return pl.pallas_call(
        functools.partial(_diff_attn_kernel, n_new=n, n_past=n_past, ck=ck, hd=hd, lam_init=lam_init,
                          n_tiles=n_tiles, heads=heads, vt_in=vt_in),
        grid=(b, N_HEADS // heads, n_tiles),
        in_specs=in_specs,
        out_specs=head,
        out_shape=jax.ShapeDtypeStruct((b, n, e), BF16),
        scratch_shapes=scratch,
        compiler_params=_params(3),
        name="diff_attn_past" if n_past else "diff_attn",
    )(*args)


def _rope_tables(n, hd, q_scale):
    axis_dim = hd // 2
    rows = n // GRID_W
    row = jnp.broadcast_to(jnp.arange(rows, dtype=F32)[:, None], (rows, GRID_W)).reshape(-1)
    colp = jnp.broadcast_to(jnp.arange(GRID_W, dtype=F32)[None, :], (rows, GRID_W)).reshape(-1)
    inv = ROPE_BASE ** (-jnp.arange(0, axis_dim, 2, dtype=F32) / axis_dim)
    ar = row[:, None] * inv[None, :]
    ac = colp[:, None] * inv[None, :]
    ang = jnp.concatenate([ar, ar, ac, ac], axis=-1)
    cos, sin = jnp.cos(ang), jnp.sin(ang)
    half = axis_dim // 2
    sign = jnp.where((jnp.arange(hd) % axis_dim) < half, -1.0, 1.0).astype(F32)
    sin_signed = sin * sign[None, :]
    return cos * q_scale, sin_signed * q_scale, cos, sin_signed


def _out_proj_kernel(*refs, final):
    if final:
        a_ref, w_ref, x_ref, gt_ref, fg_ref, y_ref = refs
    else:
        a_ref, w_ref, x_ref, gt_ref, y_ref = refs
    y = x_ref[...] + gt_ref[...] * jnp.dot(a_ref[...], w_ref[...], preferred_element_type=F32)
    if final:
        y = (y * lax.rsqrt(jnp.mean(y * y, axis=-1, keepdims=True) + EPS)) * fg_ref[...]
    y_ref[...] = y


def _out_proj(a, w_out, x, gate, final_g=None):
    b, n, d = x.shape
    e = a.shape[2]
    tm = _tile(n, 512)
    final = final_g is not None
    _, w_spec, w_out = _layer_weights(w_out)
    in_specs = [
        pl.BlockSpec((None, tm, e), lambda bi, i: (bi, i, 0)),
        w_spec,
        pl.BlockSpec((None, tm, d), lambda bi, i: (bi, i, 0)),
        pl.BlockSpec((None, 1, d), lambda bi, i: (bi, 0, 0)),
    ]
    args = [a, w_out, x, gate]
    if final:
        in_specs.append(_resident((1, d), lambda bi, i: (0, 0)))
        args.append(final_g)
    return pl.pallas_call(
        functools.partial(_out_proj_kernel, final=final),
        grid=(b, n // tm),
        in_specs=in_specs,
        out_specs=pl.BlockSpec((None, tm, d), lambda bi, i: (bi, i, 0)),
        out_shape=jax.ShapeDtypeStruct((b, n, d), F32),
        compiler_params=_params(2),
        name="out_proj_final" if final else "out_proj",
    )(*args)


def kernel(x_prompt, x_sample, cache_k, cache_v, c, c_ctx, norm_g, w_ada, b_ada, w_in_fourier, w_out_fourier, w_in_attn, w_out_attn, lam_q1, lam_k1, lam_q2, lam_k2, subln_g, final_norm_g):
    depth, d = norm_g.shape
    bp, n_ctx, _ = x_prompt.shape
    bs, n_lat, _ = x_sample.shape
    past_len = cache_k.shape[2]
    e = w_out_attn.shape[1]
    hd = e // (2 * N_HEADS)
    gd = e // N_FOURIER_GROUPS

    n_cond = 1 + bs
    rows = -(-n_cond // 8) * 8
    cond = jnp.concatenate([c_ctx[None, :], c, jnp.zeros((rows - n_cond, d), F32)], axis=0)
    mods = _adaln(cond, w_ada, b_ada).reshape(depth, rows, 3, d)

    q_scale = hd ** -0.5 * math.log2(math.e)
    tables = _rope_tables(n_lat, hd, q_scale)
    cs = _chan_dft_matrix(gd)
    dmat_p = _seq_dft_matrix(n_ctx)
    cache_k4 = cache_k.astype(BF16).reshape(bs, cache_k.shape[1], past_len, e)
    cache_v4 = cache_v.astype(BF16).reshape(bs, cache_v.shape[1], past_len, e)

    w_in_fourier, w_out_fourier = _to_bf16(w_in_fourier), _to_bf16(w_out_fourier)
    w_in_attn, w_out_attn = _to_bf16(w_in_attn), _to_bf16(w_out_attn)

    xp, xs = x_prompt, x_sample
    split_dft = n_lat % (16 * SEQ_RADIX ** 2) == 0
    if split_dft:
        xs = _to_residue_major(xs, 1)
        tables = tuple(_to_residue_major(t, 0) for t in tables)
        dmat_s = _seq_dft_rdx_matrices(n_lat)
    else:
        dmat_s = _seq_dft_matrix(n_lat)
    new_k, new_v = [], []
    for i in range(depth):
        ng = norm_g[i][None, :]
        sh_p, sc_p, gt_p = (jnp.broadcast_to(mods[i, 0:1, t][:, None, :], (bp, 1, d)) for t in range(3))
        sh_s, sc_s, gt_s = (mods[i, 1:1 + bs, t][:, None, :] for t in range(3))
        j = i // N_MIXERS
        last = i == depth - 1
        fg = final_norm_g[None, :] if last else None
        if i % N_MIXERS == 0:
            w_in = (w_in_fourier, j)
            w_out = (w_out_fourier, j)
            ab_p, sg_p = _fourier_in(xp, ng, sc_p, sh_p, w_in, cs)
            a_p = _seq_dft(dmat_p, ab_p.reshape(bp, 2 * n_ctx, e), sg_p)
            if split_dft:
                y_s, sg_s = _fourier_in_rdx(xs, ng, sc_s, sh_s, w_in, cs)
                y_s = y_s.reshape(bs, SEQ_RADIX, 2 * n_lat // SEQ_RADIX, e)
                a_s = None if fg is None else _seq_dft_rdx(dmat_s, y_s, sg_s)
            else:
                ab_s, sg_s = _fourier_in(xs, ng, sc_s, sh_s, w_in, cs)
                a_s = _seq_dft(dmat_s, ab_s.reshape(bs, 2 * n_lat, e), sg_s)
        else:
            lam_init = 0.8 - 0.6 * math.exp(-0.3 * i)
            w_in = (w_in_attn, j)
            w_out = (w_out_attn, j)
            lam_p = jnp.stack([lam_q1[j], lam_k1[j], lam_q2[j], lam_k2[j]], axis=0)
            sub = subln_g[j][None, :]
            q_p, k_p, v_p, sg_p = _attn_in(xp, ng, sc_p, sh_p, w_in, q_scale, None, F32)
            new_k.append(k_p.reshape(bp, n_ctx, N_HEADS, 2 * hd))
            new_v.append(v_p.reshape(bp, n_ctx, N_HEADS, 2 * hd))
            a_p = _diff_attn(q_p, k_p, v_p, sg_p, lam_p, sub, lam_init)
            q_s, k_s, v_s, sg_s = _attn_in(xs, ng, sc_s, sh_s, w_in, q_scale, tables, BF16)
            a_s = _diff_attn(q_s, k_s, v_s, sg_s, lam_p, sub, lam_init,
                             past=(cache_k4, cache_v4, j))
        xp = _out_proj(a_p, w_out, xp, gt_p, fg)
        if a_s is None:
            xs = _seq_dft_out(dmat_s, y_s, sg_s, w_out, xs, gt_s)
        else:
            xs = _out_proj(a_s, w_out, xs, gt_s, fg)
    if split_dft:
        xs = _from_residue_major(xs, 1)
    return (xp, xs, jnp.stack(new_k, axis=1), jnp.stack(new_v, axis=1))
```

```python
import functools
import math

import jax
import jax.numpy as jnp
from jax import lax
from jax.experimental import pallas as pl
from jax.experimental.pallas import tpu as pltpu

N_HEADS = 8
N_FOURIER_GROUPS = 8
GRID_W = 64
ROPE_BASE = 10000.0
EPS = 1e-6
N_MIXERS = 2

V7X_VMEM_BYTES = 64 * 1024 * 1024
VMEM_LIMIT_BYTES = V7X_VMEM_BYTES - 8 * 1024 * 1024
LANES = 128
SUBLANES = 8

BF16 = jnp.bfloat16
F32 = jnp.float32


def _tile(n, pref):
    t = min(n, pref)
    assert n % t == 0, (n, t)
    return t


def _params(n_axes):
    return pltpu.CompilerParams(
        dimension_semantics=("arbitrary",) * n_axes,
        vmem_limit_bytes=VMEM_LIMIT_BYTES,
    )


def _resident(block_shape, index_map):
    return pl.BlockSpec(block_shape, index_map, pipeline_mode=pl.Buffered(1))


def _layer_weights(w):
    stack, layer = w
    return stack.shape[1:], _resident((None,) + stack.shape[1:], lambda *_: (layer, 0, 0)), stack


def _to_bf16_kernel(x_ref, o_ref):
    o_ref[...] = x_ref[...].astype(BF16)


def _to_bf16(w):
    l, r, c = w.shape
    tr = _tile(r, max(8, (4 * 1024 * 1024) // (4 * c)))
    return pl.pallas_call(
        _to_bf16_kernel,
        grid=(l, r // tr),
        in_specs=[pl.BlockSpec((None, tr, c), lambda li, i: (li, i, 0))],
        out_specs=pl.BlockSpec((None, tr, c), lambda li, i: (li, i, 0)),
        out_shape=jax.ShapeDtypeStruct(w.shape, BF16),
        compiler_params=_params(2),
        name="to_bf16",
    )(w)


def _adaln_kernel(cond_ref, w_ref, b_ref, out_ref):
    cond = cond_ref[...]
    act = cond * jax.nn.sigmoid(cond)
    out_ref[...] = jnp.dot(act, w_ref[...], preferred_element_type=F32,
                           precision=lax.Precision.HIGHEST) + b_ref[...]


def _adaln(cond, w_ada, b_ada):
    depth, d, d3 = w_ada.shape
    r = cond.shape[0]
    tn = _tile(d3, 1024)
    return pl.pallas_call(
        _adaln_kernel,
        grid=(depth, d3 // tn),
        in_specs=[
            pl.BlockSpec((r, d), lambda i, j: (0, 0)),
            pl.BlockSpec((None, d, tn), lambda i, j: (i, 0, j)),
            pl.BlockSpec((None, 1, tn), lambda i, j: (i, 0, j)),
        ],
        out_specs=pl.BlockSpec((None, r, tn), lambda i, j: (i, 0, j)),
        out_shape=jax.ShapeDtypeStruct((depth, r, d3), F32),
        compiler_params=_params(2),
        name="adaln",
    )(cond, w_ada, b_ada.reshape(depth, 1, d3))


def _modulated_norm(x, ng, sc, sh):
    ms = jnp.mean(x * x, axis=-1, keepdims=True)
    h = (x * lax.rsqrt(ms + EPS)) * (ng * (1.0 + sc)) + sh
    return h.astype(BF16)


def _silu(x):
    return x * jax.nn.sigmoid(x)


def _fourier_in_kernel(x_ref, ng_ref, sc_ref, sh_ref, w_ref, cs_ref, ab_ref, sg_ref, u_ref, *, e, gd):
    hb = _modulated_norm(x_ref[...], ng_ref[...], sc_ref[...], sh_ref[...])
    tn = _tile(e, 512)
    for j in range(e // tn):
        cols = slice(j * tn, (j + 1) * tn)
        u_ref[:, cols] = jnp.dot(hb, w_ref[:, cols], preferred_element_type=F32).astype(BF16)
    for grp in range(e // gd):
        cols = slice(grp * gd, (grp + 1) * gd)
        ab = jnp.dot(u_ref[:, cols], cs_ref[...], preferred_element_type=F32)
        ab_ref[0, :, cols] = ab[:, :gd].astype(BF16)
        ab_ref[1, :, cols] = ab[:, gd:].astype(BF16)
    for j in range(e // tn):
        cols = slice(j * tn, (j + 1) * tn)
        g = jnp.dot(hb, w_ref[:, e + j * tn:e + (j + 1) * tn], preferred_element_type=F32)
        sg_ref[:, cols] = _silu(g).astype(BF16)


def _fourier_in(x, ng, sc, sh, w_in, cs):
    b, n, d = x.shape
    w_shape, w_spec, w_in = _layer_weights(w_in)
    e = w_shape[1] // 2
    gd = e // N_FOURIER_GROUPS
    tm = _tile(n, 512)
    return pl.pallas_call(
        functools.partial(_fourier_in_kernel, e=e, gd=gd),
        grid=(b, n // tm),
        in_specs=[
            pl.BlockSpec((None, tm, d), lambda bi, i: (bi, i, 0)),
            _resident((1, d), lambda bi, i: (0, 0)),
            pl.BlockSpec((None, 1, d), lambda bi, i: (bi, 0, 0)),
            pl.BlockSpec((None, 1, d), lambda bi, i: (bi, 0, 0)),
            w_spec,
            _resident((gd, 2 * gd), lambda bi, i: (0, 0)),
        ],
        out_specs=[
            pl.BlockSpec((None, 2, tm, e), lambda bi, i: (bi, 0, i, 0)),
            pl.BlockSpec((None, tm, e), lambda bi, i: (bi, i, 0)),
        ],
        out_shape=[
            jax.ShapeDtypeStruct((b, 2, n, e), BF16),
            jax.ShapeDtypeStruct((b, n, e), BF16),
        ],
        scratch_shapes=[pltpu.VMEM((tm, e), BF16)],
        compiler_params=_params(2),
        name="fourier_in",
    )(x, ng, sc, sh, w_in, cs)


def _seq_dft_kernel(d_ref, ab_ref, sg_ref, out_ref, acc_ref):
    k = pl.program_id(2)
    part = jnp.dot(d_ref[...], ab_ref[...], preferred_element_type=F32)

    @pl.when(k == 0)
    def _():
        acc_ref[...] = part

    @pl.when(k > 0)
    def _():
        acc_ref[...] += part

    @pl.when(k == pl.num_programs(2) - 1)
    def _():
        out_ref[...] = (acc_ref[...] * sg_ref[...].astype(F32)).astype(BF16)


def _seq_dft(dmat, ab, sg):
    b, n2, e = ab.shape
    n = n2 // 2
    tm = _tile(n, 1024)
    tk = _tile(n2, 1024)
    return pl.pallas_call(
        _seq_dft_kernel,
        grid=(b, n // tm, n2 // tk),
        in_specs=[
            pl.BlockSpec((tm, tk), lambda bi, i, k: (i, k)),
            pl.BlockSpec((None, tk, e), lambda bi, i, k: (bi, k, 0)),
            pl.BlockSpec((None, tm, e), lambda bi, i, k: (bi, i, 0)),
        ],
        out_specs=pl.BlockSpec((None, tm, e), lambda bi, i, k: (bi, i, 0)),
        out_shape=jax.ShapeDtypeStruct((b, n, e), BF16),
        scratch_shapes=[pltpu.VMEM((tm, e), F32)],
        compiler_params=_params(3),
        name="seq_dft",
    )(dmat, ab, sg)


SEQ_RADIX = 8


def _twiddle(z, j, r):
    x, y = z
    if j == 0:
        return z
    if 4 * j == r:
        return y, -x
    if 8 * j == r:
        return (x + y) * math.sqrt(0.5), (y - x) * math.sqrt(0.5)
    if 8 * j == 3 * r:
        return (y - x) * math.sqrt(0.5), (-x - y) * math.sqrt(0.5)
    c, sn = math.cos(2.0 * math.pi * j / r), math.sin(2.0 * math.pi * j / r)
    return x * c + y * sn, y * c - x * sn


def _dft_blocks(z):
    r = len(z)
    if r == 1:
        return z
    h = r // 2
    even = _dft_blocks([(z[j][0] + z[j + h][0], z[j][1] + z[j + h][1]) for j in range(h)])
    odd = _dft_blocks([_twiddle((z[j][0] - z[j + h][0], z[j][1] - z[j + h][1]), j, r) for j in range(h)])
    out = [None] * r
    out[0::2], out[1::2] = even, odd
    return out


def _fourier_in_rdx_kernel(x_ref, ng_ref, sc_ref, sh_ref, w_ref, cs_ref, y_ref, sg_ref, u_ref, *, e, gd, radix, res):
    hb = _modulated_norm(x_ref[...], ng_ref[...], sc_ref[...], sh_ref[...])
    rows = x_ref.shape[0] // res
    q = rows // radix
    tn = _tile(e, 512)
    for j in range(e // tn):
        cols = slice(j * tn, (j + 1) * tn)
        u_ref[:, cols] = jnp.dot(hb, w_ref[:, cols], preferred_element_type=F32).astype(BF16)
    for grp in range(e // gd):
        cols = slice(grp * gd, (grp + 1) * gd)
        ab = jnp.dot(u_ref[:, cols], cs_ref[...], preferred_element_type=F32)
        for t in range(res):
            blk = [slice(t * rows + j * q, t * rows + (j + 1) * q) for j in range(radix)]
            y = _dft_blocks([(ab[b, :gd], -ab[b, gd:]) for b in blk])
            for kappa in range(radix):
                y_ref[kappa, 0, t * q:(t + 1) * q, cols] = y[kappa][0].astype(BF16)
                y_ref[kappa, 1, t * q:(t + 1) * q, cols] = (-y[kappa][1]).astype(BF16)
    for j in range(e // tn):
        cols = slice(j * tn, (j + 1) * tn)
        g = jnp.dot(hb, w_ref[:, e + j * tn:e + (j + 1) * tn], preferred_element_type=F32)
        sg_ref[:, cols] = _silu(g).astype(BF16)


def _fourier_in_rdx(x, ng, sc, sh, w_in, cs):
    b, n, d = x.shape
    w_shape, w_spec, w_in = _layer_weights(w_in)
    e = w_shape[1] // 2
    gd = e // N_FOURIER_GROUPS
    r = SEQ_RADIX
    res = 2
    tm = res * n // r
    return pl.pallas_call(
        functools.partial(_fourier_in_rdx_kernel, e=e, gd=gd, radix=r, res=res),
        grid=(b, r // res),
        in_specs=[
            pl.BlockSpec((None, tm, d), lambda bi, i: (bi, i, 0)),
            _resident((1, d), lambda bi, i: (0, 0)),
            pl.BlockSpec((None, 1, d), lambda bi, i: (bi, 0, 0)),
            pl.BlockSpec((None, 1, d), lambda bi, i: (bi, 0, 0)),
            w_spec,
            _resident((gd, 2 * gd), lambda bi, i: (0, 0)),
        ],
        out_specs=[
            pl.BlockSpec((None, r, 2, tm // r, e), lambda bi, i: (bi, 0, 0, i, 0)),
            pl.BlockSpec((None, tm, e), lambda bi, i: (bi, i, 0)),
        ],
        out_shape=[
            jax.ShapeDtypeStruct((b, r, 2, n // r, e), BF16),
            jax.ShapeDtypeStruct((b, n, e), BF16),
        ],
        scratch_shapes=[pltpu.VMEM((tm, e), BF16)],
        compiler_params=_params(2),
        name="fourier_in_rdx",
    )(x, ng, sc, sh, w_in, cs)


def _seq_dft_rdx_kernel(d_ref, y_ref, sg_ref, out_ref):
    tn = _tile(out_ref.shape[1], 512)
    for j in range(out_ref.shape[1] // tn):
        cols = slice(j * tn, (j + 1) * tn)
        f = jnp.dot(d_ref[...], y_ref[:, cols], preferred_element_type=F32)
        out_ref[:, cols] = (f * sg_ref[:, cols].astype(F32)).astype(BF16)


def _seq_dft_rdx(dmat, y, sg):
    b, r, n2, e = y.shape
    nr = n2 // 2
    return pl.pallas_call(
        _seq_dft_rdx_kernel,
        grid=(r, b),
        in_specs=[
            pl.BlockSpec((None, nr, n2), lambda kp, bi: (kp, 0, 0)),
            pl.BlockSpec((None, None, n2, e), lambda kp, bi: (bi, kp, 0, 0)),
            pl.BlockSpec((None, nr, e), lambda kp, bi: (bi, kp, 0)),
        ],
        out_specs=pl.BlockSpec((None, nr, e), lambda kp, bi: (bi, kp, 0)),
        out_shape=jax.ShapeDtypeStruct((b, r * nr, e), BF16),
        compiler_params=_params(2),
        name="seq_dft_rdx",
    )(dmat, y, sg)


def _seq_dft_out_kernel(d_ref, y_ref, sg_ref, w_ref, x_ref, gt_ref, out_ref, a_ref):
    tn = _tile(a_ref.shape[1], 512)
    for j in range(a_ref.shape[1] // tn):
        cols = slice(j * tn, (j + 1) * tn)
        f = jnp.dot(d_ref[...], y_ref[:, cols], preferred_element_type=F32)
        a_ref[:, cols] = (f * sg_ref[:, cols].astype(F32)).astype(BF16)
    out_ref[...] = x_ref[...] + gt_ref[...] * jnp.dot(a_ref[...], w_ref[...], preferred_element_type=F32)


def _seq_dft_out(dmat, y, sg, w_out, x, gate):
    b, r, n2, e = y.shape
    nr = n2 // 2
    d = x.shape[2]
    _, w_spec, w_out = _layer_weights(w_out)
    return pl.pallas_call(
        _seq_dft_out_kernel,
        grid=(r, b),
        in_specs=[
            pl.BlockSpec((None, nr, n2), lambda kp, bi: (kp, 0, 0)),
            pl.BlockSpec((None, None, n2, e), lambda kp, bi: (bi, kp, 0, 0)),
            pl.BlockSpec((None, nr, e), lambda kp, bi: (bi, kp, 0)),
            w_spec,
            pl.BlockSpec((None, nr, d), lambda kp, bi: (bi, kp, 0)),
            pl.BlockSpec((None, 1, d), lambda kp, bi: (bi, 0, 0)),
        ],
        out_specs=pl.BlockSpec((None, nr, d), lambda kp, bi: (bi, kp, 0)),
        out_shape=jax.ShapeDtypeStruct(x.shape, F32),
        scratch_shapes=[pltpu.VMEM((nr, e), BF16)],
        compiler_params=_params(2),
        name="seq_dft_out",
    )(dmat, y, sg, w_out, x, gate)


def _seq_dft_rdx_matrices(n):
    r = SEQ_RADIX
    kk = r * jnp.arange(n // r, dtype=jnp.int32)[None, :, None] + jnp.arange(r, dtype=jnp.int32)[:, None, None]
    rho = jnp.arange(r, dtype=jnp.int32)[None, None, :]
    mr = r * jnp.arange(n // (r * r), dtype=jnp.int32)[None, None, :]
    ang_a = ((kk * mr) % n).astype(F32) * (2.0 * math.pi / n)
    ang_b = ((kk * rho) % n).astype(F32) * (2.0 * math.pi / n)
    ca, sa = jnp.cos(ang_a)[:, :, None, :], jnp.sin(ang_a)[:, :, None, :]
    cb, sb = jnp.cos(ang_b)[:, :, :, None], jnp.sin(ang_b)[:, :, :, None]
    c = (ca * cb - sa * sb).reshape(r, n // r, n // r)
    s = (sa * cb + ca * sb).reshape(r, n // r, n // r)
    return (jnp.concatenate([c, -s], axis=2) * (n ** -0.5)).astype(BF16)


def _to_residue_major(x, axis):
    n, r = x.shape[axis], SEQ_RADIX
    shape = x.shape[:axis] + (n // r, r) + x.shape[axis + 1:]
    return jnp.swapaxes(x.reshape(shape), axis, axis + 1).reshape(x.shape)


def _from_residue_major(x, axis):
    n, r = x.shape[axis], SEQ_RADIX
    shape = x.shape[:axis] + (r, n // r) + x.shape[axis + 1:]
    return jnp.swapaxes(x.reshape(shape), axis, axis + 1).reshape(x.shape)


def _dft_cos_sin(n):
    k = jnp.arange(n, dtype=jnp.int32)
    if n >= 1024 and n % 64 == 0:
        m1 = jnp.arange(n // 64, dtype=jnp.int32) * 64
        m2 = jnp.arange(64, dtype=jnp.int32)
        ang_a = ((k[:, None] * m1[None, :]) % n).astype(F32) * (2.0 * math.pi / n)
        ang_b = ((k[:, None] * m2[None, :]) % n).astype(F32) * (2.0 * math.pi / n)
        ca, sa = jnp.cos(ang_a)[:, :, None], jnp.sin(ang_a)[:, :, None]
        cb, sb = jnp.cos(ang_b)[:, None, :], jnp.sin(ang_b)[:, None, :]
        return (ca * cb - sa * sb).reshape(n, n), (sa * cb + ca * sb).reshape(n, n)
    ang = ((k[:, None] * k[None, :]) % n).astype(F32) * (2.0 * math.pi / n)
    return jnp.cos(ang), jnp.sin(ang)


def _seq_dft_matrix(n):
    c, s = _dft_cos_sin(n)
    return (jnp.concatenate([c, -s], axis=1) * (n ** -0.5)).astype(BF16)


def _chan_dft_matrix(gd):
    c, s = _dft_cos_sin(gd)
    return (jnp.concatenate([c, s], axis=1) * (gd ** -0.5)).astype(BF16)


def _rope(x, cos, sin_signed, first_half):
    fwd = pltpu.roll(x, LANES - 32, axis=1)
    bwd = pltpu.roll(x, 32, axis=1)
    return x * cos + jnp.where(first_half, fwd, bwd) * sin_signed


def _attn_in_kernel(*refs, e, hd, rope, q_scale):
    if rope:
        (x_ref, ng_ref, sc_ref, sh_ref, w_ref, wvt_ref, cq_ref, sq_ref, ck_ref, sk_ref,
         q_ref, k_ref, v_ref, sg_ref) = refs
        lane = lax.broadcasted_iota(jnp.int32, (1, LANES), 1)
        first_half = (lane % 64) < 32
    else:
        x_ref, ng_ref, sc_ref, sh_ref, w_ref, q_ref, k_ref, v_ref, sg_ref = refs
    hb = _modulated_norm(x_ref[...], ng_ref[...], sc_ref[...], sh_ref[...])
    hw = 2 * hd
    for j in range(e // hw):
        q = jnp.dot(hb, w_ref[:, j * hw:(j + 1) * hw], preferred_element_type=F32)
        k = jnp.dot(hb, w_ref[:, e + j * hw:e + (j + 1) * hw], preferred_element_type=F32)
        for mp in range(2):
            cols = slice(j * hw + mp * hd, j * hw + (mp + 1) * hd)
            qm, km = q[:, mp * hd:(mp + 1) * hd], k[:, mp * hd:(mp + 1) * hd]
            if rope:
                qm = _rope(qm, cq_ref[...], sq_ref[...], first_half)
                km = _rope(km, ck_ref[...], sk_ref[...], first_half)
            else:
                qm = qm * q_scale
            q_ref[:, cols] = qm.astype(q_ref.dtype)
            k_ref[:, cols] = km.astype(k_ref.dtype)
    tn = _tile(e, 512)
    for j in range(e // tn):
        cols = slice(j * tn, (j + 1) * tn)
        if rope:
            vt = lax.dot_general(wvt_ref[cols, :], hb, (((1,), (1,)), ((), ())), preferred_element_type=F32)
            v_ref[cols, :] = vt.astype(v_ref.dtype)
        else:
            v = jnp.dot(hb, w_ref[:, 2 * e + j * tn:2 * e + (j + 1) * tn], preferred_element_type=F32)
            v_ref[:, cols] = v.astype(v_ref.dtype)
        g = jnp.dot(hb, w_ref[:, 3 * e + j * tn:3 * e + (j + 1) * tn], preferred_element_type=F32)
        sg_ref[:, cols] = _silu(g).astype(BF16)


def _attn_in(x, ng, sc, sh, w_in, q_scale, rope_tables, kv_dtype):
    b, n, d = x.shape
    layer = w_in[1]
    w_shape, w_spec, w_in = _layer_weights(w_in)
    e = w_shape[1] // 4
    hd = e // (2 * N_HEADS)
    assert hd == LANES
    tm = _tile(n, 512)
    rope = rope_tables is not None
    in_specs = [
        pl.BlockSpec((None, tm, d), lambda bi, i: (bi, i, 0)),
        _resident((1, d), lambda bi, i: (0, 0)),
        pl.BlockSpec((None, 1, d), lambda bi, i: (bi, 0, 0)),
        pl.BlockSpec((None, 1, d), lambda bi, i: (bi, 0, 0)),
        w_spec,
    ]
    args = [x, ng, sc, sh, w_in]
    tok = pl.BlockSpec((None, tm, e), lambda bi, i: (bi, i, 0))
    v_spec, v_shape = tok, jax.ShapeDtypeStruct((b, n, e), kv_dtype)
    if rope:
        in_specs += [_resident((e, d), lambda bi, i: (0, 0))]
        in_specs += [pl.BlockSpec((tm, hd), lambda bi, i: (i, 0))] * 4
        args += [w_in[layer, :, 2 * e:3 * e].T] + list(rope_tables)
        v_spec = pl.BlockSpec((None, None, e, tm), lambda bi, i: (bi, i, 0, 0))
        v_shape = jax.ShapeDtypeStruct((b, n // tm, e, tm), kv_dtype)
    return pl.pallas_call(
        functools.partial(_attn_in_kernel, e=e, hd=hd, rope=rope, q_scale=q_scale),
        grid=(b, n // tm),
        in_specs=in_specs,
        out_specs=[tok, tok, v_spec, tok],
        out_shape=[
            jax.ShapeDtypeStruct((b, n, e), BF16),
            jax.ShapeDtypeStruct((b, n, e), kv_dtype),
            v_shape,
            jax.ShapeDtypeStruct((b, n, e), BF16),
        ],
        compiler_params=_params(2),
        name="attn_in_rope" if rope else "attn_in",
    )(*args)


def _lane_blocks(x):
    return [x[:, j * LANES:(j + 1) * LANES] for j in range(x.shape[1] // LANES)]


def _fold_rows(op, x):
    return op(x.reshape(x.shape[0] // SUBLANES, SUBLANES, x.shape[1]), axis=0)


def _diff_attn_kernel(*refs, n_new, n_past, ck, hd, lam_init, n_tiles, heads, vt_in):
    if n_past:
        (q_ref, qn_ref, k_ref, v_ref, pk_ref, pv_ref, sg_ref, lam_ref, sub_ref, out_ref,
         s_ref, p_ref, mcur_ref, vt_ref, r_ref, il_ref, o_ref) = refs
    else:
        (q_ref, qn_ref, k_ref, v_ref, sg_ref, lam_ref, sub_ref, out_ref,
         s_ref, p_ref, mcur_ref, vt_ref, r_ref, il_ref, o_ref) = refs
    n_new_chunks = n_new // ck
    tq = q_ref.shape[0]
    nt = (((1,), (1,)), ((), ()))
    i = pl.program_id(2)
    first = i == 0
    last = i == n_tiles - 1
    pipelined = n_tiles > 1
    cold = first if not pipelined else jnp.logical_and(
        first, jnp.logical_and(pl.program_id(0) == 0, pl.program_id(1) == 0))
    hw = 2 * hd

    def run_head(col0):
        hcols = slice(col0, col0 + hw)

        def rows_of(c):
            if isinstance(c, int):
                return slice(c * ck, (c + 1) * ck)
            return pl.ds(pl.multiple_of(c * ck, ck), ck)

        def new_chunk(c):
            vt_src = v_ref if vt_in else vt_ref
            return (lambda mp: (mp, c)), (lambda: k_ref[rows_of(c), hcols].astype(BF16)), (lambda: vt_src[c])

        past_chunk = ((lambda mp: (mp, n_new_chunks, slice(0, n_past), slice(None))),
                      (lambda: pk_ref[:, hcols].astype(BF16)),
                      (lambda: vt_ref[n_new_chunks, :, 0:n_past]))

        def scores_chunk(src_ref, chunk, mrun):
            at, load_keys, _ = chunk
            kc = load_keys()
            out = []
            for mp in range(2):
                cols = slice(mp * hd, (mp + 1) * hd)
                qcols = slice(col0 + mp * hd, col0 + (mp + 1) * hd)
                s = lax.dot_general(kc[:, cols], src_ref[:, qcols], nt, preferred_element_type=F32)
                s_ref[at(mp)] = s
                out.append(jnp.maximum(mrun[mp], _fold_rows(jnp.max, s)))
            return tuple(out)

        def probs_chunk(chunk, lsum):
            at = chunk[0]
            out = []
            for mp in range(2):
                s = s_ref[at(mp)]
                s3 = s.reshape(s.shape[0] // SUBLANES, SUBLANES, tq)
                x = (s3 - mcur_ref[mp][None]).reshape(s.shape)
                p = jnp.exp2(x.astype(BF16))
                p_ref[at(mp)] = p
                out.append(lsum[mp] + _fold_rows(jnp.sum, p.astype(F32)))
            return tuple(out)

        def values_chunk(chunk):
            at, _, load_vt = chunk
            a = p_ref[at(0)] - r_ref[0:1, :].astype(BF16) * p_ref[at(1)]
            return jnp.dot(load_vt(), a, preferred_element_type=F32)

        def finish_stats(carry):
            l1 = jnp.sum(carry[0], axis=0, keepdims=True)
            l2 = jnp.sum(carry[1], axis=0, keepdims=True)
            lam_p = lam_ref[...]
            lam = (jnp.exp(jnp.sum(lam_p[0:1] * lam_p[1:2], axis=-1, keepdims=True))
                   - jnp.exp(jnp.sum(lam_p[2:3] * lam_p[3:4], axis=-1, keepdims=True)) + lam_init)
            r_ref[...] = jnp.broadcast_to(lam * l1 / l2, (SUBLANES, tq))
            il_ref[...] = jnp.broadcast_to(1.0 / l1, (SUBLANES, tq))
            if pipelined:
                for mp in range(2):
                    m = jnp.max(carry[2 + mp], axis=0, keepdims=True)
                    mcur_ref[mp] = jnp.broadcast_to(m, (SUBLANES, tq))

        def write_tile(o, tile):
            rows = pl.ds(pl.multiple_of(tile * tq, tq), tq)
            o = o * il_ref[0:1, :]
            o = o * lax.rsqrt(jnp.mean(o * o, axis=0, keepdims=True) + EPS)
            o = jnp.concatenate([blk * sub_ref[...] for blk in _lane_blocks(o)], axis=1)
            out_ref[rows, hcols] = ((o.T * (1.0 - lam_init)) * sg_ref[rows, hcols].astype(F32)).astype(BF16)

        neg = jnp.full((SUBLANES, tq), -jnp.inf, F32)
        zero = jnp.zeros((SUBLANES, tq), F32)

        def stage_loop(with_values):
            def body(chunks, carry):
                if with_values:
                    o_ref[...] += functools.reduce(jnp.add, [values_chunk(ch) for ch in chunks])
                for ch in chunks:
                    lsum = probs_chunk(ch, carry[:2])
                    mrun = scores_chunk(qn_ref, ch, carry[2:]) if pipelined else carry[2:]
                    carry = lsum + mrun
                return carry
            chunks = [new_chunk(c) for c in range(n_new_chunks)] + ([past_chunk] if n_past else [])
            return body(chunks, (zero, zero, neg, neg))

        @pl.when(cold)
        def _():
            mrun = lax.fori_loop(0, n_new_chunks, lambda c, m: scores_chunk(q_ref, new_chunk(c), m), (neg, neg))
            if n_past:
                mrun = scores_chunk(q_ref, past_chunk, mrun)
            for mp in range(2):
                mcur_ref[mp] = jnp.broadcast_to(jnp.max(mrun[mp], axis=0, keepdims=True), (SUBLANES, tq))

        @pl.when(first)
        def _():
            if not vt_in:
                for c in range(n_new_chunks):
                    vt_ref[c] = v_ref[rows_of(c), hcols].astype(F32).T.astype(BF16)
            if n_past:
                vt_ref[n_new_chunks, :, 0:n_past] = pv_ref[:, hcols].astype(F32).T.astype(BF16)
            finish_stats(stage_loop(with_values=False))

        if pipelined:
            @pl.when(jnp.logical_not(first))
            def _():
                o_ref[...] = jnp.zeros(o_ref.shape, F32)
                carry = stage_loop(with_values=True)
                write_tile(o_ref[...], i - 1)
                finish_stats(carry)

        @pl.when(last)
        def _():
            o = None
            for c in range(n_new_chunks):
                d = values_chunk(new_chunk(c))
                o = d if o is None else o + d
            if n_past:
                o = o + values_chunk(past_chunk)
            write_tile(o, i)

    for hh in range(heads):
        run_head(hh * hw)


def _diff_attn(q, k, v, sg, lam_p, subln, lam_init, past=None):
    b, n, e = q.shape
    hw = e // N_HEADS
    hd = hw // 2
    tq = _tile(n, 512)
    ck = _tile(n, 512)
    n_tiles = n // tq
    vt_in = v.ndim == 4
    assert not vt_in or v.shape[3] == ck
    n_past = 0 if past is None else past[0].shape[2]
    assert n_past <= ck and n_past % LANES == 0
    n_chunks = n // ck + (1 if n_past else 0)
    heads = N_HEADS if n_tiles == 1 else 1
    bw = heads * hw
    tok = pl.BlockSpec((None, tq, bw), lambda bi, h, i: (bi, i, h))
    head = pl.BlockSpec((None, n, bw), lambda bi, h, i: (bi, 0, h))

    def ahead(bi, h, i):
        if n_tiles == 1:
            return bi, h, i
        wrap = i == n_tiles - 1
        g = jnp.minimum(bi * N_HEADS + h + 1, b * N_HEADS - 1)
        return (jnp.where(wrap, g // N_HEADS, bi), jnp.where(wrap, g % N_HEADS, h),
                jnp.where(wrap, 0, i + 1))

    def tok_ahead(bi, h, i):
        nb, nh, ni = ahead(bi, h, i)
        return nb, ni, nh

    def head_ahead(bi, h, i):
        nb, nh, _ = ahead(bi, h, i)
        return nb, 0, nh

    v_spec = pl.BlockSpec((None, n // ck, bw, ck), lambda bi, h, i: (bi, 0, h, 0)) if vt_in else head
    in_specs = [tok, pl.BlockSpec((None, tq, bw), tok_ahead), pl.BlockSpec((None, n, bw), head_ahead), v_spec]
    args = [q, q, k, v]
    scratch = [
        pltpu.VMEM((2, n_chunks, ck, tq), F32),
        pltpu.VMEM((2, n_chunks, ck, tq), BF16),
        pltpu.VMEM((2, SUBLANES, tq), F32),
        pltpu.VMEM((n_chunks, hw, ck), BF16),
        pltpu.VMEM((SUBLANES, tq), F32),
        pltpu.VMEM((SUBLANES, tq), F32),
        pltpu.VMEM((hw, tq), F32),
    ]
    if n_past:
        cache_k, cache_v, layer = past
        cached = pl.BlockSpec((None, None, n_past, bw), lambda bi, h, i: (bi, layer, 0, h))

        def cached_ahead(bi, h, i):
            nb, nh, _ = ahead(bi, h, i)
            return nb, layer, 0, nh

        in_specs += [pl.BlockSpec((None, None, n_past, bw), cached_ahead), cached]
        args += [cache_k, cache_v]
    in_specs += [head,
                 _resident((4, hd), lambda bi, h, i: (0, 0)),
                 _resident((hw, LANES), lambda bi, h, i: (0, 0))]
    args += [sg, lam_p, jnp.broadcast_to(subln.reshape(hw, 1), (hw, LANES))]
    return pl.pallas_call(
        functools.partial(_diff_attn_kernel, n_new=n, n_past=n_past, ck=ck, hd=hd, lam_init=lam_init,
                          n_tiles=n_tiles, heads=heads, vt_in=vt_in),
        grid=(b, N_HEADS // heads, n_tiles),
        in_specs=in_specs,
        out_specs=head,
        out_shape=jax.ShapeDtypeStruct((b, n, e), BF16),
        scratch_shapes=scratch,
        compiler_params=_params(3),
        name="diff_attn_past" if n_past else "diff_attn",
    )(*args)


def _rope_tables(n, hd, q_scale):
    axis_dim = hd // 2
    rows = n // GRID_W
    row = jnp.broadcast_to(jnp.arange(rows, dtype=F32)[:, None], (rows, GRID_W)).reshape(-1)
    colp = jnp.broadcast_to(jnp.arange(GRID_W, dtype=F32)[None, :], (rows, GRID_W)).reshape(-1)
    inv = ROPE_BASE ** (-jnp.arange(0, axis_dim, 2, dtype=F32) / axis_dim)
    ar = row[:, None] * inv[None, :]
    ac = colp[:, None] * inv[None, :]
    ang = jnp.concatenate([ar, ar, ac, ac], axis=-1)
    cos, sin = jnp.cos(ang), jnp.sin(ang)
    half = axis_dim // 2
    sign = jnp.where((jnp.arange(hd) % axis_dim) < half, -1.0, 1.0).astype(F32)
    sin_signed = sin * sign[None, :]
    return cos * q_scale, sin_signed * q_scale, cos, sin_signed


def _out_proj_kernel(*refs, final):
    if final:
        a_ref, w_ref, x_ref, gt_ref, fg_ref, y_ref = refs
    else:
        a_ref, w_ref, x_ref, gt_ref, y_ref = refs
    y = x_ref[...] + gt_ref[...] * jnp.dot(a_ref[...], w_ref[...], preferred_element_type=F32)
    if final:
        y = (y * lax.rsqrt(jnp.mean(y * y, axis=-1, keepdims=True) + EPS)) * fg_ref[...]
    y_ref[...] = y


def _out_proj(a, w_out, x, gate, final_g=None):
    b, n, d = x.shape
    e = a.shape[2]
    tm = _tile(n, 512)
    final = final_g is not None
    _, w_spec, w_out = _layer_weights(w_out)
    in_specs = [
        pl.BlockSpec((None, tm, e), lambda bi, i: (bi, i, 0)),
        w_spec,
        pl.BlockSpec((None, tm, d), lambda bi, i: (bi, i, 0)),
        pl.BlockSpec((None, 1, d), lambda bi, i: (bi, 0, 0)),
    ]
    args = [a, w_out, x, gate]
    if final:
        in_specs.append(_resident((1, d), lambda bi, i: (0, 0)))
        args.append(final_g)
    return pl.pallas_call(
        functools.partial(_out_proj_kernel, final=final),
        grid=(b, n // tm),
        in_specs=in_specs,
        out_specs=pl.BlockSpec((None, tm, d), lambda bi, i: (bi, i, 0)),
        out_shape=jax.ShapeDtypeStruct((b, n, d), F32),
        compiler_params=_params(2),
        name="out_proj_final" if final else "out_proj",
    )(*args)


def kernel(x_prompt, x_sample, cache_k, cache_v, c, c_ctx, norm_g, w_ada, b_ada, w_in_fourier, w_out_fourier, w_in_attn, w_out_attn, lam_q1, lam_k1, lam_q2, lam_k2, subln_g, final_norm_g):
    depth, d = norm_g.shape
    bp, n_ctx, _ = x_prompt.shape
    bs, n_lat, _ = x_sample.shape
    past_len = cache_k.shape[2]
    e = w_out_attn.shape[1]
    hd = e // (2 * N_HEADS)
    gd = e // N_FOURIER_GROUPS

    n_cond = 1 + bs
    rows = -(-n_cond // 8) * 8
    cond = jnp.concatenate([c_ctx[None, :], c, jnp.zeros((rows - n_cond, d), F32)], axis=0)
    mods = _adaln(cond, w_ada, b_ada).reshape(depth, rows, 3, d)

    q_scale = hd ** -0.5 * math.log2(math.e)
    tables = _rope_tables(n_lat, hd, q_scale)
    cs = _chan_dft_matrix(gd)
    dmat_p = _seq_dft_matrix(n_ctx)
    cache_k4 = cache_k.astype(BF16).reshape(bs, cache_k.shape[1], past_len, e)
    cache_v4 = cache_v.astype(BF16).reshape(bs, cache_v.shape[1], past_len, e)

    w_in_fourier, w_out_fourier = _to_bf16(w_in_fourier), _to_bf16(w_out_fourier)
    w_in_attn, w_out_attn = _to_bf16(w_in_attn), _to_bf16(w_out_attn)

    xp, xs = x_prompt, x_sample
    split_dft = n_lat % (16 * SEQ_RADIX ** 2) == 0
    if split_dft:
        xs = _to_residue_major(xs, 1)
        tables = tuple(_to_residue_major(t, 0) for t in tables)
        dmat_s = _seq_dft_rdx_matrices(n_lat)
    else:
        dmat_s = _seq_dft_matrix(n_lat)
    new_k, new_v = [], []
    for i in range(depth):
        ng = norm_g[i][None, :]
        sh_p, sc_p, gt_p = (jnp.broadcast_to(mods[i, 0:1, t][:, None, :], (bp, 1, d)) for t in range(3))
        sh_s, sc_s, gt_s = (mods[i, 1:1 + bs, t][:, None, :] for t in range(3))
        j = i // N_MIXERS
        last = i == depth - 1
        fg = final_norm_g[None, :] if last else None
        if i % N_MIXERS == 0:
            w_in = (w_in_fourier, j)
            w_out = (w_out_fourier, j)
            ab_p, sg_p = _fourier_in(xp, ng, sc_p, sh_p, w_in, cs)
            a_p = _seq_dft(dmat_p, ab_p.reshape(bp, 2 * n_ctx, e), sg_p)
            if split_dft:
                y_s, sg_s = _fourier_in_rdx(xs, ng, sc_s, sh_s, w_in, cs)
                y_s = y_s.reshape(bs, SEQ_RADIX, 2 * n_lat // SEQ_RADIX, e)
                a_s = None if fg is None else _seq_dft_rdx(dmat_s, y_s, sg_s)
            else:
                ab_s, sg_s = _fourier_in(xs, ng, sc_s, sh_s, w_in, cs)
                a_s = _seq_dft(dmat_s, ab_s.reshape(bs, 2 * n_lat, e), sg_s)
        else:
            lam_init = 0.8 - 0.6 * math.exp(-0.3 * i)
            w_in = (w_in_attn, j)
            w_out = (w_out_attn, j)
            lam_p = jnp.stack([lam_q1[j], lam_k1[j], lam_q2[j], lam_k2[j]], axis=0)
            sub = subln_g[j][None, :]
            q_p, k_p, v_p, sg_p = _attn_in(xp, ng, sc_p, sh_p, w_in, q_scale, None, F32)
            new_k.append(k_p.reshape(bp, n_ctx, N_HEADS, 2 * hd))
            new_v.append(v_p.reshape(bp, n_ctx, N_HEADS, 2 * hd))
            a_p = _diff_attn(q_p, k_p, v_p, sg_p, lam_p, sub, lam_init)
            q_s, k_s, v_s, sg_s = _attn_in(xs, ng, sc_s, sh_s, w_in, q_scale, tables, BF16)
            a_s = _diff_attn(q_s, k_s, v_s, sg_s, lam_p, sub, lam_init,
                             past=(cache_k4, cache_v4, j))
        xp = _out_proj(a_p, w_out, xp, gt_p, fg)
        if a_s is None:
            xs = _seq_dft_out(dmat_s, y_s, sg_s, w_out, xs, gt_s)
        else:
            xs = _out_proj(a_s, w_out, xs, gt_s, fg)
    if split_dft:
        xs = _from_residue_major(xs, 1)
    return (xp, xs, jnp.stack(new_k, axis=1), jnp.stack(new_v, axis=1))
```

```python
import functools
import math

import jax
import jax.numpy as jnp
from jax import lax
from jax.experimental import pallas as pl
from jax.experimental.pallas import tpu as pltpu

N_HEADS = 8
N_FOURIER_GROUPS = 8
GRID_W = 64
ROPE_BASE = 10000.0
EPS = 1e-6
N_MIXERS = 2

V7X_VMEM_BYTES = 64 * 1024 * 1024
VMEM_LIMIT_BYTES = V7X_VMEM_BYTES - 8 * 1024 * 1024
LANES = 128
SUBLANES = 8

BF16 = jnp.bfloat16
F32 = jnp.float32


def _tile(n, pref):
    t = min(n, pref)
    assert n % t == 0, (n, t)
    return t


def _params(n_axes):
    return pltpu.CompilerParams(
        dimension_semantics=("arbitrary",) * n_axes,
        vmem_limit_bytes=VMEM_LIMIT_BYTES,
    )


def _resident(block_shape, index_map):
    return pl.BlockSpec(block_shape, index_map, pipeline_mode=pl.Buffered(1))


def _layer_weights(w):
    stack, layer = w
    return stack.shape[1:], _resident((None,) + stack.shape[1:], lambda *_: (layer, 0, 0)), stack


def _to_bf16_kernel(x_ref, o_ref):
    o_ref[...] = x_ref[...].astype(BF16)


def _to_bf16(w):
    l, r, c = w.shape
    tr = _tile(r, max(8, (4 * 1024 * 1024) // (4 * c)))
    return pl.pallas_call(
        _to_bf16_kernel,
        grid=(l, r // tr),
        in_specs=[pl.BlockSpec((None, tr, c), lambda li, i: (li, i, 0))],
        out_specs=pl.BlockSpec((None, tr, c), lambda li, i: (li, i, 0)),
        out_shape=jax.ShapeDtypeStruct(w.shape, BF16),
        compiler_params=_params(2),
        name="to_bf16",
    )(w)


def _adaln_kernel(cond_ref, w_ref, b_ref, out_ref):
    cond = cond_ref[...]
    act = cond * jax.nn.sigmoid(cond)
    out_ref[...] = jnp.dot(act, w_ref[...], preferred_element_type=F32,
                           precision=lax.Precision.HIGHEST) + b_ref[...]


def _adaln(cond, w_ada, b_ada):
    depth, d, d3 = w_ada.shape
    r = cond.shape[0]
    tn = _tile(d3, 1024)
    return pl.pallas_call(
        _adaln_kernel,
        grid=(depth, d3 // tn),
        in_specs=[
            pl.BlockSpec((r, d), lambda i, j: (0, 0)),
            pl.BlockSpec((None, d, tn), lambda i, j: (i, 0, j)),
            pl.BlockSpec((None, 1, tn), lambda i, j: (i, 0, j)),
        ],
        out_specs=pl.BlockSpec((None, r, tn), lambda i, j: (i, 0, j)),
        out_shape=jax.ShapeDtypeStruct((depth, r, d3), F32),
        compiler_params=_params(2),
        name="adaln",
    )(cond, w_ada, b_ada.reshape(depth, 1, d3))


def _modulated_norm(x, ng, sc, sh):
    ms = jnp.mean(x * x, axis=-1, keepdims=True)
    h = (x * lax.rsqrt(ms + EPS)) * (ng * (1.0 + sc)) + sh
    return h.astype(BF16)


def _silu(x):
    return x * jax.nn.sigmoid(x)


def _fourier_in_kernel(x_ref, ng_ref, sc_ref, sh_ref, w_ref, cs_ref, ab_ref, sg_ref, u_ref, *, e, gd):
    hb = _modulated_norm(x_ref[...], ng_ref[...], sc_ref[...], sh_ref[...])
    tn = _tile(e, 512)
    for j in range(e // tn):
        cols = slice(j * tn, (j + 1) * tn)
        u_ref[:, cols] = jnp.dot(hb, w_ref[:, cols], preferred_element_type=F32).astype(BF16)
    for grp in range(e // gd):
        cols = slice(grp * gd, (grp + 1) * gd)
        ab = jnp.dot(u_ref[:, cols], cs_ref[...], preferred_element_type=F32)
        ab_ref[0, :, cols] = ab[:, :gd].astype(BF16)
        ab_ref[1, :, cols] = ab[:, gd:].astype(BF16)
    for j in range(e // tn):
        cols = slice(j * tn, (j + 1) * tn)
        g = jnp.dot(hb, w_ref[:, e + j * tn:e + (j + 1) * tn], preferred_element_type=F32)
        sg_ref[:, cols] = _silu(g).astype(BF16)


def _fourier_in(x, ng, sc, sh, w_in, cs):
    b, n, d = x.shape
    w_shape, w_spec, w_in = _layer_weights(w_in)
    e = w_shape[1] // 2
    gd = e // N_FOURIER_GROUPS
    tm = _tile(n, 512)
    return pl.pallas_call(
        functools.partial(_fourier_in_kernel, e=e, gd=gd),
        grid=(b, n // tm),
        in_specs=[
            pl.BlockSpec((None, tm, d), lambda bi, i: (bi, i, 0)),
            _resident((1, d), lambda bi, i: (0, 0)),
            pl.BlockSpec((None, 1, d), lambda bi, i: (bi, 0, 0)),
            pl.BlockSpec((None, 1, d), lambda bi, i: (bi, 0, 0)),
            w_spec,
            _resident((gd, 2 * gd), lambda bi, i: (0, 0)),
        ],
        out_specs=[
            pl.BlockSpec((None, 2, tm, e), lambda bi, i: (bi, 0, i, 0)),
            pl.BlockSpec((None, tm, e), lambda bi, i: (bi, i, 0)),
        ],
        out_shape=[
            jax.ShapeDtypeStruct((b, 2, n, e), BF16),
            jax.ShapeDtypeStruct((b, n, e), BF16),
        ],
        scratch_shapes=[pltpu.VMEM((tm, e), BF16)],
        compiler_params=_params(2),
        name="fourier_in",
    )(x, ng, sc, sh, w_in, cs)


def _seq_dft_kernel(d_ref, ab_ref, sg_ref, out_ref, acc_ref):
    k = pl.program_id(2)
    part = jnp.dot(d_ref[...], ab_ref[...], preferred_element_type=F32)

    @pl.when(k == 0)
    def _():
        acc_ref[...] = part

    @pl.when(k > 0)
    def _():
        acc_ref[...] += part

    @pl.when(k == pl.num_programs(2) - 1)
    def _():
        out_ref[...] = (acc_ref[...] * sg_ref[...].astype(F32)).astype(BF16)


def _seq_dft(dmat, ab, sg):
    b, n2, e = ab.shape
    n = n2 // 2
    tm = _tile(n, 1024)
    tk = _tile(n2, 1024)
    return pl.pallas_call(
        _seq_dft_kernel,
        grid=(b, n // tm, n2 // tk),
        in_specs=[
            pl.BlockSpec((tm, tk), lambda bi, i, k: (i, k)),
            pl.BlockSpec((None, tk, e), lambda bi, i, k: (bi, k, 0)),
            pl.BlockSpec((None, tm, e), lambda bi, i, k: (bi, i, 0)),
        ],
        out_specs=pl.BlockSpec((None, tm, e), lambda bi, i, k: (bi, i, 0)),
        out_shape=jax.ShapeDtypeStruct((b, n, e), BF16),
        scratch_shapes=[pltpu.VMEM((tm, e), F32)],
        compiler_params=_params(3),
        name="seq_dft",
    )(dmat, ab, sg)


SEQ_RADIX = 8


def _twiddle(z, j, r):
    x, y = z
    if j == 0:
        return z
    if 4 * j == r:
        return y, -x
    if 8 * j == r:
        return (x + y) * math.sqrt(0.5), (y - x) * math.sqrt(0.5)
    if 8 * j == 3 * r:
        return (y - x) * math.sqrt(0.5), (-x - y) * math.sqrt(0.5)
    c, sn = math.cos(2.0 * math.pi * j / r), math.sin(2.0 * math.pi * j / r)
    return x * c + y * sn, y * c - x * sn


def _dft_blocks(z):
    r = len(z)
    if r == 1:
        return z
    h = r // 2
    even = _dft_blocks([(z[j][0] + z[j + h][0], z[j][1] + z[j + h][1]) for j in range(h)])
    odd = _dft_blocks([_twiddle((z[j][0] - z[j + h][0], z[j][1] - z[j + h][1]), j, r) for j in range(h)])
    out = [None] * r
    out[0::2], out[1::2] = even, odd
    return out


def _fourier_in_rdx_kernel(x_ref, ng_ref, sc_ref, sh_ref, w_ref, cs_ref, y_ref, sg_ref, u_ref, *, e, gd, radix, res):
    hb = _modulated_norm(x_ref[...], ng_ref[...], sc_ref[...], sh_ref[...])
    rows = x_ref.shape[0] // res
    q = rows // radix
    tn = _tile(e, 512)
    for j in range(e // tn):
        cols = slice(j * tn, (j + 1) * tn)
        u_ref[:, cols] = jnp.dot(hb, w_ref[:, cols], preferred_element_type=F32).astype(BF16)
    for grp in range(e // gd):
        cols = slice(grp * gd, (grp + 1) * gd)
        ab = jnp.dot(u_ref[:, cols], cs_ref[...], preferred_element_type=F32)
        for t in range(res):
            blk = [slice(t * rows + j * q, t * rows + (j + 1) * q) for j in range(radix)]
            y = _dft_blocks([(ab[b, :gd], -ab[b, gd:]) for b in blk])
            for kappa in range(radix):
                y_ref[kappa, 0, t * q:(t + 1) * q, cols] = y[kappa][0].astype(BF16)
                y_ref[kappa, 1, t * q:(t + 1) * q, cols] = (-y[kappa][1]).astype(BF16)
    for j in range(e // tn):
        cols = slice(j * tn, (j + 1) * tn)
        g = jnp.dot(hb, w_ref[:, e + j * tn:e + (j + 1) * tn], preferred_element_type=F32)
        sg_ref[:, cols] = _silu(g).astype(BF16)


def _fourier_in_rdx(x, ng, sc, sh, w_in, cs):
    b, n, d = x.shape
    w_shape, w_spec, w_in = _layer_weights(w_in)
    e = w_shape[1] // 2
    gd = e // N_FOURIER_GROUPS
    r = SEQ_RADIX
    res = 2
    tm = res * n // r
    return pl.pallas_call(
        functools.partial(_fourier_in_rdx_kernel, e=e, gd=gd, radix=r, res=res),
        grid=(b, r // res),
        in_specs=[
            pl.BlockSpec((None, tm, d), lambda bi, i: (bi, i, 0)),
            _resident((1, d), lambda bi, i: (0, 0)),
            pl.BlockSpec((None, 1, d), lambda bi, i: (bi, 0, 0)),
            pl.BlockSpec((None, 1, d), lambda bi, i: (bi, 0, 0)),
            w_spec,
            _resident((gd, 2 * gd), lambda bi, i: (0, 0)),
        ],
        out_specs=[
            pl.BlockSpec((None, r, 2, tm // r, e), lambda bi, i: (bi, 0, 0, i, 0)),
            pl.BlockSpec((None, tm, e), lambda bi, i: (bi, i, 0)),
        ],
        out_shape=[
            jax.ShapeDtypeStruct((b, r, 2, n // r, e), BF16),
            jax.ShapeDtypeStruct((b, n, e), BF16),
        ],
        scratch_shapes=[pltpu.VMEM((tm, e), BF16)],
        compiler_params=_params(2),
        name="fourier_in_rdx",
    )(x, ng, sc, sh, w_in, cs)


def _seq_dft_rdx_kernel(d_ref, y_ref, sg_ref, out_ref):
    tn = _tile(out_ref.shape[1], 512)
    for j in range(out_ref.shape[1] // tn):
        cols = slice(j * tn, (j + 1) * tn)
        f = jnp.dot(d_ref[...], y_ref[:, cols], preferred_element_type=F32)
        out_ref[:, cols] = (f * sg_ref[:, cols].astype(F32)).astype(BF16)


def _seq_dft_rdx(dmat, y, sg):
    b, r, n2, e = y.shape
    nr = n2 // 2
    return pl.pallas_call(
        _seq_dft_rdx_kernel,
        grid=(r, b),
        in_specs=[
            pl.BlockSpec((None, nr, n2), lambda kp, bi: (kp, 0, 0)),
            pl.BlockSpec((None, None, n2, e), lambda kp, bi: (bi, kp, 0, 0)),
            pl.BlockSpec((None, nr, e), lambda kp, bi: (bi, kp, 0)),
        ],
        out_specs=pl.BlockSpec((None, nr, e), lambda kp, bi: (bi, kp, 0)),
        out_shape=jax.ShapeDtypeStruct((b, r * nr, e), BF16),
        compiler_params=_params(2),
        name="seq_dft_rdx",
    )(dmat, y, sg)


def _seq_dft_out_kernel(d_ref, y_ref, sg_ref, w_ref, x_ref, gt_ref, out_ref, a_ref):
    tn = _tile(a_ref.shape[1], 512)
    for j in range(a_ref.shape[1] // tn):
        cols = slice(j * tn, (j + 1) * tn)
        f = jnp.dot(d_ref[...], y_ref[:, cols], preferred_element_type=F32)
        a_ref[:, cols] = (f * sg_ref[:, cols].astype(F32)).astype(BF16)
    out_ref[...] = x_ref[...] + gt_ref[...] * jnp.dot(a_ref[...], w_ref[...], preferred_element_type=F32)


def _seq_dft_out(dmat, y, sg, w_out, x, gate):
    b, r, n2, e = y.shape
    nr = n2 // 2
    d = x.shape[2]
    _, w_spec, w_out = _layer_weights(w_out)
    return pl.pallas_call(
        _seq_dft_out_kernel,
        grid=(r, b),
        in_specs=[
            pl.BlockSpec((None, nr, n2), lambda kp, bi: (kp, 0, 0)),
            pl.BlockSpec((None, None, n2, e), lambda kp, bi: (bi, kp, 0, 0)),
            pl.BlockSpec((None, nr, e), lambda kp, bi: (bi, kp, 0)),
            w_spec,
            pl.BlockSpec((None, nr, d), lambda kp, bi: (bi, kp, 0)),
            pl.BlockSpec((None, 1, d), lambda kp, bi: (bi, 0, 0)),
        ],
        out_specs=pl.BlockSpec((None, nr, d), lambda kp, bi: (bi, kp, 0)),
        out_shape=jax.ShapeDtypeStruct(x.shape, F32),
        scratch_shapes=[pltpu.VMEM((nr, e), BF16)],
        compiler_params=_params(2),
        name="seq_dft_out",
    )(dmat, y, sg, w_out, x, gate)


def _seq_dft_rdx_matrices(n):
    r = SEQ_RADIX
    kk = r * jnp.arange(n // r, dtype=jnp.int32)[None, :, None] + jnp.arange(r, dtype=jnp.int32)[:, None, None]
    rho = jnp.arange(r, dtype=jnp.int32)[None, None, :]
    mr = r * jnp.arange(n // (r * r), dtype=jnp.int32)[None, None, :]
    ang_a = ((kk * mr) % n).astype(F32) * (2.0 * math.pi / n)
    ang_b = ((kk * rho) % n).astype(F32) * (2.0 * math.pi / n)
    ca, sa = jnp.cos(ang_a)[:, :, None, :], jnp.sin(ang_a)[:, :, None, :]
    cb, sb = jnp.cos(ang_b)[:, :, :, None], jnp.sin(ang_b)[:, :, :, None]
    c = (ca * cb - sa * sb).reshape(r, n // r, n // r)
    s = (sa * cb + ca * sb).reshape(r, n // r, n // r)
    return (jnp.concatenate([c, -s], axis=2) * (n ** -0.5)).astype(BF16)


def _to_residue_major(x, axis):
    n, r = x.shape[axis], SEQ_RADIX
    shape = x.shape[:axis] + (n // r, r) + x.shape[axis + 1:]
    return jnp.swapaxes(x.reshape(shape), axis, axis + 1).reshape(x.shape)


def _from_residue_major(x, axis):
    n, r = x.shape[axis], SEQ_RADIX
    shape = x.shape[:axis] + (r, n // r) + x.shape[axis + 1:]
    return jnp.swapaxes(x.reshape(shape), axis, axis + 1).reshape(x.shape)


def _dft_cos_sin(n):
    k = jnp.arange(n, dtype=jnp.int32)
    if n >= 1024 and n % 64 == 0:
        m1 = jnp.arange(n // 64, dtype=jnp.int32) * 64
        m2 = jnp.arange(64, dtype=jnp.int32)
        ang_a = ((k[:, None] * m1[None, :]) % n).astype(F32) * (2.0 * math.pi / n)
        ang_b = ((k[:, None] * m2[None, :]) % n).astype(F32) * (2.0 * math.pi / n)
        ca, sa = jnp.cos(ang_a)[:, :, None], jnp.sin(ang_a)[:, :, None]
        cb, sb = jnp.cos(ang_b)[:, None, :], jnp.sin(ang_b)[:, None, :]
        return (ca * cb - sa * sb).reshape(n, n), (sa * cb + ca * sb).reshape(n, n)
    ang = ((k[:, None] * k[None, :]) % n).astype(F32) * (2.0 * math.pi / n)
    return jnp.cos(ang), jnp.sin(ang)


def _seq_dft_matrix(n):
    c, s = _dft_cos_sin(n)
    return (jnp.concatenate([c, -s], axis=1) * (n ** -0.5)).astype(BF16)


def _chan_dft_matrix(gd):
    c, s = _dft_cos_sin(gd)
    return (jnp.concatenate([c, s], axis=1) * (gd ** -0.5)).astype(BF16)


def _rope(x, cos, sin_signed, first_half):
    fwd = pltpu.roll(x, LANES - 32, axis=1)
    bwd = pltpu.roll(x, 32, axis=1)
    return x * cos + jnp.where(first_half, fwd, bwd) * sin_signed


def _attn_in_kernel(*refs, e, hd, rope, q_scale):
    if rope:
        (x_ref, ng_ref, sc_ref, sh_ref, w_ref, wvt_ref, cq_ref, sq_ref, ck_ref, sk_ref,
         q_ref, k_ref, v_ref, sg_ref) = refs
        lane = lax.broadcasted_iota(jnp.int32, (1, LANES), 1)
        first_half = (lane % 64) < 32
    else:
        x_ref, ng_ref, sc_ref, sh_ref, w_ref, q_ref, k_ref, v_ref, sg_ref = refs
    hb = _modulated_norm(x_ref[...], ng_ref[...], sc_ref[...], sh_ref[...])
    hw = 2 * hd
    for j in range(e // hw):
        q = jnp.dot(hb, w_ref[:, j * hw:(j + 1) * hw], preferred_element_type=F32)
        k = jnp.dot(hb, w_ref[:, e + j * hw:e + (j + 1) * hw], preferred_element_type=F32)
        for mp in range(2):
            cols = slice(j * hw + mp * hd, j * hw + (mp + 1) * hd)
            qm, km = q[:, mp * hd:(mp + 1) * hd], k[:, mp * hd:(mp + 1) * hd]
            if rope:
                qm = _rope(qm, cq_ref[...], sq_ref[...], first_half)
                km = _rope(km, ck_ref[...], sk_ref[...], first_half)
            else:
                qm = qm * q_scale
            q_ref[:, cols] = qm.astype(q_ref.dtype)
            k_ref[:, cols] = km.astype(k_ref.dtype)
    tn = _tile(e, 512)
    for j in range(e // tn):
        cols = slice(j * tn, (j + 1) * tn)
        if rope:
            vt = lax.dot_general(wvt_ref[cols, :], hb, (((1,), (1,)), ((), ())), preferred_element_type=F32)
            v_ref[cols, :] = vt.astype(v_ref.dtype)
        else:
            v = jnp.dot(hb, w_ref[:, 2 * e + j * tn:2 * e + (j + 1) * tn], preferred_element_type=F32)
            v_ref[:, cols] = v.astype(v_ref.dtype)
        g = jnp.dot(hb, w_ref[:, 3 * e + j * tn:3 * e + (j + 1) * tn], preferred_element_type=F32)
        sg_ref[:, cols] = _silu(g).astype(BF16)


def _attn_in(x, ng, sc, sh, w_in, q_scale, rope_tables, kv_dtype):
    b, n, d = x.shape
    layer = w_in[1]
    w_shape, w_spec, w_in = _layer_weights(w_in)
    e = w_shape[1] // 4
    hd = e // (2 * N_HEADS)
    assert hd == LANES
    tm = _tile(n, 512)
    rope = rope_tables is not None
    in_specs = [
        pl.BlockSpec((None, tm, d), lambda bi, i: (bi, i, 0)),
        _resident((1, d), lambda bi, i: (0, 0)),
        pl.BlockSpec((None, 1, d), lambda bi, i: (bi, 0, 0)),
        pl.BlockSpec((None, 1, d), lambda bi, i: (bi, 0, 0)),
        w_spec,
    ]
    args = [x, ng, sc, sh, w_in]
    tok = pl.BlockSpec((None, tm, e), lambda bi, i: (bi, i, 0))
    v_spec, v_shape = tok, jax.ShapeDtypeStruct((b, n, e), kv_dtype)
    if rope:
        in_specs += [_resident((e, d), lambda bi, i: (0, 0))]
        in_specs += [pl.BlockSpec((tm, hd), lambda bi, i: (i, 0))] * 4
        args += [w_in[layer, :, 2 * e:3 * e].T] + list(rope_tables)
        v_spec = pl.BlockSpec((None, None, e, tm), lambda bi, i: (bi, i, 0, 0))
        v_shape = jax.ShapeDtypeStruct((b, n // tm, e, tm), kv_dtype)
    return pl.pallas_call(
        functools.partial(_attn_in_kernel, e=e, hd=hd, rope=rope, q_scale=q_scale),
        grid=(b, n // tm),
        in_specs=in_specs,
        out_specs=[tok, tok, v_spec, tok],
        out_shape=[
            jax.ShapeDtypeStruct((b, n, e), BF16),
            jax.ShapeDtypeStruct((b, n, e), kv_dtype),
            v_shape,
            jax.ShapeDtypeStruct((b, n, e), BF16),
        ],
        compiler_params=_params(2),
        name="attn_in_rope" if rope else "attn_in",
    )(*args)


def _lane_blocks(x):
    return [x[:, j * LANES:(j + 1) * LANES] for j in range(x.shape[1] // LANES)]


def _fold_rows(op, x):
    return op(x.reshape(x.shape[0] // SUBLANES, SUBLANES, x.shape[1]), axis=0)


def _diff_attn_kernel(*refs, n_new, n_past, ck, hd, lam_init, n_tiles, heads, vt_in):
    if n_past:
        (q_ref, qn_ref, k_ref, v_ref, pk_ref, pv_ref, sg_ref, lam_ref, sub_ref, out_ref,
         s_ref, p_ref, mcur_ref, vt_ref, r_ref, il_ref, o_ref) = refs
    else:
        (q_ref, qn_ref, k_ref, v_ref, sg_ref, lam_ref, sub_ref, out_ref,
         s_ref, p_ref, mcur_ref, vt_ref, r_ref, il_ref, o_ref) = refs
    n_new_chunks = n_new // ck
    tq = q_ref.shape[0]
    i = pl.program_id(2)
    first = i == 0
    last = i == n_tiles - 1
    pipelined = n_tiles > 1
    cold = first if not pipelined else jnp.logical_and(
        first, jnp.logical_and(pl.program_id(0) == 0, pl.program_id(1) == 0))
    hw = 2 * hd

    def run_head(col0):
        hcols = slice(col0, col0 + hw)

        def rows_of(c):
            if isinstance(c, int):
                return slice(c * ck, (c + 1) * ck)
            return pl.ds(pl.multiple_of(c * ck, ck), ck)

        def new_chunk(c):
            vt_src = v_ref if vt_in else vt_ref
            return (lambda mp: (mp, c)), (lambda: k_ref[rows_of(c), hcols].astype(BF16)), (lambda: vt_src[c])

        past_chunk = ((lambda mp: (mp, n_new_chunks, slice(0, n_past), slice(None))),
                      (lambda: pk_ref[:, hcols].astype(BF16)),
                      (lambda: vt_ref[n_new_chunks, :, 0:n_past]))

        def transposed(src_ref):
            return src_ref[:, hcols].astype(F32).T.astype(BF16)

        def scores_chunk(qt, chunk, mrun):
            at, load_keys, _ = chunk
            kc = load_keys()
            out = []
            for mp in range(2):
                cols = slice(mp * hd, (mp + 1) * hd)
                s = jnp.dot(kc[:, cols], qt[cols, :], preferred_element_type=F32)
                s_ref[at(mp)] = s
                out.append(jnp.maximum(mrun[mp], _fold_rows(jnp.max, s)))
            return tuple(out)

        def probs_chunk(chunk, lsum):
            at = chunk[0]
            out = []
            for mp in range(2):
                s = s_ref[at(mp)]
                s3 = s.reshape(s.shape[0] // SUBLANES, SUBLANES, tq)
                p3 = jnp.exp2(s3 - mcur_ref[mp][None])
                p_ref[at(mp)] = p3.reshape(s.shape).astype(BF16)
                out.append(lsum[mp] + jnp.sum(p3, axis=0))
            return tuple(out)

        def values_chunk(chunk):
            at, _, load_vt = chunk
            a = p_ref[at(0)] - r_ref[0:1, :].astype(BF16) * p_ref[at(1)]
            return jnp.dot(load_vt(), a, preferred_element_type=F32)

        def finish_stats(carry):
            l1 = jnp.sum(carry[0], axis=0, keepdims=True)
            l2 = jnp.sum(carry[1], axis=0, keepdims=True)
            lam_p = lam_ref[...]
            lam = (jnp.exp(jnp.sum(lam_p[0:1] * lam_p[1:2], axis=-1, keepdims=True))
                   - jnp.exp(jnp.sum(lam_p[2:3] * lam_p[3:4], axis=-1, keepdims=True)) + lam_init)
            r_ref[...] = jnp.broadcast_to(lam * l1 / l2, (SUBLANES, tq))
            il_ref[...] = jnp.broadcast_to(1.0 / l1, (SUBLANES, tq))
            if pipelined:
                for mp in range(2):
                    m = jnp.max(carry[2 + mp], axis=0, keepdims=True)
                    mcur_ref[mp] = jnp.broadcast_to(m, (SUBLANES, tq))

        def write_tile(o, tile):
            rows = pl.ds(pl.multiple_of(tile * tq, tq), tq)
            o = o * il_ref[0:1, :]
            o = o * lax.rsqrt(jnp.mean(o * o, axis=0, keepdims=True) + EPS)
            o = jnp.concatenate([blk * sub_ref[...] for blk in _lane_blocks(o)], axis=1)
            out_ref[rows, hcols] = ((o.T * (1.0 - lam_init)) * sg_ref[rows, hcols].astype(F32)).astype(BF16)

        neg = jnp.full((SUBLANES, tq), -jnp.inf, F32)
        zero = jnp.zeros((SUBLANES, tq), F32)

        def stage_loop(with_values):
            def body(chunks, carry):
                qt = transposed(qn_ref) if pipelined else None
                if with_values:
                    o_ref[...] += functools.reduce(jnp.add, [values_chunk(ch) for ch in chunks])
                for ch in chunks:
                    lsum = probs_chunk(ch, carry[:2])
                    mrun = scores_chunk(qt, ch, carry[2:]) if pipelined else carry[2:]
                    carry = lsum + mrun
                return carry
            chunks = [new_chunk(c) for c in range(n_new_chunks)] + ([past_chunk] if n_past else [])
            return body(chunks, (zero, zero, neg, neg))

        @pl.when(cold)
        def _():
            qt = transposed(q_ref)
            mrun = lax.fori_loop(0, n_new_chunks, lambda c, m: scores_chunk(qt, new_chunk(c), m), (neg, neg))
            if n_past:
                mrun = scores_chunk(qt, past_chunk, mrun)
            for mp in range(2):
                mcur_ref[mp] = jnp.broadcast_to(jnp.max(mrun[mp], axis=0, keepdims=True), (SUBLANES, tq))

        @pl.when(first)
        def _():
            if not vt_in:
                for c in range(n_new_chunks):
                    vt_ref[c] = v_ref[rows_of(c), hcols].astype(F32).T.astype(BF16)
            if n_past:
                vt_ref[n_new_chunks, :, 0:n_past] = pv_ref[:, hcols].astype(F32).T.astype(BF16)
            finish_stats(stage_loop(with_values=False))

        if pipelined:
            @pl.when(jnp.logical_not(first))
            def _():
                o_ref[...] = jnp.zeros(o_ref.shape, F32)
                carry = stage_loop(with_values=True)
                write_tile(o_ref[...], i - 1)
                finish_stats(carry)

        @pl.when(last)
        def _():
            o = None
            for c in range(n_new_chunks):
                d = values_chunk(new_chunk(c))
                o = d if o is None else o + d
            if n_past:
                o = o + values_chunk(past_chunk)
            write_tile(o, i)

    for hh in range(heads):
        run_head(hh * hw)


def _diff_attn(q, k, v, sg, lam_p, subln, lam_init, past=None):
    b, n, e = q.shape
    hw = e // N_HEADS
    hd = hw // 2
    tq = _tile(n, 512)
    ck = _tile(n, 512)
    n_tiles = n // tq
    vt_in = v.ndim == 4
    assert not vt_in or v.shape[3] == ck
    n_past = 0 if past is None else past[0].shape[2]
    assert n_past <= ck and n_past % LANES == 0
    n_chunks = n // ck + (1 if n_past else 0)
    heads = N_HEADS if n_tiles == 1 else 1
    bw = heads * hw
    tok = pl.BlockSpec((None, tq, bw), lambda bi, h, i: (bi, i, h))
    head = pl.BlockSpec((None, n, bw), lambda bi, h, i: (bi, 0, h))

    def ahead(bi, h, i):
        if n_tiles == 1:
            return bi, h, i
        wrap = i == n_tiles - 1
        g = jnp.minimum(bi * N_HEADS + h + 1, b * N_HEADS - 1)
        return (jnp.where(wrap, g // N_HEADS, bi), jnp.where(wrap, g % N_HEADS, h),
                jnp.where(wrap, 0, i + 1))

    def tok_ahead(bi, h, i):
        nb, nh, ni = ahead(bi, h, i)
        return nb, ni, nh

    def head_ahead(bi, h, i):
        nb, nh, _ = ahead(bi, h, i)
        return nb, 0, nh

    v_spec = pl.BlockSpec((None, n // ck, bw, ck), lambda bi, h, i: (bi, 0, h, 0)) if vt_in else head
    in_specs = [tok, pl.BlockSpec((None, tq, bw), tok_ahead), pl.BlockSpec((None, n, bw), head_ahead), v_spec]
    args = [q, q, k, v]
    scratch = [
        pltpu.VMEM((2, n_chunks, ck, tq), F32),
        pltpu.VMEM((2, n_chunks, ck, tq), BF16),
        pltpu.VMEM((2, SUBLANES, tq), F32),
        pltpu.VMEM((n_chunks, hw, ck), BF16),
        pltpu.VMEM((SUBLANES, tq), F32),
        pltpu.VMEM((SUBLANES, tq), F32),
        pltpu.VMEM((hw, tq), F32),
    ]
    if n_past:
        cache_k, cache_v, layer = past
        cached = pl.BlockSpec((None, None, n_past, bw), lambda bi, h, i: (bi, layer, 0, h))

        def cached_ahead(bi, h, i):
            nb, nh, _ = ahead(bi, h, i)
            return nb, layer, 0, nh

        in_specs += [pl.BlockSpec((None, None, n_past, bw), cached_ahead), cached]
        args += [cache_k, cache_v]
    in_specs += [head,
                 _resident((4, hd), lambda bi, h, i: (0, 0)),
                 _resident((hw, LANES), lambda bi, h, i: (0, 0))]
    args += [sg, lam_p, jnp.broadcast_to(subln.reshape(hw, 1), (hw, LANES))]
    return pl.pallas_call(
        functools.partial(_diff_attn_kernel, n_new=n, n_past=n_past, ck=ck, hd=hd, lam_init=lam_init,
                          n_tiles=n_tiles, heads=heads, vt_in=vt_in),
        grid=(b, N_HEADS // heads, n_tiles),
        in_specs=in_specs,
        out_specs=head,
        out_shape=jax.ShapeDtypeStruct((b, n, e), BF16),
        scratch_shapes=scratch,
        compiler_params=_params(3),
        name="diff_attn_past" if n_past else "diff_attn",
    )(*args)


def _rope_tables(n, hd, q_scale):
    axis_dim = hd // 2
    rows = n // GRID_W
    row = jnp.broadcast_to(jnp.arange(rows, dtype=F32)[:, None], (rows, GRID_W)).reshape(-1)
    colp = jnp.broadcast_to(jnp.arange(GRID_W, dtype=F32)[None, :], (rows, GRID_W)).reshape(-1)
    inv = ROPE_BASE ** (-jnp.arange(0, axis_dim, 2, dtype=F32) / axis_dim)
    ar = row[:, None] * inv[None, :]
    ac = colp[:, None] * inv[None, :]
    ang = jnp.concatenate([ar, ar, ac, ac], axis=-1)
    cos, sin = jnp.cos(ang), jnp.sin(ang)
    half = axis_dim // 2
    sign = jnp.where((jnp.arange(hd) % axis_dim) < half, -1.0, 1.0).astype(F32)
    sin_signed = sin * sign[None, :]
    return cos * q_scale, sin_signed * q_scale, cos, sin_signed


def _out_proj_kernel(*refs, final):
    if final:
        a_ref, w_ref, x_ref, gt_ref, fg_ref, y_ref = refs
    else:
        a_ref, w_ref, x_ref, gt_ref, y_ref = refs
    y = x_ref[...] + gt_ref[...] * jnp.dot(a_ref[...], w_ref[...], preferred_element_type=F32)
    if final:
        y = (y * lax.rsqrt(jnp.mean(y * y, axis=-1, keepdims=True) + EPS)) * fg_ref[...]
    y_ref[...] = y


def _out_proj(a, w_out, x, gate, final_g=None):
    b, n, d = x.shape
    e = a.shape[2]
    tm = _tile(n, 512)
    final = final_g is not None
    _, w_spec, w_out = _layer_weights(w_out)
    in_specs = [
        pl.BlockSpec((None, tm, e), lambda bi, i: (bi, i, 0)),
        w_spec,
        pl.BlockSpec((None, tm, d), lambda bi, i: (bi, i, 0)),
        pl.BlockSpec((None, 1, d), lambda bi, i: (bi, 0, 0)),
    ]
    args = [a, w_out, x, gate]
    if final:
        in_specs.append(_resident((1, d), lambda bi, i: (0, 0)))
        args.append(final_g)
    return pl.pallas_call(
        functools.partial(_out_proj_kernel, final=final),
        grid=(b, n // tm),
        in_specs=in_specs,
        out_specs=pl.BlockSpec((None, tm, d), lambda bi, i: (bi, i, 0)),
        out_shape=jax.ShapeDtypeStruct((b, n, d), F32),
        compiler_params=_params(2),
        name="out_proj_final" if final else "out_proj",
    )(*args)


def kernel(x_prompt, x_sample, cache_k, cache_v, c, c_ctx, norm_g, w_ada, b_ada, w_in_fourier, w_out_fourier, w_in_attn, w_out_attn, lam_q1, lam_k1, lam_q2, lam_k2, subln_g, final_norm_g):
    depth, d = norm_g.shape
    bp, n_ctx, _ = x_prompt.shape
    bs, n_lat, _ = x_sample.shape
    past_len = cache_k.shape[2]
    e = w_out_attn.shape[1]
    hd = e // (2 * N_HEADS)
    gd = e // N_FOURIER_GROUPS

    n_cond = 1 + bs
    rows = -(-n_cond // 8) * 8
    cond = jnp.concatenate([c_ctx[None, :], c, jnp.zeros((rows - n_cond, d), F32)], axis=0)
    mods = _adaln(cond, w_ada, b_ada).reshape(depth, rows, 3, d)

    q_scale = hd ** -0.5 * math.log2(math.e)
    tables = _rope_tables(n_lat, hd, q_scale)
    cs = _chan_dft_matrix(gd)
    dmat_p = _seq_dft_matrix(n_ctx)
    cache_k4 = cache_k.astype(BF16).reshape(bs, cache_k.shape[1], past_len, e)
    cache_v4 = cache_v.astype(BF16).reshape(bs, cache_v.shape[1], past_len, e)

    w_in_fourier, w_out_fourier = _to_bf16(w_in_fourier), _to_bf16(w_out_fourier)
    w_in_attn, w_out_attn = _to_bf16(w_in_attn), _to_bf16(w_out_attn)

    xp, xs = x_prompt, x_sample
    split_dft = n_lat % (16 * SEQ_RADIX ** 2) == 0
    if split_dft:
        xs = _to_residue_major(xs, 1)
        tables = tuple(_to_residue_major(t, 0) for t in tables)
        dmat_s = _seq_dft_rdx_matrices(n_lat)
    else:
        dmat_s = _seq_dft_matrix(n_lat)
    new_k, new_v = [], []
    for i in range(depth):
        ng = norm_g[i][None, :]
        sh_p, sc_p, gt_p = (jnp.broadcast_to(mods[i, 0:1, t][:, None, :], (bp, 1, d)) for t in range(3))
        sh_s, sc_s, gt_s = (mods[i, 1:1 + bs, t][:, None, :] for t in range(3))
        j = i // N_MIXERS
        last = i == depth - 1
        fg = final_norm_g[None, :] if last else None
        if i % N_MIXERS == 0:
            w_in = (w_in_fourier, j)
            w_out = (w_out_fourier, j)
            ab_p, sg_p = _fourier_in(xp, ng, sc_p, sh_p, w_in, cs)
            a_p = _seq_dft(dmat_p, ab_p.reshape(bp, 2 * n_ctx, e), sg_p)
            if split_dft:
                y_s, sg_s = _fourier_in_rdx(xs, ng, sc_s, sh_s, w_in, cs)
                y_s = y_s.reshape(bs, SEQ_RADIX, 2 * n_lat // SEQ_RADIX, e)
                a_s = None if fg is None else _seq_dft_rdx(dmat_s, y_s, sg_s)
            else:
                ab_s, sg_s = _fourier_in(xs, ng, sc_s, sh_s, w_in, cs)
                a_s = _seq_dft(dmat_s, ab_s.reshape(bs, 2 * n_lat, e), sg_s)
        else:
            lam_init = 0.8 - 0.6 * math.exp(-0.3 * i)
            w_in = (w_in_attn, j)
            w_out = (w_out_attn, j)
            lam_p = jnp.stack([lam_q1[j], lam_k1[j], lam_q2[j], lam_k2[j]], axis=0)
            sub = subln_g[j][None, :]
            q_p, k_p, v_p, sg_p = _attn_in(xp, ng, sc_p, sh_p, w_in, q_scale, None, F32)
            new_k.append(k_p.reshape(bp, n_ctx, N_HEADS, 2 * hd))
            new_v.append(v_p.reshape(bp, n_ctx, N_HEADS, 2 * hd))
            a_p = _diff_attn(q_p, k_p, v_p, sg_p, lam_p, sub, lam_init)
            q_s, k_s, v_s, sg_s = _attn_in(xs, ng, sc_s, sh_s, w_in, q_scale, tables, BF16)
            a_s = _diff_attn(q_s, k_s, v_s, sg_s, lam_p, sub, lam_init,
                             past=(cache_k4, cache_v4, j))
        xp = _out_proj(a_p, w_out, xp, gt_p, fg)
        if a_s is None:
            xs = _seq_dft_out(dmat_s, y_s, sg_s, w_out, xs, gt_s)
        else:
            xs = _out_proj(a_s, w_out, xs, gt_s, fg)
    if split_dft:
        xs = _from_residue_major(xs, 1)
    return (xp, xs, jnp.stack(new_k, axis=1), jnp.stack(new_v, axis=1))
```

```python
import functools
import math

import jax
import jax.numpy as jnp
from jax import lax
from jax.experimental import pallas as pl
from jax.experimental.pallas import tpu as pltpu

N_HEADS = 8
N_FOURIER_GROUPS = 8
GRID_W = 64
ROPE_BASE = 10000.0
EPS = 1e-6
N_MIXERS = 2

V7X_VMEM_BYTES = 64 * 1024 * 1024
VMEM_LIMIT_BYTES = V7X_VMEM_BYTES - 8 * 1024 * 1024
LANES = 128
SUBLANES = 8

BF16 = jnp.bfloat16
F32 = jnp.float32


def _tile(n, pref):
    t = min(n, pref)
    assert n % t == 0, (n, t)
    return t


def _params(n_axes):
    return pltpu.CompilerParams(
        dimension_semantics=("arbitrary",) * n_axes,
        vmem_limit_bytes=VMEM_LIMIT_BYTES,
    )


def _resident(block_shape, index_map):
    return pl.BlockSpec(block_shape, index_map, pipeline_mode=pl.Buffered(1))


def _layer_weights(w):
    stack, layer = w
    return stack.shape[1:], _resident((None,) + stack.shape[1:], lambda *_: (layer, 0, 0)), stack


def _to_bf16_kernel(x_ref, o_ref):
    o_ref[...] = x_ref[...].astype(BF16)


def _to_bf16(w):
    l, r, c = w.shape
    tr = _tile(r, max(8, (4 * 1024 * 1024) // (4 * c)))
    return pl.pallas_call(
        _to_bf16_kernel,
        grid=(l, r // tr),
        in_specs=[pl.BlockSpec((None, tr, c), lambda li, i: (li, i, 0))],
        out_specs=pl.BlockSpec((None, tr, c), lambda li, i: (li, i, 0)),
        out_shape=jax.ShapeDtypeStruct(w.shape, BF16),
        compiler_params=_params(2),
        name="to_bf16",
    )(w)


def _adaln_kernel(cond_ref, w_ref, b_ref, out_ref):
    cond = cond_ref[...]
    act = cond * jax.nn.sigmoid(cond)
    out_ref[...] = jnp.dot(act, w_ref[...], preferred_element_type=F32,
                           precision=lax.Precision.HIGHEST) + b_ref[...]


def _adaln(cond, w_ada, b_ada):
    depth, d, d3 = w_ada.shape
    r = cond.shape[0]
    tn = _tile(d3, 1024)
    return pl.pallas_call(
        _adaln_kernel,
        grid=(depth, d3 // tn),
        in_specs=[
            pl.BlockSpec((r, d), lambda i, j: (0, 0)),
            pl.BlockSpec((None, d, tn), lambda i, j: (i, 0, j)),
            pl.BlockSpec((None, 1, tn), lambda i, j: (i, 0, j)),
        ],
        out_specs=pl.BlockSpec((None, r, tn), lambda i, j: (i, 0, j)),
        out_shape=jax.ShapeDtypeStruct((depth, r, d3), F32),
        compiler_params=_params(2),
        name="adaln",
    )(cond, w_ada, b_ada.reshape(depth, 1, d3))


def _modulated_norm(x, ng, sc, sh):
    ms = jnp.mean(x * x, axis=-1, keepdims=True)
    h = (x * lax.rsqrt(ms + EPS)) * (ng * (1.0 + sc)) + sh
    return h.astype(BF16)


def _silu(x):
    return x * jax.nn.sigmoid(x)


def _fourier_in_kernel(x_ref, ng_ref, sc_ref, sh_ref, w_ref, cs_ref, ab_ref, sg_ref, u_ref, *, e, gd):
    hb = _modulated_norm(x_ref[...], ng_ref[...], sc_ref[...], sh_ref[...])
    tn = _tile(e, 512)
    for j in range(e // tn):
        cols = slice(j * tn, (j + 1) * tn)
        u_ref[:, cols] = jnp.dot(hb, w_ref[:, cols], preferred_element_type=F32).astype(BF16)
    for grp in range(e // gd):
        cols = slice(grp * gd, (grp + 1) * gd)
        ab = jnp.dot(u_ref[:, cols], cs_ref[...], preferred_element_type=F32)
        ab_ref[0, :, cols] = ab[:, :gd].astype(BF16)
        ab_ref[1, :, cols] = ab[:, gd:].astype(BF16)
    for j in range(e // tn):
        cols = slice(j * tn, (j + 1) * tn)
        g = jnp.dot(hb, w_ref[:, e + j * tn:e + (j + 1) * tn], preferred_element_type=F32)
        sg_ref[:, cols] = _silu(g).astype(BF16)


def _fourier_in(x, ng, sc, sh, w_in, cs):
    b, n, d = x.shape
    w_shape, w_spec, w_in = _layer_weights(w_in)
    e = w_shape[1] // 2
    gd = e // N_FOURIER_GROUPS
    tm = _tile(n, 512)
    return pl.pallas_call(
        functools.partial(_fourier_in_kernel, e=e, gd=gd),
        grid=(b, n // tm),
        in_specs=[
            pl.BlockSpec((None, tm, d), lambda bi, i: (bi, i, 0)),
            _resident((1, d), lambda bi, i: (0, 0)),
            pl.BlockSpec((None, 1, d), lambda bi, i: (bi, 0, 0)),
            pl.BlockSpec((None, 1, d), lambda bi, i: (bi, 0, 0)),
            w_spec,
            _resident((gd, 2 * gd), lambda bi, i: (0, 0)),
        ],
        out_specs=[
            pl.BlockSpec((None, 2, tm, e), lambda bi, i: (bi, 0, i, 0)),
            pl.BlockSpec((None, tm, e), lambda bi, i: (bi, i, 0)),
        ],
        out_shape=[
            jax.ShapeDtypeStruct((b, 2, n, e), BF16),
            jax.ShapeDtypeStruct((b, n, e), BF16),
        ],
        scratch_shapes=[pltpu.VMEM((tm, e), BF16)],
        compiler_params=_params(2),
        name="fourier_in",
    )(x, ng, sc, sh, w_in, cs)


def _seq_dft_kernel(d_ref, ab_ref, sg_ref, out_ref, acc_ref):
    k = pl.program_id(2)
    part = jnp.dot(d_ref[...], ab_ref[...], preferred_element_type=F32)

    @pl.when(k == 0)
    def _():
        acc_ref[...] = part

    @pl.when(k > 0)
    def _():
        acc_ref[...] += part

    @pl.when(k == pl.num_programs(2) - 1)
    def _():
        out_ref[...] = (acc_ref[...] * sg_ref[...].astype(F32)).astype(BF16)


def _seq_dft(dmat, ab, sg):
    b, n2, e = ab.shape
    n = n2 // 2
    tm = _tile(n, 1024)
    tk = _tile(n2, 1024)
    return pl.pallas_call(
        _seq_dft_kernel,
        grid=(b, n // tm, n2 // tk),
        in_specs=[
            pl.BlockSpec((tm, tk), lambda bi, i, k: (i, k)),
            pl.BlockSpec((None, tk, e), lambda bi, i, k: (bi, k, 0)),
            pl.BlockSpec((None, tm, e), lambda bi, i, k: (bi, i, 0)),
        ],
        out_specs=pl.BlockSpec((None, tm, e), lambda bi, i, k: (bi, i, 0)),
        out_shape=jax.ShapeDtypeStruct((b, n, e), BF16),
        scratch_shapes=[pltpu.VMEM((tm, e), F32)],
        compiler_params=_params(3),
        name="seq_dft",
    )(dmat, ab, sg)


SEQ_RADIX = 8


def _twiddle(z, j, r):
    x, y = z
    if j == 0:
        return z
    if 4 * j == r:
        return y, -x
    if 8 * j == r:
        return (x + y) * math.sqrt(0.5), (y - x) * math.sqrt(0.5)
    if 8 * j == 3 * r:
        return (y - x) * math.sqrt(0.5), (-x - y) * math.sqrt(0.5)
    c, sn = math.cos(2.0 * math.pi * j / r), math.sin(2.0 * math.pi * j / r)
    return x * c + y * sn, y * c - x * sn


def _dft_blocks(z):
    r = len(z)
    if r == 1:
        return z
    h = r // 2
    even = _dft_blocks([(z[j][0] + z[j + h][0], z[j][1] + z[j + h][1]) for j in range(h)])
    odd = _dft_blocks([_twiddle((z[j][0] - z[j + h][0], z[j][1] - z[j + h][1]), j, r) for j in range(h)])
    out = [None] * r
    out[0::2], out[1::2] = even, odd
    return out


def _fourier_in_rdx_kernel(x_ref, ng_ref, sc_ref, sh_ref, w_ref, cs_ref, y_ref, sg_ref, u_ref, *, e, gd, radix, res):
    hb = _modulated_norm(x_ref[...], ng_ref[...], sc_ref[...], sh_ref[...])
    rows = x_ref.shape[0] // res
    q = rows // radix
    tn = _tile(e, 512)
    for j in range(e // tn):
        cols = slice(j * tn, (j + 1) * tn)
        u_ref[:, cols] = jnp.dot(hb, w_ref[:, cols], preferred_element_type=F32).astype(BF16)
    for grp in range(e // gd):
        cols = slice(grp * gd, (grp + 1) * gd)
        ab = jnp.dot(u_ref[:, cols], cs_ref[...], preferred_element_type=F32)
        for t in range(res):
            blk = [slice(t * rows + j * q, t * rows + (j + 1) * q) for j in range(radix)]
            y = _dft_blocks([(ab[b, :gd], -ab[b, gd:]) for b in blk])
            for kappa in range(radix):
                y_ref[kappa, 0, t * q:(t + 1) * q, cols] = y[kappa][0].astype(BF16)
                y_ref[kappa, 1, t * q:(t + 1) * q, cols] = (-y[kappa][1]).astype(BF16)
    for j in range(e // tn):
        cols = slice(j * tn, (j + 1) * tn)
        g = jnp.dot(hb, w_ref[:, e + j * tn:e + (j + 1) * tn], preferred_element_type=F32)
        sg_ref[:, cols] = _silu(g).astype(BF16)


def _fourier_in_rdx(x, ng, sc, sh, w_in, cs):
    b, n, d = x.shape
    w_shape, w_spec, w_in = _layer_weights(w_in)
    e = w_shape[1] // 2
    gd = e // N_FOURIER_GROUPS
    r = SEQ_RADIX
    res = 2
    tm = res * n // r
    return pl.pallas_call(
        functools.partial(_fourier_in_rdx_kernel, e=e, gd=gd, radix=r, res=res),
        grid=(b, r // res),
        in_specs=[
            pl.BlockSpec((None, tm, d), lambda bi, i: (bi, i, 0)),
            _resident((1, d), lambda bi, i: (0, 0)),
            pl.BlockSpec((None, 1, d), lambda bi, i: (bi, 0, 0)),
            pl.BlockSpec((None, 1, d), lambda bi, i: (bi, 0, 0)),
            w_spec,
            _resident((gd, 2 * gd), lambda bi, i: (0, 0)),
        ],
        out_specs=[
            pl.BlockSpec((None, r, 2, tm // r, e), lambda bi, i: (bi, 0, 0, i, 0)),
            pl.BlockSpec((None, tm, e), lambda bi, i: (bi, i, 0)),
        ],
        out_shape=[
            jax.ShapeDtypeStruct((b, r, 2, n // r, e), BF16),
            jax.ShapeDtypeStruct((b, n, e), BF16),
        ],
        scratch_shapes=[pltpu.VMEM((tm, e), BF16)],
        compiler_params=_params(2),
        name="fourier_in_rdx",
    )(x, ng, sc, sh, w_in, cs)


def _seq_dft_rdx_kernel(d_ref, y_ref, sg_ref, out_ref):
    tn = _tile(out_ref.shape[1], 512)
    for j in range(out_ref.shape[1] // tn):
        cols = slice(j * tn, (j + 1) * tn)
        f = jnp.dot(d_ref[...], y_ref[:, cols], preferred_element_type=F32)
        out_ref[:, cols] = (f * sg_ref[:, cols].astype(F32)).astype(BF16)


def _seq_dft_rdx(dmat, y, sg):
    b, r, n2, e = y.shape
    nr = n2 // 2
    return pl.pallas_call(
        _seq_dft_rdx_kernel,
        grid=(r, b),
        in_specs=[
            pl.BlockSpec((None, nr, n2), lambda kp, bi: (kp, 0, 0)),
            pl.BlockSpec((None, None, n2, e), lambda kp, bi: (bi, kp, 0, 0)),
            pl.BlockSpec((None, nr, e), lambda kp, bi: (bi, kp, 0)),
        ],
        out_specs=pl.BlockSpec((None, nr, e), lambda kp, bi: (bi, kp, 0)),
        out_shape=jax.ShapeDtypeStruct((b, r * nr, e), BF16),
        compiler_params=_params(2),
        name="seq_dft_rdx",
    )(dmat, y, sg)


def _seq_dft_out_kernel(d_ref, y_ref, sg_ref, w_ref, x_ref, gt_ref, out_ref, a_ref):
    tn = _tile(a_ref.shape[1], 512)
    for j in range(a_ref.shape[1] // tn):
        cols = slice(j * tn, (j + 1) * tn)
        f = jnp.dot(d_ref[...], y_ref[:, cols], preferred_element_type=F32)
        a_ref[:, cols] = (f * sg_ref[:, cols].astype(F32)).astype(BF16)
    out_ref[...] = x_ref[...] + gt_ref[...] * jnp.dot(a_ref[...], w_ref[...], preferred_element_type=F32)


def _seq_dft_out(dmat, y, sg, w_out, x, gate):
    b, r, n2, e = y.shape
    nr = n2 // 2
    d = x.shape[2]
    _, w_spec, w_out = _layer_weights(w_out)
    return pl.pallas_call(
        _seq_dft_out_kernel,
        grid=(r, b),
        in_specs=[
            pl.BlockSpec((None, nr, n2), lambda kp, bi: (kp, 0, 0)),
            pl.BlockSpec((None, None, n2, e), lambda kp, bi: (bi, kp, 0, 0)),
            pl.BlockSpec((None, nr, e), lambda kp, bi: (bi, kp, 0)),
            w_spec,
            pl.BlockSpec((None, nr, d), lambda kp, bi: (bi, kp, 0)),
            pl.BlockSpec((None, 1, d), lambda kp, bi: (bi, 0, 0)),
        ],
        out_specs=pl.BlockSpec((None, nr, d), lambda kp, bi: (bi, kp, 0)),
        out_shape=jax.ShapeDtypeStruct(x.shape, F32),
        scratch_shapes=[pltpu.VMEM((nr, e), BF16)],
        compiler_params=_params(2),
        name="seq_dft_out",
    )(dmat, y, sg, w_out, x, gate)


def _seq_dft_rdx_matrices(n):
    r = SEQ_RADIX
    kk = r * jnp.arange(n // r, dtype=jnp.int32)[None, :, None] + jnp.arange(r, dtype=jnp.int32)[:, None, None]
    rho = jnp.arange(r, dtype=jnp.int32)[None, None, :]
    mr = r * jnp.arange(n // (r * r), dtype=jnp.int32)[None, None, :]
    ang_a = ((kk * mr) % n).astype(F32) * (2.0 * math.pi / n)
    ang_b = ((kk * rho) % n).astype(F32) * (2.0 * math.pi / n)
    ca, sa = jnp.cos(ang_a)[:, :, None, :], jnp.sin(ang_a)[:, :, None, :]
    cb, sb = jnp.cos(ang_b)[:, :, :, None], jnp.sin(ang_b)[:, :, :, None]
    c = (ca * cb - sa * sb).reshape(r, n // r, n // r)
    s = (sa * cb + ca * sb).reshape(r, n // r, n // r)
    return (jnp.concatenate([c, -s], axis=2) * (n ** -0.5)).astype(BF16)


def _to_residue_major(x, axis):
    n, r = x.shape[axis], SEQ_RADIX
    shape = x.shape[:axis] + (n // r, r) + x.shape[axis + 1:]
    return jnp.swapaxes(x.reshape(shape), axis, axis + 1).reshape(x.shape)


def _from_residue_major(x, axis):
    n, r = x.shape[axis], SEQ_RADIX
    shape = x.shape[:axis] + (r, n // r) + x.shape[axis + 1:]
    return jnp.swapaxes(x.reshape(shape), axis, axis + 1).reshape(x.shape)


def _dft_cos_sin(n):
    k = jnp.arange(n, dtype=jnp.int32)
    if n >= 1024 and n % 64 == 0:
        m1 = jnp.arange(n // 64, dtype=jnp.int32) * 64
        m2 = jnp.arange(64, dtype=jnp.int32)
        ang_a = ((k[:, None] * m1[None, :]) % n).astype(F32) * (2.0 * math.pi / n)
        ang_b = ((k[:, None] * m2[None, :]) % n).astype(F32) * (2.0 * math.pi / n)
        ca, sa = jnp.cos(ang_a)[:, :, None], jnp.sin(ang_a)[:, :, None]
        cb, sb = jnp.cos(ang_b)[:, None, :], jnp.sin(ang_b)[:, None, :]
        return (ca * cb - sa * sb).reshape(n, n), (sa * cb + ca * sb).reshape(n, n)
    ang = ((k[:, None] * k[None, :]) % n).astype(F32) * (2.0 * math.pi / n)
    return jnp.cos(ang), jnp.sin(ang)


def _seq_dft_matrix(n):
    c, s = _dft_cos_sin(n)
    return (jnp.concatenate([c, -s], axis=1) * (n ** -0.5)).astype(BF16)


def _chan_dft_matrix(gd):
    c, s = _dft_cos_sin(gd)
    return (jnp.concatenate([c, s], axis=1) * (gd ** -0.5)).astype(BF16)


def _rope(x, cos, sin_signed, first_half):
    fwd = pltpu.roll(x, LANES - 32, axis=1)
    bwd = pltpu.roll(x, 32, axis=1)
    return x * cos + jnp.where(first_half, fwd, bwd) * sin_signed


def _attn_in_kernel(*refs, e, hd, rope, q_scale):
    if rope:
        (x_ref, ng_ref, sc_ref, sh_ref, w_ref, wvt_ref, cq_ref, sq_ref, ck_ref, sk_ref,
         q_ref, k_ref, v_ref, sg_ref) = refs
        lane = lax.broadcasted_iota(jnp.int32, (1, LANES), 1)
        first_half = (lane % 64) < 32
    else:
        x_ref, ng_ref, sc_ref, sh_ref, w_ref, q_ref, k_ref, v_ref, sg_ref = refs
    hb = _modulated_norm(x_ref[...], ng_ref[...], sc_ref[...], sh_ref[...])
    hw = 2 * hd
    for j in range(e // hw):
        q = jnp.dot(hb, w_ref[:, j * hw:(j + 1) * hw], preferred_element_type=F32)
        k = jnp.dot(hb, w_ref[:, e + j * hw:e + (j + 1) * hw], preferred_element_type=F32)
        for mp in range(2):
            cols = slice(j * hw + mp * hd, j * hw + (mp + 1) * hd)
            qm, km = q[:, mp * hd:(mp + 1) * hd], k[:, mp * hd:(mp + 1) * hd]
            if rope:
                qm = _rope(qm, cq_ref[...], sq_ref[...], first_half)
                km = _rope(km, ck_ref[...], sk_ref[...], first_half)
            else:
                qm = qm * q_scale
            q_ref[:, cols] = qm.astype(q_ref.dtype)
            k_ref[:, cols] = km.astype(k_ref.dtype)
    tn = _tile(e, 512)
    for j in range(e // tn):
        cols = slice(j * tn, (j + 1) * tn)
        if rope:
            vt = lax.dot_general(wvt_ref[cols, :], hb, (((1,), (1,)), ((), ())), preferred_element_type=F32)
            v_ref[cols, :] = vt.astype(v_ref.dtype)
        else:
            v = jnp.dot(hb, w_ref[:, 2 * e + j * tn:2 * e + (j + 1) * tn], preferred_element_type=F32)
            v_ref[:, cols] = v.astype(v_ref.dtype)
        g = jnp.dot(hb, w_ref[:, 3 * e + j * tn:3 * e + (j + 1) * tn], preferred_element_type=F32)
        sg_ref[:, cols] = _silu(g).astype(BF16)


def _attn_in(x, ng, sc, sh, w_in, q_scale, rope_tables, kv_dtype):
    b, n, d = x.shape
    layer = w_in[1]
    w_shape, w_spec, w_in = _layer_weights(w_in)
    e = w_shape[1] // 4
    hd = e // (2 * N_HEADS)
    assert hd == LANES
    tm = _tile(n, 512)
    rope = rope_tables is not None
    in_specs = [
        pl.BlockSpec((None, tm, d), lambda bi, i: (bi, i, 0)),
        _resident((1, d), lambda bi, i: (0, 0)),
        pl.BlockSpec((None, 1, d), lambda bi, i: (bi, 0, 0)),
        pl.BlockSpec((None, 1, d), lambda bi, i: (bi, 0, 0)),
        w_spec,
    ]
    args = [x, ng, sc, sh, w_in]
    tok = pl.BlockSpec((None, tm, e), lambda bi, i: (bi, i, 0))
    v_spec, v_shape = tok, jax.ShapeDtypeStruct((b, n, e), kv_dtype)
    if rope:
        in_specs += [_resident((e, d), lambda bi, i: (0, 0))]
        in_specs += [pl.BlockSpec((tm, hd), lambda bi, i: (i, 0))] * 4
        args += [w_in[layer, :, 2 * e:3 * e].T] + list(rope_tables)
        v_spec = pl.BlockSpec((None, None, e, tm), lambda bi, i: (bi, i, 0, 0))
        v_shape = jax.ShapeDtypeStruct((b, n // tm, e, tm), kv_dtype)
    return pl.pallas_call(
        functools.partial(_attn_in_kernel, e=e, hd=hd, rope=rope, q_scale=q_scale),
        grid=(b, n // tm),
        in_specs=in_specs,
        out_specs=[tok, tok, v_spec, tok],
        out_shape=[
            jax.ShapeDtypeStruct((b, n, e), BF16),
            jax.ShapeDtypeStruct((b, n, e), kv_dtype),
            v_shape,
            jax.ShapeDtypeStruct((b, n, e), BF16),
        ],
        compiler_params=_params(2),
        name="attn_in_rope" if rope else "attn_in",
    )(*args)


def _lane_blocks(x):
    return [x[:, j * LANES:(j + 1) * LANES] for j in range(x.shape[1] // LANES)]


def _fold_rows(op, x):
    return op(x.reshape(x.shape[0] // SUBLANES, SUBLANES, x.shape[1]), axis=0)


def _diff_attn_kernel(*refs, n_new, n_past, ck, hd, lam_init, n_tiles, heads, vt_in):
    if n_past:
        (q_ref, qn_ref, k_ref, v_ref, pk_ref, pv_ref, sg_ref, lam_ref, sub_ref, out_ref,
         s_ref, p_ref, mcur_ref, vt_ref, r_ref, il_ref, o_ref) = refs
    else:
        (q_ref, qn_ref, k_ref, v_ref, sg_ref, lam_ref, sub_ref, out_ref,
         s_ref, p_ref, mcur_ref, vt_ref, r_ref, il_ref, o_ref) = refs
    n_new_chunks = n_new // ck
    tq = q_ref.shape[0]
    i = pl.program_id(2)
    first = i == 0
    last = i == n_tiles - 1
    pipelined = n_tiles > 1
    cold = first if not pipelined else jnp.logical_and(
        first, jnp.logical_and(pl.program_id(0) == 0, pl.program_id(1) == 0))
    hw = 2 * hd

    def run_head(col0):
        hcols = slice(col0, col0 + hw)

        def rows_of(c):
            if isinstance(c, int):
                return slice(c * ck, (c + 1) * ck)
            return pl.ds(pl.multiple_of(c * ck, ck), ck)

        def new_chunk(c):
            vt_src = v_ref if vt_in else vt_ref
            return (lambda mp: (mp, c)), (lambda: k_ref[rows_of(c), hcols].astype(BF16)), (lambda: vt_src[c])

        past_chunk = ((lambda mp: (mp, n_new_chunks, slice(0, n_past), slice(None))),
                      (lambda: pk_ref[:, hcols].astype(BF16)),
                      (lambda: vt_ref[n_new_chunks, :, 0:n_past]))

        def transposed(src_ref):
            return src_ref[:, hcols].astype(F32).T.astype(BF16)

        def scores_chunk(qt, chunk, mrun):
            at, load_keys, _ = chunk
            kc = load_keys()
            out = []
            for mp in range(2):
                cols = slice(mp * hd, (mp + 1) * hd)
                s = jnp.dot(kc[:, cols], qt[cols, :], preferred_element_type=F32)
                s_ref[at(mp)] = s
                out.append(jnp.maximum(mrun[mp], _fold_rows(jnp.max, s)))
            return tuple(out)

        def probs_chunk(chunk, lsum):
            at = chunk[0]
            out = []
            for mp in range(2):
                s = s_ref[at(mp)]
                s3 = s.reshape(s.shape[0] // SUBLANES, SUBLANES, tq)
                p3 = jnp.exp2(s3 - mcur_ref[mp][None])
                p_ref[at(mp)] = p3.reshape(s.shape).astype(BF16)
                out.append(lsum[mp] + jnp.sum(p3, axis=0))
            return tuple(out)

        def values_chunk(chunk):
            at, _, load_vt = chunk
            a = p_ref[at(0)] - r_ref[0:1, :].astype(BF16) * p_ref[at(1)]
            return jnp.dot(load_vt(), a, preferred_element_type=F32)

        def finish_stats(carry):
            l1 = jnp.sum(carry[0], axis=0, keepdims=True)
            l2 = jnp.sum(carry[1], axis=0, keepdims=True)
            lam_p = lam_ref[...]
            lam = (jnp.exp(jnp.sum(lam_p[0:1] * lam_p[1:2], axis=-1, keepdims=True))
                   - jnp.exp(jnp.sum(lam_p[2:3] * lam_p[3:4], axis=-1, keepdims=True)) + lam_init)
            r_ref[...] = jnp.broadcast_to(lam * l1 / l2, (SUBLANES, tq))
            il_ref[...] = jnp.broadcast_to(1.0 / l1, (SUBLANES, tq))
            if pipelined:
                for mp in range(2):
                    m = jnp.max(carry[2 + mp], axis=0, keepdims=True)
                    mcur_ref[mp] = jnp.broadcast_to(m, (SUBLANES, tq))

        def write_tile(o, tile):
            rows = pl.ds(pl.multiple_of(tile * tq, tq), tq)
            o = o * il_ref[0:1, :]
            o = o * lax.rsqrt(jnp.mean(o * o, axis=0, keepdims=True) + EPS)
            o = jnp.concatenate([blk * sub_ref[...] for blk in _lane_blocks(o)], axis=1)
            out_ref[rows, hcols] = ((o.T * (1.0 - lam_init)) * sg_ref[rows, hcols].astype(F32)).astype(BF16)

        neg = jnp.full((SUBLANES, tq), -jnp.inf, F32)
        zero = jnp.zeros((SUBLANES, tq), F32)

        def stage_loop(with_values):
            def body(chunks, carry):
                qt = transposed(qn_ref) if pipelined else None
                if with_values:
                    o_ref[...] += functools.reduce(jnp.add, [values_chunk(ch) for ch in chunks])
                for ch in chunks:
                    lsum = probs_chunk(ch, carry[:2])
                    mrun = scores_chunk(qt, ch, carry[2:]) if pipelined else carry[2:]
                    carry = lsum + mrun
                return carry
            chunks = [new_chunk(c) for c in range(n_new_chunks)] + ([past_chunk] if n_past else [])
            return body(chunks, (zero, zero, neg, neg))

        @pl.when(cold)
        def _():
            qt = transposed(q_ref)
            mrun = lax.fori_loop(0, n_new_chunks, lambda c, m: scores_chunk(qt, new_chunk(c), m), (neg, neg))
            if n_past:
                mrun = scores_chunk(qt, past_chunk, mrun)
            for mp in range(2):
                mcur_ref[mp] = jnp.broadcast_to(jnp.max(mrun[mp], axis=0, keepdims=True), (SUBLANES, tq))

        @pl.when(first)
        def _():
            if not vt_in:
                for c in range(n_new_chunks):
                    vt_ref[c] = v_ref[rows_of(c), hcols].astype(F32).T.astype(BF16)
            if n_past:
                vt_ref[n_new_chunks, :, 0:n_past] = pv_ref[:, hcols].astype(F32).T.astype(BF16)
            finish_stats(stage_loop(with_values=False))

        if pipelined:
            @pl.when(jnp.logical_not(first))
            def _():
                o_ref[...] = jnp.zeros(o_ref.shape, F32)
                carry = stage_loop(with_values=True)
                write_tile(o_ref[...], i - 1)
                finish_stats(carry)

        @pl.when(last)
        def _():
            o = None
            for c in range(n_new_chunks):
                d = values_chunk(new_chunk(c))
                o = d if o is None else o + d
            if n_past:
                o = o + values_chunk(past_chunk)
            write_tile(o, i)

    for hh in range(heads):
        run_head(hh * hw)


def _diff_attn(q, k, v, sg, lam_p, subln, lam_init, past=None):
    b, n, e = q.shape
    hw = e // N_HEADS
    hd = hw // 2
    tq = _tile(n, 512)
    ck = _tile(n, 512)
    n_tiles = n // tq
    vt_in = v.ndim == 4
    assert not vt_in or v.shape[3] == ck
    n_past = 0 if past is None else past[0].shape[2]
    assert n_past <= ck and n_past % LANES == 0
    n_chunks = n // ck + (1 if n_past else 0)
    heads = N_HEADS if n_tiles == 1 else 1
    bw = heads * hw
    tok = pl.BlockSpec((None, tq, bw), lambda bi, h, i: (bi, i, h))
    head = pl.BlockSpec((None, n, bw), lambda bi, h, i: (bi, 0, h))

    def ahead(bi, h, i):
        if n_tiles == 1:
            return bi, h, i
        wrap = i == n_tiles - 1
        g = jnp.minimum(bi * N_HEADS + h + 1, b * N_HEADS - 1)
        return (jnp.where(wrap, g // N_HEADS, bi), jnp.where(wrap, g % N_HEADS, h),
                jnp.where(wrap, 0, i + 1))

    def tok_ahead(bi, h, i):
        nb, nh, ni = ahead(bi, h, i)
        return nb, ni, nh

    def head_ahead(bi, h, i):
        nb, nh, _ = ahead(bi, h, i)
        return nb, 0, nh

    v_spec = pl.BlockSpec((None, n // ck, bw, ck), lambda bi, h, i: (bi, 0, h, 0)) if vt_in else head
    in_specs = [tok, pl.BlockSpec((None, tq, bw), tok_ahead), pl.BlockSpec((None, n, bw), head_ahead), v_spec]
    args = [q, q, k, v]
    scratch = [
        pltpu.VMEM((2, n_chunks, ck, tq), F32),
        pltpu.VMEM((2, n_chunks, ck, tq), BF16),
        pltpu.VMEM((2, SUBLANES, tq), F32),
        pltpu.VMEM((n_chunks, hw, ck), BF16),
        pltpu.VMEM((SUBLANES, tq), F32),
        pltpu.VMEM((SUBLANES, tq), F32),
        pltpu.VMEM((hw, tq), F32),
    ]
    if n_past:
        cache_k, cache_v, layer = past
        cached = pl.BlockSpec((None, None, n_past, bw), lambda bi, h, i: (bi, layer, 0, h))

        def cached_ahead(bi, h, i):
            nb, nh, _ = ahead(bi, h, i)
            return nb, layer, 0, nh

        in_specs += [pl.BlockSpec((None, None, n_past, bw), cached_ahead), cached]
        args += [cache_k, cache_v]
    in_specs += [head,
                 _resident((4, hd), lambda bi, h, i: (0, 0)),
                 _resident((hw, LANES), lambda bi, h, i: (0, 0))]
    args += [sg, lam_p, jnp.broadcast_to(subln.reshape(hw, 1), (hw, LANES))]
    return pl.pallas_call(
        functools.partial(_diff_attn_kernel, n_new=n, n_past=n_past, ck=ck, hd=hd, lam_init=lam_init,
                          n_tiles=n_tiles, heads=heads, vt_in=vt_in),
        grid=(b, N_HEADS // heads, n_tiles),
        in_specs=in_specs,
        out_specs=head,
        out_shape=jax.ShapeDtypeStruct((b, n, e), BF16),
        scratch_shapes=scratch,
        compiler_params=_params(3),
        name="diff_attn_past" if n_past else "diff_attn",
    )(*args)


def _rope_tables(n, hd, q_scale):
    axis_dim = hd // 2
    rows = n // GRID_W
    row = jnp.broadcast_to(jnp.arange(rows, dtype=F32)[:, None], (rows, GRID_W)).reshape(-1)
    colp = jnp.broadcast_to(jnp.arange(GRID_W, dtype=F32)[None, :], (rows, GRID_W)).reshape(-1)
    inv = ROPE_BASE ** (-jnp.arange(0, axis_dim, 2, dtype=F32) / axis_dim)
    ar = row[:, None] * inv[None, :]
    ac = colp[:, None] * inv[None, :]
    ang = jnp.concatenate([ar, ar, ac, ac], axis=-1)
    cos, sin = jnp.cos(ang), jnp.sin(ang)
    half = axis_dim // 2
    sign = jnp.where((jnp.arange(hd) % axis_dim) < half, -1.0, 1.0).astype(F32)
    sin_signed = sin * sign[None, :]
    return cos * q_scale, sin_signed * q_scale, cos, sin_signed


def _out_proj_kernel(*refs, final):
    if final:
        a_ref, w_ref, x_ref, gt_ref, fg_ref, y_ref = refs
    else:
        a_ref, w_ref, x_ref, gt_ref, y_ref = refs
    y = x_ref[...] + gt_ref[...] * jnp.dot(a_ref[...], w_ref[...], preferred_element_type=F32)
    if final:
        y = (y * lax.rsqrt(jnp.mean(y * y, axis=-1, keepdims=True) + EPS)) * fg_ref[...]
    y_ref[...] = y


def _out_proj(a, w_out, x, gate, final_g=None):
    b, n, d = x.shape
    e = a.shape[2]
    tm = _tile(n, 512)
    final = final_g is not None
    _, w_spec, w_out = _layer_weights(w_out)
    in_specs = [
        pl.BlockSpec((None, tm, e), lambda bi, i: (bi, i, 0)),
        w_spec,
        pl.BlockSpec((None, tm, d), lambda bi, i: (bi, i, 0)),
        pl.BlockSpec((None, 1, d), lambda bi, i: (bi, 0, 0)),
    ]
    args = [a, w_out, x, gate]
    if final:
        in_specs.append(_resident((1, d), lambda bi, i: (0, 0)))
        args.append(final_g)
    return pl.pallas_call(
        functools.partial(_out_proj_kernel, final=final),
        grid=(b, n // tm),
        in_specs=in_specs,
        out_specs=pl.BlockSpec((None, tm, d), lambda bi, i: (bi, i, 0)),
        out_shape=jax.ShapeDtypeStruct((b, n, d), F32),
        compiler_params=_params(2),
        name="out_proj_final" if final else "out_proj",
    )(*args)


def kernel(x_prompt, x_sample, cache_k, cache_v, c, c_ctx, norm_g, w_ada, b_ada, w_in_fourier, w_out_fourier, w_in_attn, w_out_attn, lam_q1, lam_k1, lam_q2, lam_k2, subln_g, final_norm_g):
    depth, d = norm_g.shape
    bp, n_ctx, _ = x_prompt.shape
    bs, n_lat, _ = x_sample.shape
    past_len = cache_k.shape[2]
    e = w_out_attn.shape[1]
    hd = e // (2 * N_HEADS)
    gd = e // N_FOURIER_GROUPS

    n_cond = 1 + bs
    rows = -(-n_cond // 8) * 8
    cond = jnp.concatenate([c_ctx[None, :], c, jnp.zeros((rows - n_cond, d), F32)], axis=0)
    mods = _adaln(cond, w_ada, b_ada).reshape(depth, rows, 3, d)

    q_scale = hd ** -0.5 * math.log2(math.e)
    tables = _rope_tables(n_lat, hd, q_scale)
    cs = _chan_dft_matrix(gd)
    dmat_p = _seq_dft_matrix(n_ctx)
    cache_k4 = cache_k.astype(BF16).reshape(bs, cache_k.shape[1], past_len, e)
    cache_v4 = cache_v.astype(BF16).reshape(bs, cache_v.shape[1], past_len, e)

    w_in_fourier, w_out_fourier = _to_bf16(w_in_fourier), _to_bf16(w_out_fourier)
    w_in_attn, w_out_attn = _to_bf16(w_in_attn), _to_bf16(w_out_attn)

    xp, xs = x_prompt, x_sample
    split_dft = n_lat % (16 * SEQ_RADIX ** 2) == 0
    if split_dft:
        xs = _to_residue_major(xs, 1)
        tables = tuple(_to_residue_major(t, 0) for t in tables)
        dmat_s = _seq_dft_rdx_matrices(n_lat)
    else:
        dmat_s = _seq_dft_matrix(n_lat)
    new_k, new_v = [], []
    for i in range(depth):
        ng = norm_g[i][None, :]
        sh_p, sc_p, gt_p = (jnp.broadcast_to(mods[i, 0:1, t][:, None, :], (bp, 1, d)) for t in range(3))
        sh_s, sc_s, gt_s = (mods[i, 1:1 + bs, t][:, None, :] for t in range(3))
        j = i // N_MIXERS
        last = i == depth - 1
        fg = final_norm_g[None, :] if last else None
        if i % N_MIXERS == 0:
            w_in = (w_in_fourier, j)
            w_out = (w_out_fourier, j)
            ab_p, sg_p = _fourier_in(xp, ng, sc_p, sh_p, w_in, cs)
            a_p = _seq_dft(dmat_p, ab_p.reshape(bp, 2 * n_ctx, e), sg_p)
            if split_dft:
                y_s, sg_s = _fourier_in_rdx(xs, ng, sc_s, sh_s, w_in, cs)
                y_s = y_s.reshape(bs, SEQ_RADIX, 2 * n_lat // SEQ_RADIX, e)
                a_s = None if fg is None else _seq_dft_rdx(dmat_s, y_s, sg_s)
            else:
                ab_s, sg_s = _fourier_in(xs, ng, sc_s, sh_s, w_in, cs)
                a_s = _seq_dft(dmat_s, ab_s.reshape(bs, 2 * n_lat, e), sg_s)
        else:
            lam_init = 0.8 - 0.6 * math.exp(-0.3 * i)
            w_in = (w_in_attn, j)
            w_out = (w_out_attn, j)
            lam_p = jnp.stack([lam_q1[j], lam_k1[j], lam_q2[j], lam_k2[j]], axis=0)
            sub = subln_g[j][None, :]
            q_p, k_p, v_p, sg_p = (t.reshape(bp, n_ctx, e) for t in _attn_in(
                xp.reshape(1, bp * n_ctx, d), ng, sc_p[:1], sh_p[:1], w_in, q_scale, None, F32))
            new_k.append(k_p.reshape(bp, n_ctx, N_HEADS, 2 * hd))
            new_v.append(v_p.reshape(bp, n_ctx, N_HEADS, 2 * hd))
            a_p = _diff_attn(q_p, k_p, v_p, sg_p, lam_p, sub, lam_init)
            q_s, k_s, v_s, sg_s = _attn_in(xs, ng, sc_s, sh_s, w_in, q_scale, tables, BF16)
            a_s = _diff_attn(q_s, k_s, v_s, sg_s, lam_p, sub, lam_init,
                             past=(cache_k4, cache_v4, j))
        xp = _out_proj(a_p.reshape(1, bp * n_ctx, e), w_out, xp.reshape(1, bp * n_ctx, d), gt_p[:1],
                       fg).reshape(bp, n_ctx, d)
        if a_s is None:
            xs = _seq_dft_out(dmat_s, y_s, sg_s, w_out, xs, gt_s)
        else:
            xs = _out_proj(a_s, w_out, xs, gt_s, fg)
    if split_dft:
        xs = _from_residue_major(xs, 1)
    return (xp, xs, jnp.stack(new_k, axis=1), jnp.stack(new_v, axis=1))
```
